```python
import math
import jax, jax.numpy as jnp
from jax import lax
import numpy as np

D_MODEL = 2048
BATCH = 4
SEQ = 4096
DEPTH = 2

SB_HEADS = 8
SB_HEAD_DIM = 128
SB_WIDTH = SB_HEADS * SB_HEAD_DIM
HG_HEADS = 8
HG_DIM = 128
HG_WIDTH = HG_HEADS * HG_DIM
D_FF = 4 * D_MODEL
N_ADA = 6 * D_MODEL
Q_BLOCK = 128
HG_CHUNK = 32
EPS = 1e-6
IN_SIZES = [SB_WIDTH] * 3 + [HG_WIDTH] * 4 + [D_MODEL] * 2
N_IN = sum(IN_SIZES)
IN_SPLITS = [int(v) for v in np.cumsum(IN_SIZES)[:-1]]

kernel_name = "hybrid_stickbreak_hgrn2_adaln_block"


def rmsnorm(x, g):
    xf = x.astype(jnp.float32)
    y = xf * lax.rsqrt(jnp.mean(xf * xf, axis=-1, keepdims=True) + EPS)
    return (y * g.astype(jnp.float32)).astype(x.dtype)


def modulate(h, shift, scale):
    return h * (1.0 + scale[:, None, :]) + shift[:, None, :]


def stick_breaking_attention(q, k, v):
    B, T, H, d = q.shape
    nblk = T // Q_BLOCK
    qf = q.astype(jnp.float32).transpose(0, 2, 1, 3)
    kf = k.astype(jnp.float32).transpose(0, 2, 1, 3)
    vf = v.astype(jnp.float32).transpose(0, 2, 1, 3)
    q_blocks = qf.reshape(B, H, nblk, Q_BLOCK, d).transpose(2, 0, 1, 3, 4)
    starts = jnp.arange(nblk, dtype=jnp.int32) * Q_BLOCK
    scale = 1.0 / math.sqrt(d)
    spos = jnp.arange(T, dtype=jnp.int32)[None, :]

    def block(args):
        qb, start = args
        z = jnp.einsum('bhqd,bhkd->bhqk', qb, kf) * scale
        tpos = start + jnp.arange(Q_BLOCK, dtype=jnp.int32)[:, None]
        causal = spos < tpos
        log_not = jnp.where(causal, jax.nn.log_sigmoid(-z), 0.0)
        rest = lax.cumsum(log_not, axis=3, reverse=True) - log_not
        a = jnp.where(causal, jnp.exp(jax.nn.log_sigmoid(z) + rest), 0.0)
        return jnp.einsum('bhqk,bhkd->bhqd', a, vf)

    out = lax.map(block, (q_blocks, starts))
    return out.transpose(1, 0, 3, 2, 4).reshape(B, T, H * d)


def hgrn2_recurrence(q, k, v, logf):
    B, T, H, dk = q.shape
    dv = v.shape[-1]
    C = HG_CHUNK
    N = T // C

    def chunks(a):
        return a.reshape(B, N, C, H, a.shape[-1]).transpose(1, 0, 3, 2, 4)

    q, k, v, logf = chunks(q), chunks(k), chunks(v), chunks(logf)
    G = jnp.cumsum(logf, axis=3)
    G_last = G[:, :, :, -1:, :]
    qg = q * jnp.exp(G)
    kg = k * jnp.exp(-G)
    kdec = k * jnp.exp(G_last - G)
    causal = jnp.tril(jnp.ones((C, C), dtype=bool))
    scores = jnp.where(causal, jnp.einsum('nbhtd,nbhsd->nbhts', qg, kg), 0.0)
    o_intra = jnp.einsum('nbhts,nbhse->nbhte', scores, v)
    state_in = jnp.einsum('nbhsd,nbhse->nbhde', kdec, v)
    decay = jnp.exp(G_last[:, :, :, 0, :])

    def step(S, inp):
        dec, ds = inp
        return dec[..., None] * S + ds, S

    S0 = jnp.zeros((B, H, dk, dv), jnp.float32)
    _, S_prev = lax.scan(step, S0, (decay, state_in))
    o_inter = jnp.einsum('nbhtd,nbhde->nbhte', qg, S_prev)
    o = o_intra + o_inter
    return o.transpose(1, 0, 3, 2, 4).reshape(B, T, H, dv)


def setup_inputs(seed: int = 0) -> dict:
    key = jax.random.key(seed)
    ks = jax.random.split(key, 16)
    L, D = DEPTH, D_MODEL

    def w(k, shape, fan_in, mult=1.0):
        return jax.random.normal(k, shape, jnp.float32) * (mult * fan_in ** -0.5)

    def gain(k, shape):
        return 1.0 + 0.05 * jax.random.normal(k, shape, jnp.float32)

    return {
        "x": jax.random.normal(ks[0], (BATCH, SEQ, D), jnp.float32),
        "c": jax.random.normal(ks[1], (BATCH, D), jnp.float32),
        "w_ada": w(ks[2], (L, D, N_ADA), D, 0.5),
        "b_ada": 0.01 * jax.random.normal(ks[3], (L, N_ADA), jnp.float32),
        "g_pre_mix": gain(ks[4], (L, D)),
        "g_post_mix": gain(ks[5], (L, D)),
        "w_in": w(ks[6], (L, D, N_IN), D),
        "hg_lb_logits": 0.5 * jax.random.normal(ks[7], (L, HG_WIDTH), jnp.float32),
        "g_hg_norm": gain(ks[8], (L, HG_WIDTH)),
        "w_proj_sb": w(ks[9], (L, SB_WIDTH, D), SB_WIDTH),
        "w_proj_hg": w(ks[10], (L, HG_WIDTH, D), HG_WIDTH),
        "w_out": w(ks[11], (L, D, D), D),
        "g_pre_mlp": gain(ks[12], (L, D)),
        "g_post_mlp": gain(ks[13], (L, D)),
        "w_mlp_up": w(ks[14], (L, D, D_FF), D),
        "w_mlp_down": w(ks[15], (L, D_FF, D), D_FF),
    }


def reference(x, c, w_ada, b_ada, g_pre_mix, g_post_mix, w_in, hg_lb_logits,
              g_hg_norm, w_proj_sb, w_proj_hg, w_out, g_pre_mlp, g_post_mlp,
              w_mlp_up, w_mlp_down):
    B, T, _ = x.shape
    dt = x.dtype
    lbs = jnp.cumsum(jax.nn.softmax(hg_lb_logits.astype(jnp.float32), axis=0), axis=0)
    lbs = lbs - lbs[0:1]
    c_act = jax.nn.silu(c)
    for l in range(DEPTH):
        cond = c_act @ w_ada[l] + b_ada[l]
        sh1, sc1, gt1, sh2, sc2, gt2 = jnp.split(cond, 6, axis=-1)

        h = modulate(rmsnorm(x, g_pre_mix[l]), sh1, sc1)
        p = h @ w_in[l]
        sq, sk, sv, hq, hf, hi, hg, ga, gb = jnp.split(p, IN_SPLITS, axis=-1)

        def heads_sb(a):
            return a.reshape(B, T, SB_HEADS, SB_HEAD_DIM)

        y_sb = stick_breaking_attention(heads_sb(sq), heads_sb(sk), heads_sb(sv)).astype(dt)

        lb = lbs[l]
        hf32 = hf.astype(jnp.float32)
        logf = jnp.logaddexp(jnp.log(lb), jnp.log1p(-lb) + jax.nn.log_sigmoid(hf32))
        kk = (1.0 - lb) * jax.nn.sigmoid(-hf32)
        qq = jax.nn.silu(hq.astype(jnp.float32))

        def heads_hg(a):
            return a.reshape(B, T, HG_HEADS, HG_DIM)

        o = hgrn2_recurrence(heads_hg(qq), heads_hg(kk), heads_hg(hi.astype(jnp.float32)),
                             heads_hg(logf))
        o = rmsnorm(o, g_hg_norm[l].reshape(HG_HEADS, HG_DIM))
        o = o * jax.nn.silu(heads_hg(hg.astype(jnp.float32)))
        y_hg = o.reshape(B, T, HG_WIDTH).astype(dt)

        merged = jax.nn.sigmoid(ga) * (y_sb @ w_proj_sb[l]) + jax.nn.sigmoid(gb) * (y_hg @ w_proj_hg[l])
        mix_out = merged @ w_out[l]
        x = x + gt1[:, None, :] * rmsnorm(mix_out, g_post_mix[l])

        h2 = modulate(rmsnorm(x, g_pre_mlp[l]), sh2, sc2)
        u = jnp.square(jax.nn.relu(h2 @ w_mlp_up[l]))
        mlp_out = u @ w_mlp_down[l]
        x = x + gt2[:, None, :] * rmsnorm(mlp_out, g_post_mlp[l])
    return x
```

```python
import functools
import math

import jax
import jax.numpy as jnp
from jax import lax
from jax.experimental import pallas as pl
from jax.experimental.pallas import tpu as pltpu

F32 = jnp.float32
BF16 = jnp.bfloat16

EPS = 1e-6
HEAD_DIM = 128
SB_HEADS = 8
HG_HEADS = 8
HG_CHUNK = 32
LANES = 128

SB_Q_BLK, SB_K_BLK, SB_V_BLK = 0, 8, 16
HG_Q_BLK, HG_F_BLK, HG_I_BLK, HG_G_BLK = 24, 32, 40, 48
GATE_A_COL, GATE_B_COL = 7168, 9216

VMEM_LIMIT = 56 * 1024 * 1024


def _params(sem):
    return pltpu.CompilerParams(dimension_semantics=sem, vmem_limit_bytes=VMEM_LIMIT)


def _rms(xf, g):
    ms = jnp.mean(xf * xf, axis=-1, keepdims=True)
    return xf * lax.rsqrt(ms + EPS) * g


def _split_bf16(a):
    hi = a.astype(BF16)
    lo = (a - hi.astype(F32)).astype(BF16)
    return hi, lo


def _ada_kernel(c_ref, w_ref, b_ref, o_ref):
    c = c_ref[...]
    ca = (c * jax.nn.sigmoid(c)).astype(BF16)
    o_ref[...] = jnp.dot(ca, w_ref[...].astype(BF16),
                         preferred_element_type=F32) + b_ref[...]


def _ada_cond(c_pad, w_ada, b_ada, tn=1024):
    L, D, N = w_ada.shape
    M = c_pad.shape[0]
    return pl.pallas_call(
        _ada_kernel,
        grid=(L, N // tn),
        in_specs=[
            pl.BlockSpec((M, D), lambda l, j: (0, 0)),
            pl.BlockSpec((None, D, tn), lambda l, j: (l, 0, j)),
            pl.BlockSpec((None, 1, tn), lambda l, j: (l, 0, j)),
        ],
        out_specs=pl.BlockSpec((None, M, tn), lambda l, j: (l, 0, j)),
        out_shape=jax.ShapeDtypeStruct((L, M, N), F32),
        compiler_params=_params(("parallel", "parallel")),
        name="ada_cond",
    )(c_pad, w_ada, b_ada.reshape(L, 1, N))


def _in_proj_kernel(x_ref, g_ref, cond_ref, w_ref, o_ref, h_ref):
    @pl.when(pl.program_id(2) == 0)
    def _():
        y = _rms(x_ref[...], g_ref[...])
        h = y * (1.0 + cond_ref[1:2, :]) + cond_ref[0:1, :]
        h_ref[...] = h.astype(BF16)

    o_ref[...] = jnp.dot(h_ref[...], w_ref[...],
                         preferred_element_type=F32).astype(o_ref.dtype)


def _in_proj(x, g, cond, w, tm=1024, tn=1024):
    B, T, D = x.shape
    N = w.shape[1]
    return pl.pallas_call(
        _in_proj_kernel,
        grid=(B, T // tm, N // tn),
        in_specs=[
            pl.BlockSpec((None, tm, D), lambda b, i, j: (b, i, 0)),
            pl.BlockSpec((1, D), lambda b, i, j: (0, 0)),
            pl.BlockSpec((None, 6, D), lambda b, i, j: (b, 0, 0)),
            pl.BlockSpec((D, tn), lambda b, i, j: (0, j)),
        ],
        out_specs=pl.BlockSpec((None, tm, tn), lambda b, i, j: (b, i, j)),
        out_shape=jax.ShapeDtypeStruct((B, T, N), BF16),
        scratch_shapes=[pltpu.VMEM((tm, D), BF16)],
        compiler_params=_params(("parallel", "parallel", "arbitrary")),
        name="in_proj",
    )(x, g, cond, w)


def _sb_kernel(q_ref, k_ref, v_ref, u_ref, o_ref, *, tile, scale):
    i = pl.program_id(2)
    q = q_ref[...]
    u = u_ref[...]

    def log_not_and_z(j):
        ks = k_ref[pl.ds(pl.multiple_of(j * tile, tile), tile), :]
        s = lax.dot_general(q, ks, (((1,), (1,)), ((), ())),
                            preferred_element_type=F32)
        z = s * scale
        nz = -z
        ln = jnp.minimum(nz, 0.0) - jnp.log(1.0 + jnp.exp(jnp.minimum(z, nz)))
        return z, ln

    def suffix_sum(ln):
        hi, lo = _split_bf16(ln)
        return jnp.dot(jnp.concatenate([hi, lo], axis=1), u,
                       preferred_element_type=F32)

    def weighted_v(a, j):
        vs = v_ref[pl.ds(pl.multiple_of(j * tile, tile), tile), :]
        return jnp.dot(a.astype(BF16), vs, preferred_element_type=F32)

    z, ln = log_not_and_z(i)
    row = lax.broadcasted_iota(jnp.int32, (tile, tile), 0)
    col = lax.broadcasted_iota(jnp.int32, (tile, tile), 1)
    causal = col < row
    incl = suffix_sum(jnp.where(causal, ln, 0.0))
    a = jnp.where(causal, jnp.exp(z + incl), 0.0)
    acc0 = weighted_v(a, i)
    carry0 = incl[:, 0:1]

    def body(it, state):
        carry, acc = state
        j = i - 1 - it
        z, ln = log_not_and_z(j)
        incl = suffix_sum(ln)
        a = jnp.exp(z + incl + carry)
        return carry + incl[:, 0:1], acc + weighted_v(a, j)

    _, acc = lax.fori_loop(0, i, body, (carry0, acc0))
    o_ref[...] = acc.astype(o_ref.dtype)


def _sb_attention(p, tile=256):
    B, T, _ = p.shape
    idx = jnp.arange(tile)
    u1 = (idx[:, None] >= idx[None, :]).astype(BF16)
    u = jnp.concatenate([u1, u1], axis=0)
    kern = functools.partial(_sb_kernel, tile=tile, scale=1.0 / math.sqrt(HEAD_DIM))
    return pl.pallas_call(
        kern,
        grid=(B, SB_HEADS, T // tile),
        in_specs=[
            pl.BlockSpec((None, tile, HEAD_DIM), lambda b, h, i: (b, i, SB_Q_BLK + h)),
            pl.BlockSpec((None, T, HEAD_DIM), lambda b, h, i: (b, 0, SB_K_BLK + h)),
            pl.BlockSpec((None, T, HEAD_DIM), lambda b, h, i: (b, 0, SB_V_BLK + h)),
            pl.BlockSpec((2 * tile, tile), lambda b, h, i: (0, 0)),
        ],
        out_specs=pl.BlockSpec((None, tile, HEAD_DIM), lambda b, h, i: (b, i, h)),
        out_shape=jax.ShapeDtypeStruct((B, T, SB_HEADS * HEAD_DIM), BF16),
        compiler_params=_params(("parallel", "parallel", "arbitrary")),
        name="sb_attention",
    )(p, p, p, u)


def _hgrn_kernel(hq_ref, hf_ref, hi_ref, hg_ref, lb_ref, gn_ref, cs_ref, o_ref,
                 st_ref, *, tb, chunk):
    @pl.when(pl.program_id(2) == 0)
    def _():
        st_ref[...] = jnp.zeros_like(st_ref)

    hq = hq_ref[...].astype(F32)
    hf = hf_ref[...].astype(F32)
    v = hi_ref[...]
    log_lb = lb_ref[0:1, :]
    log1m_lb = lb_ref[1:2, :]
    one_m_lb = lb_ref[2:3, :]

    lsig = jnp.minimum(hf, 0.0) - jnp.log(1.0 + jnp.exp(-jnp.abs(hf)))
    bb = log1m_lb + lsig
    logf = jnp.maximum(log_lb, bb) + jnp.log(1.0 + jnp.exp(-jnp.abs(log_lb - bb)))
    kk = one_m_lb * jax.nn.sigmoid(-hf)
    qq = hq * jax.nn.sigmoid(hq)

    hi, lo = _split_bf16(logf)
    gg = jnp.dot(cs_ref[...], jnp.concatenate([hi, lo], axis=0),
                 preferred_element_type=F32)
    g = gg[:tb]
    g_last = gg[tb:]

    qg = (qq * jnp.exp(g)).astype(BF16)
    kg = (kk * jnp.exp(-g)).astype(BF16)
    kdec = (kk * jnp.exp(g_last - g)).astype(BF16)
    decay = jnp.exp(g_last)

    scores = lax.dot_general(qg, kg, (((1,), (1,)), ((), ())),
                             preferred_element_type=F32)
    row = lax.broadcasted_iota(jnp.int32, (tb, tb), 0)
    col = lax.broadcasted_iota(jnp.int32, (tb, tb), 1)
    keep = (col <= row) & ((row // chunk) == (col // chunk))
    scores = jnp.where(keep, scores, 0.0).astype(BF16)
    o_intra = jnp.dot(scores, v, preferred_element_type=F32)

    gn = gn_ref[...]
    st = st_ref[...]
    for n in range(tb // chunk):
        sl = slice(n * chunk, (n + 1) * chunk)
        o_inter = lax.dot_general(qg[sl], st.astype(BF16), (((1,), (1,)), ((), ())),
                                  preferred_element_type=F32)
        o = o_intra[sl] + o_inter
        hg = hg_ref[sl, :].astype(F32)
        y = _rms(o, gn) * (hg * jax.nn.sigmoid(hg))
        o_ref[sl, :] = y.astype(o_ref.dtype)
        upd = lax.dot_general(v[sl], kdec[sl], (((0,), (0,)), ((), ())),
                              preferred_element_type=F32)
        st = decay[n * chunk:n * chunk + 1, :] * st + upd
    st_ref[...] = st


def _hgrn(p, lb_rows, gn, tb=256):
    B, T, _ = p.shape
    c = HG_CHUNK
    idx = jnp.arange(tb)
    same = (idx[:, None] // c) == (idx[None, :] // c)
    tri = (same & (idx[None, :] <= idx[:, None])).astype(BF16)
    m = jnp.concatenate([tri, same.astype(BF16)], axis=0)
    cs = jnp.concatenate([m, m], axis=1)
    kern = functools.partial(_hgrn_kernel, tb=tb, chunk=c)

    def col(blk):
        return pl.BlockSpec((None, tb, HEAD_DIM), lambda b, h, i: (b, i, blk + h))

    return pl.pallas_call(
        kern,
        grid=(B, HG_HEADS, T // tb),
        in_specs=[
            col(HG_Q_BLK), col(HG_F_BLK), col(HG_I_BLK), col(HG_G_BLK),
            pl.BlockSpec((3, HEAD_DIM), lambda b, h, i: (0, h)),
            pl.BlockSpec((1, HEAD_DIM), lambda b, h, i: (0, h)),
            pl.BlockSpec((2 * tb, 2 * tb), lambda b, h, i: (0, 0)),
        ],
        out_specs=pl.BlockSpec((None, tb, HEAD_DIM), lambda b, h, i: (b, i, h)),
        out_shape=jax.ShapeDtypeStruct((B, T, HG_HEADS * HEAD_DIM), BF16),
        scratch_shapes=[pltpu.VMEM((HEAD_DIM, HEAD_DIM), F32)],
        compiler_params=_params(("parallel", "parallel", "arbitrary")),
        name="hgrn2",
    )(p, p, p, p, lb_rows, gn, cs)


def _mix_out_kernel(ysb_ref, yhg_ref, ga0_ref, ga1_ref, gb0_ref, gb1_ref,
                    wsb_ref, whg_ref, wout_ref, x_ref, g_ref, cond_ref, o_ref):
    ysb = ysb_ref[...]
    yhg = yhg_ref[...]
    half = ga0_ref.shape[-1]
    mo = None
    for c, (ga_ref, gb_ref) in enumerate(((ga0_ref, gb0_ref), (ga1_ref, gb1_ref))):
        cols = slice(c * half, (c + 1) * half)
        a = jnp.dot(ysb, wsb_ref[:, cols], preferred_element_type=F32)
        b = jnp.dot(yhg, whg_ref[:, cols], preferred_element_type=F32)
        m = (jax.nn.sigmoid(ga_ref[...].astype(F32)) * a
             + jax.nn.sigmoid(gb_ref[...].astype(F32)) * b)
        part = jnp.dot(m.astype(BF16), wout_ref[cols, :], preferred_element_type=F32)
        mo = part if mo is None else mo + part
    o_ref[...] = x_ref[...] + cond_ref[2:3, :] * _rms(mo, g_ref[...])


def _mix_out(y_sb, y_hg, p, w_sb, w_hg, w_out, x, g, cond, tm=512):
    B, T, D = x.shape
    W = y_sb.shape[-1]
    half = D // 2
    ga_blk, gb_blk = GATE_A_COL // half, GATE_B_COL // half

    def gate(blk):
        return pl.BlockSpec((None, tm, half), lambda b, i: (b, i, blk))

    def resident(shape):
        return pl.BlockSpec(shape, lambda b, i: (0, 0), pipeline_mode=pl.Buffered(1))

    return pl.pallas_call(
        _mix_out_kernel,
        grid=(B, T // tm),
        in_specs=[
            pl.BlockSpec((None, tm, W), lambda b, i: (b, i, 0)),
            pl.BlockSpec((None, tm, W), lambda b, i: (b, i, 0)),
            gate(ga_blk), gate(ga_blk + 1), gate(gb_blk), gate(gb_blk + 1),
            resident((W, D)), resident((W, D)), resident((D, D)),
            pl.BlockSpec((None, tm, D), lambda b, i: (b, i, 0)),
            pl.BlockSpec((1, D), lambda b, i: (0, 0)),
            pl.BlockSpec((None, 6, D), lambda b, i: (b, 0, 0)),
        ],
        out_specs=pl.BlockSpec((None, tm, D), lambda b, i: (b, i, 0)),
        out_shape=jax.ShapeDtypeStruct((B, T, D), F32),
        compiler_params=_params(("parallel", "parallel")),
        name="mix_out",
    )(y_sb, y_hg, p, p, p, p, w_sb, w_hg, w_out, x, g, cond)


def _mlp_kernel(x_ref, g1_ref, g2_ref, cond_ref, wup_ref, wdn_ref, o_ref,
                h_ref, acc_ref):
    j = pl.program_id(2)

    @pl.when(j == 0)
    def _():
        y = _rms(x_ref[...], g1_ref[...])
        h = y * (1.0 + cond_ref[4:5, :]) + cond_ref[3:4, :]
        h_ref[...] = h.astype(BF16)

    u = jnp.maximum(jnp.dot(h_ref[...], wup_ref[...], preferred_element_type=F32), 0.0)
    part = jnp.dot((u * u).astype(BF16), wdn_ref[...], preferred_element_type=F32)

    @pl.when(j == 0)
    def _():
        acc_ref[...] = part

    @pl.when(j > 0)
    def _():
        acc_ref[...] += part

    @pl.when(j == pl.num_programs(2) - 1)
    def _():
        o_ref[...] = x_ref[...] + cond_ref[5:6, :] * _rms(acc_ref[...], g2_ref[...])


def _mlp(x, g1, g2, cond, w_up, w_dn, tm=512, tf=1024):
    B, T, D = x.shape
    F = w_up.shape[1]
    return pl.pallas_call(
        _mlp_kernel,
        grid=(B, T // tm, F // tf),
        in_specs=[
            pl.BlockSpec((None, tm, D), lambda b, i, j: (b, i, 0)),
            pl.BlockSpec((1, D), lambda b, i, j: (0, 0)),
            pl.BlockSpec((1, D), lambda b, i, j: (0, 0)),
            pl.BlockSpec((None, 6, D), lambda b, i, j: (b, 0, 0)),
            pl.BlockSpec((D, tf), lambda b, i, j: (0, j)),
            pl.BlockSpec((tf, D), lambda b, i, j: (j, 0)),
        ],
        out_specs=pl.BlockSpec((None, tm, D), lambda b, i, j: (b, i, 0)),
        out_shape=jax.ShapeDtypeStruct((B, T, D), F32),
        scratch_shapes=[pltpu.VMEM((tm, D), BF16), pltpu.VMEM((tm, D), F32)],
        compiler_params=_params(("parallel", "parallel", "arbitrary")),
        name="mlp",
    )(x, g1, g2, cond, w_up, w_dn)


def kernel(x, c, w_ada, b_ada, g_pre_mix, g_post_mix, w_in, hg_lb_logits, g_hg_norm,
           w_proj_sb, w_proj_hg, w_out, g_pre_mlp, g_post_mlp, w_mlp_up, w_mlp_down):
    B, T, D = x.shape
    L = w_ada.shape[0]

    lbs = jnp.cumsum(jax.nn.softmax(hg_lb_logits.astype(F32), axis=0), axis=0)
    lbs = lbs - lbs[0:1]
    lb_rows = jnp.stack([jnp.log(lbs), jnp.log1p(-lbs), 1.0 - lbs], axis=1)

    c_pad = jnp.pad(c, ((0, 8 - B), (0, 0)))
    cond_all = _ada_cond(c_pad, w_ada, b_ada)[:, :B, :].reshape(L, B, 6, D)

    for l in range(L):
        cond = cond_all[l]
        p = _in_proj(x, g_pre_mix[l].reshape(1, D), cond, w_in[l].astype(BF16))
        y_sb = _sb_attention(p)
        y_hg = _hgrn(p, lb_rows[l], g_hg_norm[l].reshape(1, -1))
        x = _mix_out(y_sb, y_hg, p, w_proj_sb[l].astype(BF16), w_proj_hg[l].astype(BF16),
                     w_out[l].astype(BF16), x, g_post_mix[l].reshape(1, D), cond)
        x = _mlp(x, g_pre_mlp[l].reshape(1, D), g_post_mlp[l].reshape(1, D), cond,
                 w_mlp_up[l].astype(BF16), w_mlp_down[l].astype(BF16))
    return x
```

```python
import functools
import math

import jax
import jax.numpy as jnp
from jax import lax
from jax.experimental import pallas as pl
from jax.experimental.pallas import tpu as pltpu

F32 = jnp.float32
BF16 = jnp.bfloat16

EPS = 1e-6
HEAD_DIM = 128
SB_HEADS = 8
HG_HEADS = 8
HG_CHUNK = 32

SB_Q_BLK, SB_K_BLK, SB_V_BLK = 0, 8, 16
HG_Q_BLK, HG_F_BLK, HG_I_BLK, HG_G_BLK = 24, 32, 40, 48
GATE_A_COL, GATE_B_COL = 7168, 9216

VMEM_LIMIT = 56 * 1024 * 1024

SKIP_BELOW = -104.0
LOG2E = 1.4426950408889634

NT_DIMS = (((1,), (1,)), ((), ()))
TN_DIMS = (((0,), (0,)), ((), ()))


def _params(sem):
    return pltpu.CompilerParams(dimension_semantics=sem, vmem_limit_bytes=VMEM_LIMIT)


def _rms(xf, g):
    ms = jnp.mean(xf * xf, axis=-1, keepdims=True)
    return xf * lax.rsqrt(ms + EPS) * g


def _split_bf16(a):
    hi = a.astype(BF16)
    lo = (a - hi.astype(F32)).astype(BF16)
    return hi, lo


def _ada_kernel(c_ref, w_ref, b_ref, o_ref):
    c = c_ref[...]
    ca = (c * jax.nn.sigmoid(c)).astype(BF16)
    o_ref[...] = jnp.dot(ca, w_ref[...].astype(BF16),
                         preferred_element_type=F32) + b_ref[...]


def _ada_cond(c_pad, w_ada, b_ada, tn=1024):
    L, D, N = w_ada.shape
    M = c_pad.shape[0]
    return pl.pallas_call(
        _ada_kernel,
        grid=(L, N // tn),
        in_specs=[
            pl.BlockSpec((M, D), lambda l, j: (0, 0)),
            pl.BlockSpec((None, D, tn), lambda l, j: (l, 0, j)),
            pl.BlockSpec((None, 1, tn), lambda l, j: (l, 0, j)),
        ],
        out_specs=pl.BlockSpec((None, M, tn), lambda l, j: (l, 0, j)),
        out_shape=jax.ShapeDtypeStruct((L, M, N), F32),
        compiler_params=_params(("parallel", "parallel")),
        name="ada_cond",
    )(c_pad, w_ada, b_ada.reshape(L, 1, N))


def _in_proj_kernel(x_ref, g_ref, cond_ref, w_ref, o_ref, h_ref):
    @pl.when(pl.program_id(2) == 0)
    def _():
        y = _rms(x_ref[...], g_ref[...])
        h = y * (1.0 + cond_ref[1:2, :]) + cond_ref[0:1, :]
        h_ref[...] = h.astype(BF16)

    o_ref[...] = jnp.dot(h_ref[...], w_ref[...],
                         preferred_element_type=F32).astype(o_ref.dtype)


def _in_proj(x, g, cond, w, tm=1024, tn=1024):
    B, T, D = x.shape
    N = w.shape[1]
    return pl.pallas_call(
        _in_proj_kernel,
        grid=(B, T // tm, N // tn),
        in_specs=[
            pl.BlockSpec((None, tm, D), lambda b, i, j: (b, i, 0)),
            pl.BlockSpec((1, D), lambda b, i, j: (0, 0)),
            pl.BlockSpec((None, 6, D), lambda b, i, j: (b, 0, 0)),
            pl.BlockSpec((D, tn), lambda b, i, j: (0, j)),
        ],
        out_specs=pl.BlockSpec((None, tm, tn), lambda b, i, j: (b, i, j)),
        out_shape=jax.ShapeDtypeStruct((B, T, N), BF16),
        scratch_shapes=[pltpu.VMEM((tm, D), BF16)],
        compiler_params=_params(("parallel", "parallel", "arbitrary")),
        name="in_proj",
    )(x, g, cond, w)


def _sb_kernel(q_ref, k_ref, v_ref, u_ref, o_ref, *, tile, heads, scale):
    i = pl.program_id(2)
    u = u_ref[...]
    row = lax.broadcasted_iota(jnp.int32, (tile, tile), 0)
    col = lax.broadcasted_iota(jnp.int32, (tile, tile), 1)
    causal = col < row

    def tile_terms(h, j, diagonal):
        lanes = slice(h * HEAD_DIM, (h + 1) * HEAD_DIM)
        keys = pl.ds(pl.multiple_of(j * tile, tile), tile)
        s = lax.dot_general(q_ref[:, lanes], k_ref[keys, lanes], NT_DIMS,
                            preferred_element_type=F32)
        e = jnp.exp2(jnp.abs(s) * (-scale * LOG2E))
        ln = jnp.maximum(s, 0.0) * (-scale) - jnp.log(1.0 + e)
        if diagonal:
            ln = jnp.where(causal, ln, 0.0)
        hi, lo = _split_bf16(ln)
        incl = jnp.dot(jnp.concatenate([hi, lo], axis=1), u, preferred_element_type=F32)
        a = jnp.exp(s * scale + incl)
        if diagonal:
            a = jnp.where(causal, a, 0.0)
        pv = jnp.dot(a.astype(BF16), v_ref[keys, lanes], preferred_element_type=F32)
        return pv, incl[:, 0:1]

    has_prev = i > 0
    jp = jnp.maximum(i - 1, 0)
    carries, accs = [], []
    for h in range(heads):
        pv_d, tot_d = tile_terms(h, i, True)
        pv_p, tot_p = tile_terms(h, jp, False)
        accs.append(pv_d + jnp.where(has_prev, jnp.exp(tot_d), 0.0) * pv_p)
        carries.append(tot_d + jnp.where(has_prev, tot_p, 0.0))

    def live(state):
        j, carries, _ = state
        top = carries[0]
        for c in carries[1:]:
            top = jnp.maximum(top, c)
        return (j >= 0) & (jnp.max(top) >= SKIP_BELOW)

    def step(state):
        j, carries, accs = state
        new_c, new_a = [], []
        for h in range(heads):
            pv, tot = tile_terms(h, j, False)
            new_a.append(accs[h] + jnp.exp(carries[h]) * pv)
            new_c.append(carries[h] + tot)
        return j - 1, tuple(new_c), tuple(new_a)

    _, _, accs = lax.while_loop(live, step, (i - 2, tuple(carries), tuple(accs)))
    for h in range(heads):
        o_ref[:, h * HEAD_DIM:(h + 1) * HEAD_DIM] = accs[h].astype(o_ref.dtype)


def _sb_attention(p, tile=256, heads=8):
    B, T, _ = p.shape
    idx = jnp.arange(tile)
    u1 = (idx[:, None] >= idx[None, :]).astype(BF16)
    u = jnp.concatenate([u1, u1], axis=0)
    kern = functools.partial(_sb_kernel, tile=tile, heads=heads,
                             scale=1.0 / math.sqrt(HEAD_DIM))
    w = heads * HEAD_DIM
    groups = SB_HEADS // heads
    return pl.pallas_call(
        kern,
        grid=(B, groups, T // tile),
        in_specs=[
            pl.BlockSpec((None, tile, w), lambda b, g, i: (b, i, SB_Q_BLK // heads + g)),
            pl.BlockSpec((None, T, w), lambda b, g, i: (b, 0, SB_K_BLK // heads + g)),
            pl.BlockSpec((None, T, w), lambda b, g, i: (b, 0, SB_V_BLK // heads + g)),
            pl.BlockSpec((2 * tile, tile), lambda b, g, i: (0, 0)),
        ],
        out_specs=pl.BlockSpec((None, tile, w), lambda b, g, i: (b, i, g)),
        out_shape=jax.ShapeDtypeStruct((B, T, SB_HEADS * HEAD_DIM), BF16),
        compiler_params=_params(("parallel", "parallel", "arbitrary")),
        name="sb_attention",
    )(p, p, p, u)


def _hgrn_kernel(hq_ref, hf_ref, hi_ref, hg_ref, lb_ref, gn_ref, tri_ref, o_ref,
                 st_ref, *, tb, chunk, heads):
    @pl.when(pl.program_id(2) == 0)
    def _():
        st_ref[...] = jnp.zeros_like(st_ref)

    hq = hq_ref[...].astype(F32)
    hf = hf_ref[...].astype(F32)
    log_lb = lb_ref[0:1, :]
    log1m_lb = lb_ref[1:2, :]
    one_m_lb = lb_ref[2:3, :]

    lsig = jnp.minimum(hf, 0.0) - jnp.log(1.0 + jnp.exp(-jnp.abs(hf)))
    bb = log1m_lb + lsig
    logf = jnp.maximum(log_lb, bb) + jnp.log(1.0 + jnp.exp(-jnp.abs(log_lb - bb)))
    kk_all = one_m_lb * jax.nn.sigmoid(-hf)
    qq = hq * jax.nn.sigmoid(hq)

    hi, lo = _split_bf16(logf)
    g_all = jnp.dot(tri_ref[...], jnp.concatenate([hi, lo], axis=0),
                    preferred_element_type=F32)
    qg_all = (qq * jnp.exp(g_all)).astype(BF16)
    kg_all = (kk_all * jnp.exp(-g_all)).astype(BF16)

    row = lax.broadcasted_iota(jnp.int32, (tb, tb), 0)
    col = lax.broadcasted_iota(jnp.int32, (tb, tb), 1)
    keep = (col <= row) & ((row // chunk) == (col // chunk))
    chunks = [slice(n * chunk, (n + 1) * chunk) for n in range(tb // chunk)]

    for h in range(heads):
        lanes = slice(h * HEAD_DIM, (h + 1) * HEAD_DIM)
        g, kk, qg = g_all[:, lanes], kk_all[:, lanes], qg_all[:, lanes]
        v = hi_ref[:, lanes]

        scores = lax.dot_general(qg, kg_all[:, lanes], NT_DIMS, preferred_element_type=F32)
        scores = jnp.where(keep, scores, 0.0).astype(BF16)
        o = jnp.dot(scores, v, preferred_element_type=F32)

        upds, decays = [], []
        for sl in chunks:
            g_end = g[sl.stop - 1:sl.stop, :]
            kdec = (kk[sl] * jnp.exp(g_end - g[sl])).astype(BF16)
            upds.append(lax.dot_general(v[sl], kdec, TN_DIMS, preferred_element_type=F32))
            decays.append(jnp.exp(g_end))

        st = st_ref[h]
        inter = []
        for sl, upd, dec in zip(chunks, upds, decays):
            inter.append(lax.dot_general(qg[sl], st.astype(BF16), NT_DIMS,
                                         preferred_element_type=F32))
            st = dec * st + upd
        st_ref[h] = st

        o = o + jnp.concatenate(inter, axis=0)
        hg = hg_ref[:, lanes].astype(F32)
        y = _rms(o, gn_ref[:, lanes]) * (hg * jax.nn.sigmoid(hg))
        o_ref[:, lanes] = y.astype(o_ref.dtype)


def _hgrn(p, lb_rows, gn, tb=256, heads=8):
    B, T, _ = p.shape
    c = HG_CHUNK
    idx = jnp.arange(tb)
    same = (idx[:, None] // c) == (idx[None, :] // c)
    tri1 = (same & (idx[None, :] <= idx[:, None])).astype(BF16)
    tri = jnp.concatenate([tri1, tri1], axis=1)
    kern = functools.partial(_hgrn_kernel, tb=tb, chunk=c, heads=heads)
    w = heads * HEAD_DIM
    groups = HG_HEADS // heads

    def col(blk):
        return pl.BlockSpec((None, tb, w), lambda b, g, i: (b, i, blk // heads + g))

    return pl.pallas_call(
        kern,
        grid=(B, groups, T // tb),
        in_specs=[
            col(HG_Q_BLK), col(HG_F_BLK), col(HG_I_BLK), col(HG_G_BLK),
            pl.BlockSpec((3, w), lambda b, g, i: (0, g)),
            pl.BlockSpec((1, w), lambda b, g, i: (0, g)),
            pl.BlockSpec((tb, 2 * tb), lambda b, g, i: (0, 0)),
        ],
        out_specs=pl.BlockSpec((None, tb, w), lambda b, g, i: (b, i, g)),
        out_shape=jax.ShapeDtypeStruct((B, T, HG_HEADS * HEAD_DIM), BF16),
        scratch_shapes=[pltpu.VMEM((heads, HEAD_DIM, HEAD_DIM), F32)],
        compiler_params=_params(("parallel", "parallel", "arbitrary")),
        name="hgrn2",
    )(p, p, p, p, lb_rows, gn, tri)


def _mix_out_kernel(ysb_ref, yhg_ref, ga0_ref, ga1_ref, gb0_ref, gb1_ref,
                    wsb_ref, whg_ref, wout_ref, x_ref, g_ref, cond_ref, o_ref):
    ysb = ysb_ref[...]
    yhg = yhg_ref[...]
    half = ga0_ref.shape[-1]
    mo = None
    for c, (ga_ref, gb_ref) in enumerate(((ga0_ref, gb0_ref), (ga1_ref, gb1_ref))):
        cols = slice(c * half, (c + 1) * half)
        a = jnp.dot(ysb, wsb_ref[:, cols], preferred_element_type=F32)
        b = jnp.dot(yhg, whg_ref[:, cols], preferred_element_type=F32)
        m = (jax.nn.sigmoid(ga_ref[...].astype(F32)) * a
             + jax.nn.sigmoid(gb_ref[...].astype(F32)) * b)
        part = jnp.dot(m.astype(BF16), wout_ref[cols, :], preferred_element_type=F32)
        mo = part if mo is None else mo + part
    o_ref[...] = x_ref[...] + cond_ref[2:3, :] * _rms(mo, g_ref[...])


def _mix_out(y_sb, y_hg, p, w_sb, w_hg, w_out, x, g, cond, tm=512):
    B, T, D = x.shape
    W = y_sb.shape[-1]
    half = D // 2
    ga_blk, gb_blk = GATE_A_COL // half, GATE_B_COL // half

    def gate(blk):
        return pl.BlockSpec((None, tm, half), lambda b, i: (b, i, blk))

    def resident(shape):
        return pl.BlockSpec(shape, lambda b, i: (0, 0), pipeline_mode=pl.Buffered(1))

    return pl.pallas_call(
        _mix_out_kernel,
        grid=(B, T // tm),
        in_specs=[
            pl.BlockSpec((None, tm, W), lambda b, i: (b, i, 0)),
            pl.BlockSpec((None, tm, W), lambda b, i: (b, i, 0)),
            gate(ga_blk), gate(ga_blk + 1), gate(gb_blk), gate(gb_blk + 1),
            resident((W, D)), resident((W, D)), resident((D, D)),
            pl.BlockSpec((None, tm, D), lambda b, i: (b, i, 0)),
            pl.BlockSpec((1, D), lambda b, i: (0, 0)),
            pl.BlockSpec((None, 6, D), lambda b, i: (b, 0, 0)),
        ],
        out_specs=pl.BlockSpec((None, tm, D), lambda b, i: (b, i, 0)),
        out_shape=jax.ShapeDtypeStruct((B, T, D), F32),
        compiler_params=_params(("parallel", "parallel")),
        name="mix_out",
    )(y_sb, y_hg, p, p, p, p, w_sb, w_hg, w_out, x, g, cond)


def _mlp_kernel(x_ref, g1_ref, g2_ref, cond_ref, wup_ref, wdn_ref, o_ref,
                h_ref, acc_ref):
    j = pl.program_id(2)

    @pl.when(j == 0)
    def _():
        y = _rms(x_ref[...], g1_ref[...])
        h = y * (1.0 + cond_ref[4:5, :]) + cond_ref[3:4, :]
        h_ref[...] = h.astype(BF16)

    u = jnp.maximum(jnp.dot(h_ref[...], wup_ref[...], preferred_element_type=F32), 0.0)
    part = jnp.dot((u * u).astype(BF16), wdn_ref[...], preferred_element_type=F32)

    @pl.when(j == 0)
    def _():
        acc_ref[...] = part

    @pl.when(j > 0)
    def _():
        acc_ref[...] += part

    @pl.when(j == pl.num_programs(2) - 1)
    def _():
        o_ref[...] = x_ref[...] + cond_ref[5:6, :] * _rms(acc_ref[...], g2_ref[...])


def _mlp(x, g1, g2, cond, w_up, w_dn, tm=512, tf=1024):
    B, T, D = x.shape
    F = w_up.shape[1]
    return pl.pallas_call(
        _mlp_kernel,
        grid=(B, T // tm, F // tf),
        in_specs=[
            pl.BlockSpec((None, tm, D), lambda b, i, j: (b, i, 0)),
            pl.BlockSpec((1, D), lambda b, i, j: (0, 0)),
            pl.BlockSpec((1, D), lambda b, i, j: (0, 0)),
            pl.BlockSpec((None, 6, D), lambda b, i, j: (b, 0, 0)),
            pl.BlockSpec((D, tf), lambda b, i, j: (0, j)),
            pl.BlockSpec((tf, D), lambda b, i, j: (j, 0)),
        ],
        out_specs=pl.BlockSpec((None, tm, D), lambda b, i, j: (b, i, 0)),
        out_shape=jax.ShapeDtypeStruct((B, T, D), F32),
        scratch_shapes=[pltpu.VMEM((tm, D), BF16), pltpu.VMEM((tm, D), F32)],
        compiler_params=_params(("parallel", "parallel", "arbitrary")),
        name="mlp",
    )(x, g1, g2, cond, w_up, w_dn)


def kernel(x, c, w_ada, b_ada, g_pre_mix, g_post_mix, w_in, hg_lb_logits, g_hg_norm,
           w_proj_sb, w_proj_hg, w_out, g_pre_mlp, g_post_mlp, w_mlp_up, w_mlp_down):
    B, T, D = x.shape
    L = w_ada.shape[0]

    lbs = jnp.cumsum(jax.nn.softmax(hg_lb_logits.astype(F32), axis=0), axis=0)
    lbs = lbs - lbs[0:1]
    lb_rows = jnp.stack([jnp.log(lbs), jnp.log1p(-lbs), 1.0 - lbs], axis=1)

    c_pad = jnp.pad(c, ((0, 8 - B), (0, 0)))
    cond_all = _ada_cond(c_pad, w_ada, b_ada)[:, :B, :].reshape(L, B, 6, D)

    for l in range(L):
        cond = cond_all[l]
        p = _in_proj(x, g_pre_mix[l].reshape(1, D), cond, w_in[l].astype(BF16))
        y_sb = _sb_attention(p)
        y_hg = _hgrn(p, lb_rows[l], g_hg_norm[l].reshape(1, -1))
        x = _mix_out(y_sb, y_hg, p, w_proj_sb[l].astype(BF16), w_proj_hg[l].astype(BF16),
                     w_out[l].astype(BF16), x, g_post_mix[l].reshape(1, D), cond)
        x = _mlp(x, g_pre_mlp[l].reshape(1, D), g_post_mlp[l].reshape(1, D), cond,
                 w_mlp_up[l].astype(BF16), w_mlp_down[l].astype(BF16))
    return x
```

```python
import functools
import math

import jax
import jax.numpy as jnp
from jax import lax
from jax.experimental import pallas as pl
from jax.experimental.pallas import tpu as pltpu

F32 = jnp.float32
BF16 = jnp.bfloat16

EPS = 1e-6
HEAD_DIM = 128
SB_HEADS = 8
HG_HEADS = 8
HG_CHUNK = 32

SB_Q_BLK, SB_K_BLK, SB_V_BLK = 0, 8, 16
HG_Q_BLK, HG_F_BLK, HG_I_BLK, HG_G_BLK = 24, 32, 40, 48
GATE_A_COL, GATE_B_COL = 7168, 9216

VMEM_LIMIT = 56 * 1024 * 1024

SKIP_BELOW = -104.0
LOG2E = 1.4426950408889634

NT_DIMS = (((1,), (1,)), ((), ()))
TN_DIMS = (((0,), (0,)), ((), ()))


def _params(sem):
    return pltpu.CompilerParams(dimension_semantics=sem, vmem_limit_bytes=VMEM_LIMIT)


def _rms(xf, g):
    ms = jnp.mean(xf * xf, axis=-1, keepdims=True)
    return xf * lax.rsqrt(ms + EPS) * g


def _silu(x):
    hx = 0.5 * x
    return hx + hx * jnp.tanh(hx)


def _for_row_chunks(n_rows, chunk_rows, fn):
    def body(r, carry):
        fn(pl.ds(pl.multiple_of(r * chunk_rows, chunk_rows), chunk_rows))
        return carry
    lax.fori_loop(0, n_rows // chunk_rows, body, 0, unroll=4)


def _norm_modulate_to(h_ref, x_ref, g_ref, shift, scale):
    gmod = g_ref[...] * (1.0 + scale)

    def chunk(rows):
        x = x_ref[rows, :]
        ms = jnp.mean(x * x, axis=-1, keepdims=True)
        h_ref[rows, :] = (x * lax.rsqrt(ms + EPS) * gmod + shift).astype(h_ref.dtype)

    _for_row_chunks(x_ref.shape[0], 32, chunk)


def _split_bf16(a):
    hi = a.astype(BF16)
    lo = (a - hi.astype(F32)).astype(BF16)
    return hi, lo


def _ada_kernel(c_ref, w_ref, b_ref, o_ref):
    c = c_ref[...]
    ca = (c * jax.nn.sigmoid(c)).astype(BF16)
    o_ref[...] = jnp.dot(ca, w_ref[...].astype(BF16),
                         preferred_element_type=F32) + b_ref[...]


def _ada_cond(c_pad, w_ada, b_ada, tn=1024):
    L, D, N = w_ada.shape
    M = c_pad.shape[0]
    return pl.pallas_call(
        _ada_kernel,
        grid=(L, N // tn),
        in_specs=[
            pl.BlockSpec((M, D), lambda l, j: (0, 0)),
            pl.BlockSpec((None, D, tn), lambda l, j: (l, 0, j)),
            pl.BlockSpec((None, 1, tn), lambda l, j: (l, 0, j)),
        ],
        out_specs=pl.BlockSpec((None, M, tn), lambda l, j: (l, 0, j)),
        out_shape=jax.ShapeDtypeStruct((L, M, N), F32),
        compiler_params=_params(("parallel", "parallel")),
        name="ada_cond",
    )(c_pad, w_ada, b_ada.reshape(L, 1, N))


def _in_proj_kernel(x_ref, g_ref, cond_ref, w_ref, o_ref, h_ref):
    @pl.when(pl.program_id(2) == 0)
    def _():
        _norm_modulate_to(h_ref, x_ref, g_ref, cond_ref[0:1, :], cond_ref[1:2, :])

    o_ref[...] = jnp.dot(h_ref[...], w_ref[...],
                         preferred_element_type=F32).astype(o_ref.dtype)


def _in_proj(x, g, cond, w, layer, tm=1024, tn=1024):
    B, T, D = x.shape
    N = w.shape[2]
    return pl.pallas_call(
        _in_proj_kernel,
        grid=(B, T // tm, N // tn),
        in_specs=[
            pl.BlockSpec((None, tm, D), lambda b, i, j: (b, i, 0)),
            pl.BlockSpec((1, D), lambda b, i, j: (0, 0)),
            pl.BlockSpec((None, 6, D), lambda b, i, j: (b, 0, 0)),
            pl.BlockSpec((None, D, tn), lambda b, i, j: (layer, 0, j)),
        ],
        out_specs=pl.BlockSpec((None, tm, tn), lambda b, i, j: (b, i, j)),
        out_shape=jax.ShapeDtypeStruct((B, T, N), BF16),
        scratch_shapes=[pltpu.VMEM((tm, D), BF16)],
        compiler_params=_params(("parallel", "parallel", "arbitrary")),
        name="in_proj",
    )(x, g, cond, w)


def _sb_kernel(q_ref, k_ref, v_ref, u_ref, o_ref, *, tile, heads, scale):
    i = pl.program_id(2)
    u = u_ref[...]
    row = lax.broadcasted_iota(jnp.int32, (tile, tile), 0)
    col = lax.broadcasted_iota(jnp.int32, (tile, tile), 1)
    causal = col < row

    def tile_terms(h, j, diagonal):
        lanes = slice(h * HEAD_DIM, (h + 1) * HEAD_DIM)
        keys = pl.ds(pl.multiple_of(j * tile, tile), tile)
        s = lax.dot_general(q_ref[:, lanes], k_ref[keys, lanes], NT_DIMS,
                            preferred_element_type=F32)
        e = jnp.exp2(jnp.abs(s) * (-scale * LOG2E))
        ln = jnp.maximum(s, 0.0) * (-scale) - jnp.log(1.0 + e)
        if diagonal:
            ln = jnp.where(causal, ln, 0.0)
        hi, lo = _split_bf16(ln)
        incl = jnp.dot(jnp.concatenate([hi, lo], axis=1), u, preferred_element_type=F32)
        a = jnp.exp(s * scale + incl)
        if diagonal:
            a = jnp.where(causal, a, 0.0)
        pv = jnp.dot(a.astype(BF16), v_ref[keys, lanes], preferred_element_type=F32)
        return pv, incl[:, 0:1]

    has_prev = i > 0
    jp = jnp.maximum(i - 1, 0)
    carries, accs = [], []
    for h in range(heads):
        pv_d, tot_d = tile_terms(h, i, True)
        pv_p, tot_p = tile_terms(h, jp, False)
        accs.append(pv_d + jnp.where(has_prev, jnp.exp(tot_d), 0.0) * pv_p)
        carries.append(tot_d + jnp.where(has_prev, tot_p, 0.0))

    def live(state):
        j, carries, _ = state
        top = carries[0]
        for c in carries[1:]:
            top = jnp.maximum(top, c)
        return (j >= 0) & (jnp.max(top) >= SKIP_BELOW)

    def step(state):
        j, carries, accs = state
        new_c, new_a = [], []
        for h in range(heads):
            pv, tot = tile_terms(h, j, False)
            new_a.append(accs[h] + jnp.exp(carries[h]) * pv)
            new_c.append(carries[h] + tot)
        return j - 1, tuple(new_c), tuple(new_a)

    _, _, accs = lax.while_loop(live, step, (i - 2, tuple(carries), tuple(accs)))
    for h in range(heads):
        o_ref[:, h * HEAD_DIM:(h + 1) * HEAD_DIM] = accs[h].astype(o_ref.dtype)


def _sb_attention(p, tile=256, heads=8):
    B, T, _ = p.shape
    idx = jnp.arange(tile)
    u1 = (idx[:, None] >= idx[None, :]).astype(BF16)
    u = jnp.concatenate([u1, u1], axis=0)
    kern = functools.partial(_sb_kernel, tile=tile, heads=heads,
                             scale=1.0 / math.sqrt(HEAD_DIM))
    w = heads * HEAD_DIM
    groups = SB_HEADS // heads
    return pl.pallas_call(
        kern,
        grid=(B, groups, T // tile),
        in_specs=[
            pl.BlockSpec((None, tile, w), lambda b, g, i: (b, i, SB_Q_BLK // heads + g)),
            pl.BlockSpec((None, T, w), lambda b, g, i: (b, 0, SB_K_BLK // heads + g)),
            pl.BlockSpec((None, T, w), lambda b, g, i: (b, 0, SB_V_BLK // heads + g)),
            pl.BlockSpec((2 * tile, tile), lambda b, g, i: (0, 0)),
        ],
        out_specs=pl.BlockSpec((None, tile, w), lambda b, g, i: (b, i, g)),
        out_shape=jax.ShapeDtypeStruct((B, T, SB_HEADS * HEAD_DIM), BF16),
        compiler_params=_params(("parallel", "parallel", "arbitrary")),
        name="sb_attention",
    )(p, p, p, u)


def _hgrn_kernel(hq_ref, hf_ref, hi_ref, hg_ref, lb_ref, gn_ref, tri_ref, o_ref,
                 st_ref, *, tb, chunk, heads):
    @pl.when(pl.program_id(2) == 0)
    def _():
        st_ref[...] = jnp.zeros_like(st_ref)

    hq = hq_ref[...].astype(F32)
    hf = hf_ref[...].astype(F32)
    lb = lb_ref[0:1, :]
    one_m_lb = lb_ref[1:2, :]

    e = jnp.exp(-jnp.abs(hf))
    r = 1.0 / (1.0 + e)
    er = e * r
    pos = hf >= 0.0
    logf = jnp.log(lb + one_m_lb * jnp.where(pos, r, er))
    kk_all = one_m_lb * jnp.where(pos, er, r)
    qq = _silu(hq)

    hi, lo = _split_bf16(logf)
    g_all = jnp.dot(tri_ref[...], jnp.concatenate([hi, lo], axis=0),
                    preferred_element_type=F32)
    qg_all = (qq * jnp.exp(g_all)).astype(BF16)
    kg_all = (kk_all * jnp.exp(-g_all)).astype(BF16)

    row = lax.broadcasted_iota(jnp.int32, (tb, tb), 0)
    col = lax.broadcasted_iota(jnp.int32, (tb, tb), 1)
    keep = (col <= row) & ((row // chunk) == (col // chunk))
    chunks = [slice(n * chunk, (n + 1) * chunk) for n in range(tb // chunk)]

    for h in range(heads):
        lanes = slice(h * HEAD_DIM, (h + 1) * HEAD_DIM)
        g, kk, qg = g_all[:, lanes], kk_all[:, lanes], qg_all[:, lanes]
        v = hi_ref[:, lanes]

        scores = lax.dot_general(qg, kg_all[:, lanes], NT_DIMS, preferred_element_type=F32)
        scores = jnp.where(keep, scores, 0.0).astype(BF16)
        o = jnp.dot(scores, v, preferred_element_type=F32)

        upds, decays = [], []
        for sl in chunks:
            g_end = g[sl.stop - 1:sl.stop, :]
            kdec = (kk[sl] * jnp.exp(g_end - g[sl])).astype(BF16)
            upds.append(lax.dot_general(v[sl], kdec, TN_DIMS, preferred_element_type=F32))
            decays.append(jnp.exp(g_end))

        st = st_ref[h]
        inter = []
        for sl, upd, dec in zip(chunks, upds, decays):
            inter.append(lax.dot_general(qg[sl], st.astype(BF16), NT_DIMS,
                                         preferred_element_type=F32))
            st = dec * st + upd
        st_ref[h] = st

        o = o + jnp.concatenate(inter, axis=0)
        hg = hg_ref[:, lanes].astype(F32)
        y = _rms(o, gn_ref[:, lanes]) * _silu(hg)
        o_ref[:, lanes] = y.astype(o_ref.dtype)


def _hgrn(p, lb_rows, gn, tb=256, heads=8):
    B, T, _ = p.shape
    c = HG_CHUNK
    idx = jnp.arange(tb)
    same = (idx[:, None] // c) == (idx[None, :] // c)
    tri1 = (same & (idx[None, :] <= idx[:, None])).astype(BF16)
    tri = jnp.concatenate([tri1, tri1], axis=1)
    kern = functools.partial(_hgrn_kernel, tb=tb, chunk=c, heads=heads)
    w = heads * HEAD_DIM
    groups = HG_HEADS // heads

    def col(blk):
        return pl.BlockSpec((None, tb, w), lambda b, g, i: (b, i, blk // heads + g))

    return pl.pallas_call(
        kern,
        grid=(B, groups, T // tb),
        in_specs=[
            col(HG_Q_BLK), col(HG_F_BLK), col(HG_I_BLK), col(HG_G_BLK),
            pl.BlockSpec((2, w), lambda b, g, i: (0, g)),
            pl.BlockSpec((1, w), lambda b, g, i: (0, g)),
            pl.BlockSpec((tb, 2 * tb), lambda b, g, i: (0, 0)),
        ],
        out_specs=pl.BlockSpec((None, tb, w), lambda b, g, i: (b, i, g)),
        out_shape=jax.ShapeDtypeStruct((B, T, HG_HEADS * HEAD_DIM), BF16),
        scratch_shapes=[pltpu.VMEM((heads, HEAD_DIM, HEAD_DIM), F32)],
        compiler_params=_params(("parallel", "parallel", "arbitrary")),
        name="hgrn2",
    )(p, p, p, p, lb_rows, gn, tri)


def _mix_out_kernel(ysb_ref, yhg_ref, ga0_ref, ga1_ref, gb0_ref, gb1_ref,
                    wsb_ref, whg_ref, wout_ref, x_ref, g_ref, cond_ref, o_ref):
    ysb = ysb_ref[...]
    yhg = yhg_ref[...]
    half = ga0_ref.shape[-1]
    mo = None
    for c, (ga_ref, gb_ref) in enumerate(((ga0_ref, gb0_ref), (ga1_ref, gb1_ref))):
        cols = slice(c * half, (c + 1) * half)
        a = jnp.dot(ysb, wsb_ref[:, cols], preferred_element_type=F32)
        b = jnp.dot(yhg, whg_ref[:, cols], preferred_element_type=F32)
        m = (jax.nn.sigmoid(ga_ref[...].astype(F32)) * a
             + jax.nn.sigmoid(gb_ref[...].astype(F32)) * b)
        part = jnp.dot(m.astype(BF16), wout_ref[cols, :], preferred_element_type=F32)
        mo = part if mo is None else mo + part
    o_ref[...] = x_ref[...] + cond_ref[2:3, :] * _rms(mo, g_ref[...])


def _mix_out(y_sb, y_hg, p, w_sb, w_hg, w_out, layer, x, g, cond, tm=512):
    B, T, D = x.shape
    W = y_sb.shape[-1]
    half = D // 2
    ga_blk, gb_blk = GATE_A_COL // half, GATE_B_COL // half

    def gate(blk):
        return pl.BlockSpec((None, tm, half), lambda b, i: (b, i, blk))

    def resident(shape):
        return pl.BlockSpec((None,) + shape, lambda b, i: (layer, 0, 0),
                            pipeline_mode=pl.Buffered(1))

    return pl.pallas_call(
        _mix_out_kernel,
        grid=(B, T // tm),
        in_specs=[
            pl.BlockSpec((None, tm, W), lambda b, i: (b, i, 0)),
            pl.BlockSpec((None, tm, W), lambda b, i: (b, i, 0)),
            gate(ga_blk), gate(ga_blk + 1), gate(gb_blk), gate(gb_blk + 1),
            resident((W, D)), resident((W, D)), resident((D, D)),
            pl.BlockSpec((None, tm, D), lambda b, i: (b, i, 0)),
            pl.BlockSpec((1, D), lambda b, i: (0, 0)),
            pl.BlockSpec((None, 6, D), lambda b, i: (b, 0, 0)),
        ],
        out_specs=pl.BlockSpec((None, tm, D), lambda b, i: (b, i, 0)),
        out_shape=jax.ShapeDtypeStruct((B, T, D), F32),
        compiler_params=_params(("parallel", "parallel")),
        name="mix_out",
    )(y_sb, y_hg, p, p, p, p, w_sb, w_hg, w_out, x, g, cond)


def _mlp_kernel(x_ref, g1_ref, g2_ref, cond_ref, wup_ref, wdn_ref, o_ref,
                h_ref, acc_ref):
    j = pl.program_id(2)

    @pl.when(j == 0)
    def _():
        _norm_modulate_to(h_ref, x_ref, g1_ref, cond_ref[3:4, :], cond_ref[4:5, :])
        acc_ref[...] = jnp.zeros_like(acc_ref)

    u = jnp.maximum(jnp.dot(h_ref[...], wup_ref[...], preferred_element_type=F32), 0.0)
    acc_ref[...] += jnp.dot((u * u).astype(BF16), wdn_ref[...], preferred_element_type=F32)

    @pl.when(j == pl.num_programs(2) - 1)
    def _():
        gate = cond_ref[5:6, :]
        g2 = g2_ref[...]

        def chunk(rows):
            o_ref[rows, :] = x_ref[rows, :] + gate * _rms(acc_ref[rows, :], g2)

        _for_row_chunks(acc_ref.shape[0], 32, chunk)


def _mlp(x, g1, g2, cond, w_up, w_dn, layer, tm=512, tf=1024):
    B, T, D = x.shape
    F = w_up.shape[2]
    return pl.pallas_call(
        _mlp_kernel,
        grid=(B, T // tm, F // tf),
        in_specs=[
            pl.BlockSpec((None, tm, D), lambda b, i, j: (b, i, 0)),
            pl.BlockSpec((1, D), lambda b, i, j: (0, 0)),
            pl.BlockSpec((1, D), lambda b, i, j: (0, 0)),
            pl.BlockSpec((None, 6, D), lambda b, i, j: (b, 0, 0)),
            pl.BlockSpec((None, D, tf), lambda b, i, j: (layer, 0, j)),
            pl.BlockSpec((None, tf, D), lambda b, i, j: (layer, j, 0)),
        ],
        out_specs=pl.BlockSpec((None, tm, D), lambda b, i, j: (b, i, 0)),
        out_shape=jax.ShapeDtypeStruct((B, T, D), F32),
        scratch_shapes=[pltpu.VMEM((tm, D), BF16), pltpu.VMEM((tm, D), F32)],
        compiler_params=_params(("parallel", "parallel", "arbitrary")),
        name="mlp",
    )(x, g1, g2, cond, w_up, w_dn)


def kernel(x, c, w_ada, b_ada, g_pre_mix, g_post_mix, w_in, hg_lb_logits, g_hg_norm,
           w_proj_sb, w_proj_hg, w_out, g_pre_mlp, g_post_mlp, w_mlp_up, w_mlp_down):
    B, T, D = x.shape
    L = w_ada.shape[0]

    lbs = jnp.cumsum(jax.nn.softmax(hg_lb_logits.astype(F32), axis=0), axis=0)
    lbs = lbs - lbs[0:1]
    lb_rows = jnp.stack([lbs, 1.0 - lbs], axis=1)

    c_pad = jnp.pad(c, ((0, 8 - B), (0, 0)))
    cond_all = _ada_cond(c_pad, w_ada, b_ada)[:, :B, :].reshape(L, B, 6, D)

    w_in, w_proj_sb, w_proj_hg, w_out, w_mlp_up, w_mlp_down = (
        w.astype(BF16) for w in (w_in, w_proj_sb, w_proj_hg, w_out, w_mlp_up, w_mlp_down))

    for l in range(L):
        cond = cond_all[l]
        p = _in_proj(x, g_pre_mix[l].reshape(1, D), cond, w_in, l)
        y_sb = _sb_attention(p)
        y_hg = _hgrn(p, lb_rows[l], g_hg_norm[l].reshape(1, -1))
        x = _mix_out(y_sb, y_hg, p, w_proj_sb, w_proj_hg, w_out, l,
                     x, g_post_mix[l].reshape(1, D), cond)
        x = _mlp(x, g_pre_mlp[l].reshape(1, D), g_post_mlp[l].reshape(1, D), cond,
                 w_mlp_up, w_mlp_down, l)
    return x
```

```python
import functools
import math

import jax
import jax.numpy as jnp
from jax import lax
from jax.experimental import pallas as pl
from jax.experimental.pallas import tpu as pltpu

F32 = jnp.float32
BF16 = jnp.bfloat16

EPS = 1e-6
HEAD_DIM = 128
SB_HEADS = 8
HG_HEADS = 8
HG_CHUNK = 32

SB_Q_BLK, SB_K_BLK, SB_V_BLK = 0, 8, 16
HG_Q_BLK, HG_F_BLK, HG_I_BLK, HG_G_BLK = 24, 32, 40, 48
GATE_A_COL, GATE_B_COL = 7168, 9216

VMEM_LIMIT = 56 * 1024 * 1024

SKIP_BELOW = -104.0
LOG2E = 1.4426950408889634

NT_DIMS = (((1,), (1,)), ((), ()))
TN_DIMS = (((0,), (0,)), ((), ()))


def _params(sem):
    return pltpu.CompilerParams(dimension_semantics=sem, vmem_limit_bytes=VMEM_LIMIT)


def _rms(xf, g):
    ms = jnp.mean(xf * xf, axis=-1, keepdims=True)
    return xf * lax.rsqrt(ms + EPS) * g


def _silu(x):
    hx = 0.5 * x
    return hx + hx * jnp.tanh(hx)


def _for_row_chunks(n_rows, chunk_rows, fn):
    def body(r, carry):
        fn(pl.ds(pl.multiple_of(r * chunk_rows, chunk_rows), chunk_rows))
        return carry
    lax.fori_loop(0, n_rows // chunk_rows, body, 0, unroll=4)


def _norm_modulate_to(h_ref, x_ref, g_ref, shift, scale):
    gmod = g_ref[...] * (1.0 + scale)

    def chunk(rows):
        x = x_ref[rows, :]
        ms = jnp.mean(x * x, axis=-1, keepdims=True)
        h_ref[rows, :] = (x * lax.rsqrt(ms + EPS) * gmod + shift).astype(h_ref.dtype)

    _for_row_chunks(x_ref.shape[0], 32, chunk)


def _split_bf16(a):
    hi = a.astype(BF16)
    lo = (a - hi.astype(F32)).astype(BF16)
    return hi, lo


def _ada_kernel(c_ref, w_ref, b_ref, o_ref):
    c = c_ref[...]
    ca = (c * jax.nn.sigmoid(c)).astype(BF16)
    o_ref[...] = jnp.dot(ca, w_ref[...].astype(BF16),
                         preferred_element_type=F32) + b_ref[...]


def _ada_cond(c_pad, w_ada, b_ada, tn=1024):
    L, D, N = w_ada.shape
    M = c_pad.shape[0]
    return pl.pallas_call(
        _ada_kernel,
        grid=(L, N // tn),
        in_specs=[
            pl.BlockSpec((M, D), lambda l, j: (0, 0)),
            pl.BlockSpec((None, D, tn), lambda l, j: (l, 0, j)),
            pl.BlockSpec((None, 1, tn), lambda l, j: (l, 0, j)),
        ],
        out_specs=pl.BlockSpec((None, M, tn), lambda l, j: (l, 0, j)),
        out_shape=jax.ShapeDtypeStruct((L, M, N), F32),
        compiler_params=_params(("parallel", "parallel")),
        name="ada_cond",
    )(c_pad, w_ada, b_ada.reshape(L, 1, N))


def _in_proj_kernel(x_ref, g_ref, cond_ref, w_ref, o_ref, h_ref):
    @pl.when(pl.program_id(2) == 0)
    def _():
        _norm_modulate_to(h_ref, x_ref, g_ref, cond_ref[0:1, :], cond_ref[1:2, :])

    o_ref[...] = jnp.dot(h_ref[...], w_ref[...],
                         preferred_element_type=F32).astype(o_ref.dtype)


def _in_proj(x, g, cond, w, layer, tm=1024, tn=1024):
    B, T, D = x.shape
    N = w.shape[2]
    return pl.pallas_call(
        _in_proj_kernel,
        grid=(B, T // tm, N // tn),
        in_specs=[
            pl.BlockSpec((None, tm, D), lambda b, i, j: (b, i, 0)),
            pl.BlockSpec((1, D), lambda b, i, j: (0, 0)),
            pl.BlockSpec((None, 6, D), lambda b, i, j: (b, 0, 0)),
            pl.BlockSpec((None, D, tn), lambda b, i, j: (layer, 0, j)),
        ],
        out_specs=pl.BlockSpec((None, tm, tn), lambda b, i, j: (b, i, j)),
        out_shape=jax.ShapeDtypeStruct((B, T, N), BF16),
        scratch_shapes=[pltpu.VMEM((tm, D), BF16)],
        compiler_params=_params(("parallel", "parallel", "arbitrary")),
        name="in_proj",
    )(x, g, cond, w)


def _sb_kernel(q_ref, k_ref, v_ref, u_ref, o_ref, *, tile, heads, scale):
    i = pl.program_id(2)
    u = u_ref[...]
    row = lax.broadcasted_iota(jnp.int32, (tile, tile), 0)
    col = lax.broadcasted_iota(jnp.int32, (tile, tile), 1)
    causal = col < row

    def tile_terms(h, j, diagonal):
        lanes = slice(h * HEAD_DIM, (h + 1) * HEAD_DIM)
        keys = pl.ds(pl.multiple_of(j * tile, tile), tile)
        s = lax.dot_general(q_ref[:, lanes], k_ref[keys, lanes], NT_DIMS,
                            preferred_element_type=F32)
        e = jnp.exp2(jnp.abs(s) * (-scale * LOG2E))
        ln = jnp.maximum(s, 0.0) * (-scale) - jnp.log(1.0 + e)
        if diagonal:
            ln = jnp.where(causal, ln, 0.0)
        hi, lo = _split_bf16(ln)
        incl = jnp.dot(jnp.concatenate([hi, lo], axis=1), u, preferred_element_type=F32)
        a = jnp.exp(s * scale + incl)
        if diagonal:
            a = jnp.where(causal, a, 0.0)
        pv = jnp.dot(a.astype(BF16), v_ref[keys, lanes], preferred_element_type=F32)
        return pv, incl[:, 0:1]

    has_prev = i > 0
    jp = jnp.maximum(i - 1, 0)
    carries, accs = [], []
    for h in range(heads):
        pv_d, tot_d = tile_terms(h, i, True)
        pv_p, tot_p = tile_terms(h, jp, False)
        accs.append(pv_d + jnp.where(has_prev, jnp.exp(tot_d), 0.0) * pv_p)
        carries.append(tot_d + jnp.where(has_prev, tot_p, 0.0))

    tops = [jnp.max(c) for c in carries]
    for h in range(heads):
        def live(state):
            j, top, _, _ = state
            return (j >= 0) & (top >= SKIP_BELOW)

        def step(state, h=h):
            j, _, carry, acc = state
            pv, tot = tile_terms(h, j, False)
            carry_new = carry + tot
            return j - 1, jnp.max(carry_new), carry_new, acc + jnp.exp(carry) * pv

        _, _, _, acc = lax.while_loop(live, step, (i - 2, tops[h], carries[h], accs[h]))
        o_ref[:, h * HEAD_DIM:(h + 1) * HEAD_DIM] = acc.astype(o_ref.dtype)


def _sb_attention(p, tile=256, heads=8):
    B, T, _ = p.shape
    idx = jnp.arange(tile)
    u1 = (idx[:, None] >= idx[None, :]).astype(BF16)
    u = jnp.concatenate([u1, u1], axis=0)
    kern = functools.partial(_sb_kernel, tile=tile, heads=heads,
                             scale=1.0 / math.sqrt(HEAD_DIM))
    w = heads * HEAD_DIM
    groups = SB_HEADS // heads
    return pl.pallas_call(
        kern,
        grid=(B, groups, T // tile),
        in_specs=[
            pl.BlockSpec((None, tile, w), lambda b, g, i: (b, i, SB_Q_BLK // heads + g)),
            pl.BlockSpec((None, T, w), lambda b, g, i: (b, 0, SB_K_BLK // heads + g)),
            pl.BlockSpec((None, T, w), lambda b, g, i: (b, 0, SB_V_BLK // heads + g)),
            pl.BlockSpec((2 * tile, tile), lambda b, g, i: (0, 0)),
        ],
        out_specs=pl.BlockSpec((None, tile, w), lambda b, g, i: (b, i, g)),
        out_shape=jax.ShapeDtypeStruct((B, T, SB_HEADS * HEAD_DIM), BF16),
        compiler_params=_params(("parallel", "parallel", "arbitrary")),
        name="sb_attention",
    )(p, p, p, u)


def _hgrn_kernel(hq_ref, hf_ref, hi_ref, hg_ref, lb_ref, gn_ref, tri_ref, o_ref,
                 st_ref, *, tb, chunk, heads):
    @pl.when(pl.program_id(2) == 0)
    def _():
        st_ref[...] = jnp.zeros_like(st_ref)

    hq = hq_ref[...].astype(F32)
    hf = hf_ref[...].astype(F32)
    lb = lb_ref[0:1, :]
    one_m_lb = lb_ref[1:2, :]

    e = jnp.exp(-jnp.abs(hf))
    r = 1.0 / (1.0 + e)
    er = e * r
    pos = hf >= 0.0
    logf = jnp.log(lb + one_m_lb * jnp.where(pos, r, er))
    kk_all = one_m_lb * jnp.where(pos, er, r)
    qq = _silu(hq)

    hi, lo = _split_bf16(logf)
    g_all = jnp.dot(tri_ref[...], jnp.concatenate([hi, lo], axis=0),
                    preferred_element_type=F32)
    qg_all = (qq * jnp.exp(g_all)).astype(BF16)
    kg_all = (kk_all * jnp.exp(-g_all)).astype(BF16)

    row = lax.broadcasted_iota(jnp.int32, (tb, tb), 0)
    col = lax.broadcasted_iota(jnp.int32, (tb, tb), 1)
    keep = (col <= row) & ((row // chunk) == (col // chunk))
    chunks = [slice(n * chunk, (n + 1) * chunk) for n in range(tb // chunk)]

    for h in range(heads):
        lanes = slice(h * HEAD_DIM, (h + 1) * HEAD_DIM)
        g, kk, qg = g_all[:, lanes], kk_all[:, lanes], qg_all[:, lanes]
        v = hi_ref[:, lanes]

        scores = lax.dot_general(qg, kg_all[:, lanes], NT_DIMS, preferred_element_type=F32)
        scores = jnp.where(keep, scores, 0.0).astype(BF16)
        o = jnp.dot(scores, v, preferred_element_type=F32)

        upds, decays = [], []
        for sl in chunks:
            g_end = g[sl.stop - 1:sl.stop, :]
            kdec = (kk[sl] * jnp.exp(g_end - g[sl])).astype(BF16)
            upds.append(lax.dot_general(v[sl], kdec, TN_DIMS, preferred_element_type=F32))
            decays.append(jnp.exp(g_end))

        st = st_ref[h]
        inter = []
        for sl, upd, dec in zip(chunks, upds, decays):
            inter.append(lax.dot_general(qg[sl], st.astype(BF16), NT_DIMS,
                                         preferred_element_type=F32))
            st = dec * st + upd
        st_ref[h] = st

        o = o + jnp.concatenate(inter, axis=0)
        hg = hg_ref[:, lanes].astype(F32)
        y = _rms(o, gn_ref[:, lanes]) * _silu(hg)
        o_ref[:, lanes] = y.astype(o_ref.dtype)


def _hgrn(p, lb_rows, gn, tb=256, heads=8):
    B, T, _ = p.shape
    c = HG_CHUNK
    idx = jnp.arange(tb)
    same = (idx[:, None] // c) == (idx[None, :] // c)
    tri1 = (same & (idx[None, :] <= idx[:, None])).astype(BF16)
    tri = jnp.concatenate([tri1, tri1], axis=1)
    kern = functools.partial(_hgrn_kernel, tb=tb, chunk=c, heads=heads)
    w = heads * HEAD_DIM
    groups = HG_HEADS // heads

    def col(blk):
        return pl.BlockSpec((None, tb, w), lambda b, g, i: (b, i, blk // heads + g))

    return pl.pallas_call(
        kern,
        grid=(B, groups, T // tb),
        in_specs=[
            col(HG_Q_BLK), col(HG_F_BLK), col(HG_I_BLK), col(HG_G_BLK),
            pl.BlockSpec((2, w), lambda b, g, i: (0, g)),
            pl.BlockSpec((1, w), lambda b, g, i: (0, g)),
            pl.BlockSpec((tb, 2 * tb), lambda b, g, i: (0, 0)),
        ],
        out_specs=pl.BlockSpec((None, tb, w), lambda b, g, i: (b, i, g)),
        out_shape=jax.ShapeDtypeStruct((B, T, HG_HEADS * HEAD_DIM), BF16),
        scratch_shapes=[pltpu.VMEM((heads, HEAD_DIM, HEAD_DIM), F32)],
        compiler_params=_params(("parallel", "parallel", "arbitrary")),
        name="hgrn2",
    )(p, p, p, p, lb_rows, gn, tri)


def _mix_out_kernel(ysb_ref, yhg_ref, ga0_ref, ga1_ref, gb0_ref, gb1_ref,
                    wsb_ref, whg_ref, wout_ref, x_ref, g_ref, cond_ref, o_ref):
    ysb = ysb_ref[...]
    yhg = yhg_ref[...]
    half = ga0_ref.shape[-1]
    mo = None
    for c, (ga_ref, gb_ref) in enumerate(((ga0_ref, gb0_ref), (ga1_ref, gb1_ref))):
        cols = slice(c * half, (c + 1) * half)
        a = jnp.dot(ysb, wsb_ref[:, cols], preferred_element_type=F32)
        b = jnp.dot(yhg, whg_ref[:, cols], preferred_element_type=F32)
        m = (jax.nn.sigmoid(ga_ref[...].astype(F32)) * a
             + jax.nn.sigmoid(gb_ref[...].astype(F32)) * b)
        part = jnp.dot(m.astype(BF16), wout_ref[cols, :], preferred_element_type=F32)
        mo = part if mo is None else mo + part
    o_ref[...] = x_ref[...] + cond_ref[2:3, :] * _rms(mo, g_ref[...])


def _mix_out(y_sb, y_hg, p, w_sb, w_hg, w_out, layer, x, g, cond, tm=512):
    B, T, D = x.shape
    W = y_sb.shape[-1]
    half = D // 2
    ga_blk, gb_blk = GATE_A_COL // half, GATE_B_COL // half

    def gate(blk):
        return pl.BlockSpec((None, tm, half), lambda b, i: (b, i, blk))

    def resident(shape):
        return pl.BlockSpec((None,) + shape, lambda b, i: (layer, 0, 0),
                            pipeline_mode=pl.Buffered(1))

    return pl.pallas_call(
        _mix_out_kernel,
        grid=(B, T // tm),
        in_specs=[
            pl.BlockSpec((None, tm, W), lambda b, i: (b, i, 0)),
            pl.BlockSpec((None, tm, W), lambda b, i: (b, i, 0)),
            gate(ga_blk), gate(ga_blk + 1), gate(gb_blk), gate(gb_blk + 1),
            resident((W, D)), resident((W, D)), resident((D, D)),
            pl.BlockSpec((None, tm, D), lambda b, i: (b, i, 0)),
            pl.BlockSpec((1, D), lambda b, i: (0, 0)),
            pl.BlockSpec((None, 6, D), lambda b, i: (b, 0, 0)),
        ],
        out_specs=pl.BlockSpec((None, tm, D), lambda b, i: (b, i, 0)),
        out_shape=jax.ShapeDtypeStruct((B, T, D), F32),
        compiler_params=_params(("parallel", "parallel")),
        name="mix_out",
    )(y_sb, y_hg, p, p, p, p, w_sb, w_hg, w_out, x, g, cond)


def _mlp_kernel(x_ref, g1_ref, g2_ref, cond_ref, wup_ref, wdn_ref, o_ref,
                h_ref, acc_ref):
    j = pl.program_id(2)

    @pl.when(j == 0)
    def _():
        _norm_modulate_to(h_ref, x_ref, g1_ref, cond_ref[3:4, :], cond_ref[4:5, :])
        acc_ref[...] = jnp.zeros_like(acc_ref)

    u = jnp.maximum(jnp.dot(h_ref[...], wup_ref[...], preferred_element_type=F32), 0.0)
    acc_ref[...] += jnp.dot((u * u).astype(BF16), wdn_ref[...], preferred_element_type=F32)

    @pl.when(j == pl.num_programs(2) - 1)
    def _():
        gate = cond_ref[5:6, :]
        g2 = g2_ref[...]

        def chunk(rows):
            o_ref[rows, :] = x_ref[rows, :] + gate * _rms(acc_ref[rows, :], g2)

        _for_row_chunks(acc_ref.shape[0], 32, chunk)


def _mlp(x, g1, g2, cond, w_up, w_dn, layer, tm=512, tf=1024):
    B, T, D = x.shape
    F = w_up.shape[2]
    return pl.pallas_call(
        _mlp_kernel,
        grid=(B, T // tm, F // tf),
        in_specs=[
            pl.BlockSpec((None, tm, D), lambda b, i, j: (b, i, 0)),
            pl.BlockSpec((1, D), lambda b, i, j: (0, 0)),
            pl.BlockSpec((1, D), lambda b, i, j: (0, 0)),
            pl.BlockSpec((None, 6, D), lambda b, i, j: (b, 0, 0)),
            pl.BlockSpec((None, D, tf), lambda b, i, j: (layer, 0, j)),
            pl.BlockSpec((None, tf, D), lambda b, i, j: (layer, j, 0)),
        ],
        out_specs=pl.BlockSpec((None, tm, D), lambda b, i, j: (b, i, 0)),
        out_shape=jax.ShapeDtypeStruct((B, T, D), F32),
        scratch_shapes=[pltpu.VMEM((tm, D), BF16), pltpu.VMEM((tm, D), F32)],
        compiler_params=_params(("parallel", "parallel", "arbitrary")),
        name="mlp",
    )(x, g1, g2, cond, w_up, w_dn)


def kernel(x, c, w_ada, b_ada, g_pre_mix, g_post_mix, w_in, hg_lb_logits, g_hg_norm,
           w_proj_sb, w_proj_hg, w_out, g_pre_mlp, g_post_mlp, w_mlp_up, w_mlp_down):
    B, T, D = x.shape
    L = w_ada.shape[0]

    lbs = jnp.cumsum(jax.nn.softmax(hg_lb_logits.astype(F32), axis=0), axis=0)
    lbs = lbs - lbs[0:1]
    lb_rows = jnp.stack([lbs, 1.0 - lbs], axis=1)

    c_pad = jnp.pad(c, ((0, 8 - B), (0, 0)))
    cond_all = _ada_cond(c_pad, w_ada, b_ada)[:, :B, :].reshape(L, B, 6, D)

    w_in, w_proj_sb, w_proj_hg, w_out, w_mlp_up, w_mlp_down = (
        w.astype(BF16) for w in (w_in, w_proj_sb, w_proj_hg, w_out, w_mlp_up, w_mlp_down))

    for l in range(L):
        cond = cond_all[l]
        p = _in_proj(x, g_pre_mix[l].reshape(1, D), cond, w_in, l)
        y_sb = _sb_attention(p)
        y_hg = _hgrn(p, lb_rows[l], g_hg_norm[l].reshape(1, -1))
        x = _mix_out(y_sb, y_hg, p, w_proj_sb, w_proj_hg, w_out, l,
                     x, g_post_mix[l].reshape(1, D), cond)
        x = _mlp(x, g_pre_mlp[l].reshape(1, D), g_post_mlp[l].reshape(1, D), cond,
                 w_mlp_up, w_mlp_down, l)
    return x
```

```python
import functools
import math

import jax
import jax.numpy as jnp
from jax import lax
from jax.experimental import pallas as pl
from jax.experimental.pallas import tpu as pltpu

F32 = jnp.float32
BF16 = jnp.bfloat16

EPS = 1e-6
HEAD_DIM = 128
SB_HEADS = 8
HG_HEADS = 8
HG_CHUNK = 32

SB_Q_BLK, SB_K_BLK, SB_V_BLK = 0, 8, 16
HG_Q_BLK, HG_F_BLK, HG_I_BLK, HG_G_BLK = 24, 32, 40, 48
GATE_A_COL, GATE_B_COL = 7168, 9216

VMEM_LIMIT = 56 * 1024 * 1024

SKIP_BELOW = -104.0
LOG2E = 1.4426950408889634

NT_DIMS = (((1,), (1,)), ((), ()))
TN_DIMS = (((0,), (0,)), ((), ()))


def _params(sem):
    return pltpu.CompilerParams(dimension_semantics=sem, vmem_limit_bytes=VMEM_LIMIT)


def _rms(xf, g):
    ms = jnp.mean(xf * xf, axis=-1, keepdims=True)
    return xf * lax.rsqrt(ms + EPS) * g


def _silu(x):
    hx = 0.5 * x
    return hx + hx * jnp.tanh(hx)


def _for_row_chunks(n_rows, chunk_rows, fn):
    def body(r, carry):
        fn(pl.ds(pl.multiple_of(r * chunk_rows, chunk_rows), chunk_rows))
        return carry
    lax.fori_loop(0, n_rows // chunk_rows, body, 0, unroll=4)


def _norm_modulate_to(h_ref, x_ref, g_ref, shift, scale):
    gmod = g_ref[...] * (1.0 + scale)

    def chunk(rows):
        x = x_ref[rows, :]
        ms = jnp.mean(x * x, axis=-1, keepdims=True)
        h_ref[rows, :] = (x * lax.rsqrt(ms + EPS) * gmod + shift).astype(h_ref.dtype)

    _for_row_chunks(x_ref.shape[0], 32, chunk)


def _split_bf16(a):
    hi = a.astype(BF16)
    lo = (a - hi.astype(F32)).astype(BF16)
    return hi, lo


def _ada_kernel(c_ref, w_ref, b_ref, o_ref):
    c = c_ref[...]
    ca = (c * jax.nn.sigmoid(c)).astype(BF16)
    o_ref[...] = jnp.dot(ca, w_ref[...].astype(BF16),
                         preferred_element_type=F32) + b_ref[...]


def _ada_cond(c_pad, w_ada, b_ada, tn=1024):
    L, D, N = w_ada.shape
    M = c_pad.shape[0]
    return pl.pallas_call(
        _ada_kernel,
        grid=(L, N // tn),
        in_specs=[
            pl.BlockSpec((M, D), lambda l, j: (0, 0)),
            pl.BlockSpec((None, D, tn), lambda l, j: (l, 0, j)),
            pl.BlockSpec((None, 1, tn), lambda l, j: (l, 0, j)),
        ],
        out_specs=pl.BlockSpec((None, M, tn), lambda l, j: (l, 0, j)),
        out_shape=jax.ShapeDtypeStruct((L, M, N), F32),
        compiler_params=_params(("parallel", "parallel")),
        name="ada_cond",
    )(c_pad, w_ada, b_ada.reshape(L, 1, N))


def _in_proj_kernel(x_ref, g_ref, cond_ref, w_ref, o_ref, h_ref):
    @pl.when(pl.program_id(2) == 0)
    def _():
        _norm_modulate_to(h_ref, x_ref, g_ref, cond_ref[0:1, :], cond_ref[1:2, :])

    o_ref[...] = jnp.dot(h_ref[...], w_ref[...],
                         preferred_element_type=F32).astype(o_ref.dtype)


def _in_proj(x, g, cond, w, layer, tm=512, tn=2816):
    B, T, D = x.shape
    N = w.shape[2]
    return pl.pallas_call(
        _in_proj_kernel,
        grid=(B, T // tm, N // tn),
        in_specs=[
            pl.BlockSpec((None, tm, D), lambda b, i, j: (b, i, 0)),
            pl.BlockSpec((1, D), lambda b, i, j: (0, 0)),
            pl.BlockSpec((None, 6, D), lambda b, i, j: (b, 0, 0)),
            pl.BlockSpec((None, D, tn), lambda b, i, j: (layer, 0, j)),
        ],
        out_specs=pl.BlockSpec((None, tm, tn), lambda b, i, j: (b, i, j)),
        out_shape=jax.ShapeDtypeStruct((B, T, N), BF16),
        scratch_shapes=[pltpu.VMEM((tm, D), BF16)],
        compiler_params=_params(("parallel", "parallel", "arbitrary")),
        name="in_proj",
    )(x, g, cond, w)


def _sb_kernel(q_ref, k_ref, v_ref, u_ref, o_ref, *, tile, heads, scale):
    i = pl.program_id(2)
    u = u_ref[...]
    row = lax.broadcasted_iota(jnp.int32, (tile, tile), 0)
    col = lax.broadcasted_iota(jnp.int32, (tile, tile), 1)
    causal = col < row

    def tile_terms(h, j, diagonal):
        lanes = slice(h * HEAD_DIM, (h + 1) * HEAD_DIM)
        keys = pl.ds(pl.multiple_of(j * tile, tile), tile)
        s = lax.dot_general(q_ref[:, lanes], k_ref[keys, lanes], NT_DIMS,
                            preferred_element_type=F32)
        e = jnp.exp2(jnp.abs(s) * (-scale * LOG2E))
        ln = jnp.maximum(s, 0.0) * (-scale) - jnp.log(1.0 + e)
        if diagonal:
            ln = jnp.where(causal, ln, 0.0)
        hi, lo = _split_bf16(ln)
        incl = jnp.dot(jnp.concatenate([hi, lo], axis=1), u, preferred_element_type=F32)
        a = jnp.exp(s * scale + incl)
        if diagonal:
            a = jnp.where(causal, a, 0.0)
        pv = jnp.dot(a.astype(BF16), v_ref[keys, lanes], preferred_element_type=F32)
        return pv, incl[:, 0:1]

    has_prev = i > 0
    jp = jnp.maximum(i - 1, 0)
    carries, accs = [], []
    for h in range(heads):
        pv_d, tot_d = tile_terms(h, i, True)
        pv_p, tot_p = tile_terms(h, jp, False)
        accs.append(pv_d + jnp.where(has_prev, jnp.exp(tot_d), 0.0) * pv_p)
        carries.append(tot_d + jnp.where(has_prev, tot_p, 0.0))

    tops = [jnp.max(c) for c in carries]
    for h in range(heads):
        def live(state):
            j, top, _, _ = state
            return (j >= 0) & (top >= SKIP_BELOW)

        def step(state, h=h):
            j, _, carry, acc = state
            pv, tot = tile_terms(h, j, False)
            carry_new = carry + tot
            return j - 1, jnp.max(carry_new), carry_new, acc + jnp.exp(carry) * pv

        _, _, _, acc = lax.while_loop(live, step, (i - 2, tops[h], carries[h], accs[h]))
        o_ref[:, h * HEAD_DIM:(h + 1) * HEAD_DIM] = acc.astype(o_ref.dtype)


def _sb_attention(p, tile=256, heads=8):
    B, T, _ = p.shape
    idx = jnp.arange(tile)
    u1 = (idx[:, None] >= idx[None, :]).astype(BF16)
    u = jnp.concatenate([u1, u1], axis=0)
    kern = functools.partial(_sb_kernel, tile=tile, heads=heads,
                             scale=1.0 / math.sqrt(HEAD_DIM))
    w = heads * HEAD_DIM
    groups = SB_HEADS // heads
    return pl.pallas_call(
        kern,
        grid=(B, groups, T // tile),
        in_specs=[
            pl.BlockSpec((None, tile, w), lambda b, g, i: (b, i, SB_Q_BLK // heads + g)),
            pl.BlockSpec((None, T, w), lambda b, g, i: (b, 0, SB_K_BLK // heads + g)),
            pl.BlockSpec((None, T, w), lambda b, g, i: (b, 0, SB_V_BLK // heads + g)),
            pl.BlockSpec((2 * tile, tile), lambda b, g, i: (0, 0)),
        ],
        out_specs=pl.BlockSpec((None, tile, w), lambda b, g, i: (b, i, g)),
        out_shape=jax.ShapeDtypeStruct((B, T, SB_HEADS * HEAD_DIM), BF16),
        compiler_params=_params(("parallel", "parallel", "arbitrary")),
        name="sb_attention",
    )(p, p, p, u)


def _hgrn_kernel(hq_ref, hf_ref, hi_ref, hg_ref, lb_ref, gn_ref, tri_ref, o_ref,
                 st_ref, *, tb, chunk, heads):
    @pl.when(pl.program_id(2) == 0)
    def _():
        st_ref[...] = jnp.zeros_like(st_ref)

    hf = hf_ref[...].astype(F32)
    lb = lb_ref[0:1, :]
    one_m_lb = lb_ref[1:2, :]

    e = jnp.exp(-jnp.abs(hf))
    r = 1.0 / (1.0 + e)
    er = e * r
    pos = hf >= 0.0
    logf = jnp.log(lb + one_m_lb * jnp.where(pos, r, er))
    kk_all = one_m_lb * jnp.where(pos, er, r)
    qq = _silu(hq_ref[...].astype(F32))

    hi, lo = _split_bf16(logf)
    g_all = jnp.dot(tri_ref[...], jnp.concatenate([hi, lo], axis=0),
                    preferred_element_type=F32)
    qg_all = (qq * jnp.exp(g_all)).astype(BF16)
    kg_all = (kk_all * jnp.exp(-g_all)).astype(BF16)

    row = lax.broadcasted_iota(jnp.int32, (tb, tb), 0)
    col = lax.broadcasted_iota(jnp.int32, (tb, tb), 1)
    keep = (col <= row) & ((row // chunk) == (col // chunk))
    chunks = [slice(n * chunk, (n + 1) * chunk) for n in range(tb // chunk)]

    for h in range(heads):
        lanes = slice(h * HEAD_DIM, (h + 1) * HEAD_DIM)
        g, kk, qg = g_all[:, lanes], kk_all[:, lanes], qg_all[:, lanes]
        v = hi_ref[:, lanes]

        scores = lax.dot_general(qg, kg_all[:, lanes], NT_DIMS, preferred_element_type=F32)
        scores = jnp.where(keep, scores, 0.0).astype(BF16)
        o = jnp.dot(scores, v, preferred_element_type=F32)

        upds, decays = [], []
        for sl in chunks:
            g_end = g[sl.stop - 1:sl.stop, :]
            kdec = (kk[sl] * jnp.exp(g_end - g[sl])).astype(BF16)
            upds.append(lax.dot_general(v[sl], kdec, TN_DIMS, preferred_element_type=F32))
            decays.append(jnp.exp(g_end))

        st = st_ref[h]
        inter = []
        for sl, upd, dec in zip(chunks, upds, decays):
            inter.append(lax.dot_general(qg[sl], st.astype(BF16), NT_DIMS,
                                         preferred_element_type=F32))
            st = dec * st + upd
        st_ref[h] = st

        o = o + jnp.concatenate(inter, axis=0)
        y = _rms(o, gn_ref[:, lanes]) * _silu(hg_ref[:, lanes].astype(F32))
        o_ref[:, lanes] = y.astype(o_ref.dtype)


def _hgrn(p, lb_rows, gn, tb=256, heads=8):
    B, T, _ = p.shape
    c = HG_CHUNK
    idx = jnp.arange(tb)
    same = (idx[:, None] // c) == (idx[None, :] // c)
    tri1 = (same & (idx[None, :] <= idx[:, None])).astype(BF16)
    tri = jnp.concatenate([tri1, tri1], axis=1)
    kern = functools.partial(_hgrn_kernel, tb=tb, chunk=c, heads=heads)
    w = heads * HEAD_DIM
    groups = HG_HEADS // heads

    def col(blk):
        return pl.BlockSpec((None, tb, w), lambda b, g, i: (b, i, blk // heads + g))

    return pl.pallas_call(
        kern,
        grid=(B, groups, T // tb),
        in_specs=[
            col(HG_Q_BLK), col(HG_F_BLK), col(HG_I_BLK), col(HG_G_BLK),
            pl.BlockSpec((2, w), lambda b, g, i: (0, g)),
            pl.BlockSpec((1, w), lambda b, g, i: (0, g)),
            pl.BlockSpec((tb, 2 * tb), lambda b, g, i: (0, 0)),
        ],
        out_specs=pl.BlockSpec((None, tb, w), lambda b, g, i: (b, i, g)),
        out_shape=jax.ShapeDtypeStruct((B, T, HG_HEADS * HEAD_DIM), BF16),
        scratch_shapes=[pltpu.VMEM((heads, HEAD_DIM, HEAD_DIM), F32)],
        compiler_params=_params(("parallel", "parallel", "arbitrary")),
        name="hgrn2",
    )(p, p, p, p, lb_rows, gn, tri)


def _mix_out_kernel(ysb_ref, yhg_ref, ga0_ref, ga1_ref, gb0_ref, gb1_ref,
                    wsb_ref, whg_ref, wout_ref, x_ref, g_ref, cond_ref, o_ref):
    ysb = ysb_ref[...]
    yhg = yhg_ref[...]
    half = ga0_ref.shape[-1]
    mo = None
    for c, (ga_ref, gb_ref) in enumerate(((ga0_ref, gb0_ref), (ga1_ref, gb1_ref))):
        cols = slice(c * half, (c + 1) * half)
        a = jnp.dot(ysb, wsb_ref[:, cols], preferred_element_type=F32)
        b = jnp.dot(yhg, whg_ref[:, cols], preferred_element_type=F32)
        m = (jax.nn.sigmoid(ga_ref[...].astype(F32)) * a
             + jax.nn.sigmoid(gb_ref[...].astype(F32)) * b)
        part = jnp.dot(m.astype(BF16), wout_ref[cols, :], preferred_element_type=F32)
        mo = part if mo is None else mo + part
    o_ref[...] = x_ref[...] + cond_ref[2:3, :] * _rms(mo, g_ref[...])


def _mix_out(y_sb, y_hg, p, w_sb, w_hg, w_out, layer, x, g, cond, tm=512):
    B, T, D = x.shape
    W = y_sb.shape[-1]
    half = D // 2
    ga_blk, gb_blk = GATE_A_COL // half, GATE_B_COL // half

    def gate(blk):
        return pl.BlockSpec((None, tm, half), lambda b, i: (b, i, blk))

    def resident(shape):
        return pl.BlockSpec((None,) + shape, lambda b, i: (layer, 0, 0),
                            pipeline_mode=pl.Buffered(1))

    return pl.pallas_call(
        _mix_out_kernel,
        grid=(B, T // tm),
        in_specs=[
            pl.BlockSpec((None, tm, W), lambda b, i: (b, i, 0)),
            pl.BlockSpec((None, tm, W), lambda b, i: (b, i, 0)),
            gate(ga_blk), gate(ga_blk + 1), gate(gb_blk), gate(gb_blk + 1),
            resident((W, D)), resident((W, D)), resident((D, D)),
            pl.BlockSpec((None, tm, D), lambda b, i: (b, i, 0)),
            pl.BlockSpec((1, D), lambda b, i: (0, 0)),
            pl.BlockSpec((None, 6, D), lambda b, i: (b, 0, 0)),
        ],
        out_specs=pl.BlockSpec((None, tm, D), lambda b, i: (b, i, 0)),
        out_shape=jax.ShapeDtypeStruct((B, T, D), F32),
        compiler_params=_params(("parallel", "parallel")),
        name="mix_out",
    )(y_sb, y_hg, p, p, p, p, w_sb, w_hg, w_out, x, g, cond)


def _mlp_kernel(x_ref, g1_ref, g2_ref, cond_ref, wup_ref, wdn_ref, o_ref,
                h_ref, acc_ref):
    j = pl.program_id(2)

    @pl.when(j == 0)
    def _():
        _norm_modulate_to(h_ref, x_ref, g1_ref, cond_ref[3:4, :], cond_ref[4:5, :])
        acc_ref[...] = jnp.zeros_like(acc_ref)

    u = jnp.maximum(jnp.dot(h_ref[...], wup_ref[...], preferred_element_type=F32), 0.0)
    acc_ref[...] += jnp.dot((u * u).astype(BF16), wdn_ref[...], preferred_element_type=F32)

    @pl.when(j == pl.num_programs(2) - 1)
    def _():
        gate = cond_ref[5:6, :]
        g2 = g2_ref[...]

        def chunk(rows):
            o_ref[rows, :] = x_ref[rows, :] + gate * _rms(acc_ref[rows, :], g2)

        _for_row_chunks(acc_ref.shape[0], 32, chunk)


def _mlp(x, g1, g2, cond, w_up, w_dn, layer, tm=512, tf=1024):
    B, T, D = x.shape
    F = w_up.shape[2]
    return pl.pallas_call(
        _mlp_kernel,
        grid=(B, T // tm, F // tf),
        in_specs=[
            pl.BlockSpec((None, tm, D), lambda b, i, j: (b, i, 0)),
            pl.BlockSpec((1, D), lambda b, i, j: (0, 0)),
            pl.BlockSpec((1, D), lambda b, i, j: (0, 0)),
            pl.BlockSpec((None, 6, D), lambda b, i, j: (b, 0, 0)),
            pl.BlockSpec((None, D, tf), lambda b, i, j: (layer, 0, j)),
            pl.BlockSpec((None, tf, D), lambda b, i, j: (layer, j, 0)),
        ],
        out_specs=pl.BlockSpec((None, tm, D), lambda b, i, j: (b, i, 0)),
        out_shape=jax.ShapeDtypeStruct((B, T, D), F32),
        scratch_shapes=[pltpu.VMEM((tm, D), BF16), pltpu.VMEM((tm, D), F32)],
        compiler_params=_params(("parallel", "parallel", "arbitrary")),
        name="mlp",
    )(x, g1, g2, cond, w_up, w_dn)


def kernel(x, c, w_ada, b_ada, g_pre_mix, g_post_mix, w_in, hg_lb_logits, g_hg_norm,
           w_proj_sb, w_proj_hg, w_out, g_pre_mlp, g_post_mlp, w_mlp_up, w_mlp_down):
    B, T, D = x.shape
    L = w_ada.shape[0]

    lbs = jnp.cumsum(jax.nn.softmax(hg_lb_logits.astype(F32), axis=0), axis=0)
    lbs = lbs - lbs[0:1]
    lb_rows = jnp.stack([lbs, 1.0 - lbs], axis=1)

    c_pad = jnp.pad(c, ((0, 8 - B), (0, 0)))
    cond_all = _ada_cond(c_pad, w_ada, b_ada)[:, :B, :].reshape(L, B, 6, D)

    w_in, w_proj_sb, w_proj_hg, w_out, w_mlp_up, w_mlp_down = (
        w.astype(BF16) for w in (w_in, w_proj_sb, w_proj_hg, w_out, w_mlp_up, w_mlp_down))

    for l in range(L):
        cond = cond_all[l]
        p = _in_proj(x, g_pre_mix[l].reshape(1, D), cond, w_in, l)
        y_sb = _sb_attention(p)
        y_hg = _hgrn(p, lb_rows[l], g_hg_norm[l].reshape(1, -1))
        x = _mix_out(y_sb, y_hg, p, w_proj_sb, w_proj_hg, w_out, l,
                     x, g_post_mix[l].reshape(1, D), cond)
        x = _mlp(x, g_pre_mlp[l].reshape(1, D), g_post_mlp[l].reshape(1, D), cond,
                 w_mlp_up, w_mlp_down, l)
    return x
```

```python
import functools
import math

import jax
import jax.numpy as jnp
from jax import lax
from jax.experimental import pallas as pl
from jax.experimental.pallas import tpu as pltpu

F32 = jnp.float32
BF16 = jnp.bfloat16

EPS = 1e-6
HEAD_DIM = 128
SB_HEADS = 8
HG_HEADS = 8
HG_CHUNK = 32

SB_Q_BLK, SB_K_BLK, SB_V_BLK = 0, 8, 16
HG_Q_BLK, HG_F_BLK, HG_I_BLK, HG_G_BLK = 24, 32, 40, 48
GATE_A_COL, GATE_B_COL = 7168, 9216

VMEM_LIMIT = 56 * 1024 * 1024

SKIP_BELOW = -104.0
LOG2E = 1.4426950408889634

NT_DIMS = (((1,), (1,)), ((), ()))
TN_DIMS = (((0,), (0,)), ((), ()))


def _params(sem):
    return pltpu.CompilerParams(dimension_semantics=sem, vmem_limit_bytes=VMEM_LIMIT)


def _rms(xf, g):
    ms = jnp.mean(xf * xf, axis=-1, keepdims=True)
    return xf * lax.rsqrt(ms + EPS) * g


def _silu(x):
    hx = 0.5 * x
    return hx + hx * jnp.tanh(hx)


def _for_row_chunks(n_rows, chunk_rows, fn):
    def body(r, carry):
        fn(pl.ds(pl.multiple_of(r * chunk_rows, chunk_rows), chunk_rows))
        return carry
    lax.fori_loop(0, n_rows // chunk_rows, body, 0, unroll=4)


def _norm_modulate_to(h_ref, x_ref, g_ref, shift, scale):
    gmod = g_ref[...] * (1.0 + scale)

    def chunk(rows):
        x = x_ref[rows, :]
        ms = jnp.mean(x * x, axis=-1, keepdims=True)
        h_ref[rows, :] = (x * lax.rsqrt(ms + EPS) * gmod + shift).astype(h_ref.dtype)

    _for_row_chunks(x_ref.shape[0], 32, chunk)


def _split_bf16(a):
    hi = a.astype(BF16)
    lo = (a - hi.astype(F32)).astype(BF16)
    return hi, lo


def _ada_kernel(c_ref, w_ref, b_ref, o_ref):
    c = c_ref[...]
    ca = (c * jax.nn.sigmoid(c)).astype(BF16)
    o_ref[...] = jnp.dot(ca, w_ref[...].astype(BF16),
                         preferred_element_type=F32) + b_ref[...]


def _ada_cond(c_pad, w_ada, b_ada, tn=1024):
    L, D, N = w_ada.shape
    M = c_pad.shape[0]
    return pl.pallas_call(
        _ada_kernel,
        grid=(L, N // tn),
        in_specs=[
            pl.BlockSpec((M, D), lambda l, j: (0, 0)),
            pl.BlockSpec((None, D, tn), lambda l, j: (l, 0, j)),
            pl.BlockSpec((None, 1, tn), lambda l, j: (l, 0, j)),
        ],
        out_specs=pl.BlockSpec((None, M, tn), lambda l, j: (l, 0, j)),
        out_shape=jax.ShapeDtypeStruct((L, M, N), F32),
        compiler_params=_params(("parallel", "parallel")),
        name="ada_cond",
    )(c_pad, w_ada, b_ada.reshape(L, 1, N))


def _in_proj_kernel(x_ref, g_ref, cond_ref, w_ref, o_ref, h_ref):
    @pl.when(pl.program_id(2) == 0)
    def _():
        _norm_modulate_to(h_ref, x_ref, g_ref, cond_ref[0:1, :], cond_ref[1:2, :])

    o_ref[...] = jnp.dot(h_ref[...], w_ref[...],
                         preferred_element_type=F32).astype(o_ref.dtype)


def _in_proj(x, g, cond, w, layer, tm=512, tn=2816):
    B, T, D = x.shape
    N = w.shape[2]
    return pl.pallas_call(
        _in_proj_kernel,
        grid=(B, T // tm, N // tn),
        in_specs=[
            pl.BlockSpec((None, tm, D), lambda b, i, j: (b, i, 0)),
            pl.BlockSpec((1, D), lambda b, i, j: (0, 0)),
            pl.BlockSpec((None, 6, D), lambda b, i, j: (b, 0, 0)),
            pl.BlockSpec((None, D, tn), lambda b, i, j: (layer, 0, j)),
        ],
        out_specs=pl.BlockSpec((None, tm, tn), lambda b, i, j: (b, i, j)),
        out_shape=jax.ShapeDtypeStruct((B, T, N), BF16),
        scratch_shapes=[pltpu.VMEM((tm, D), BF16)],
        compiler_params=_params(("parallel", "parallel", "arbitrary")),
        name="in_proj",
    )(x, g, cond, w)


def _sb_kernel(q_ref, k_ref, v_ref, u_ref, o_ref, *, tile, heads, scale):
    i = pl.program_id(2)
    u = u_ref[...]
    row = lax.broadcasted_iota(jnp.int32, (tile, tile), 0)
    col = lax.broadcasted_iota(jnp.int32, (tile, tile), 1)
    causal = col < row

    def tile_terms(h, j, diagonal):
        lanes = slice(h * HEAD_DIM, (h + 1) * HEAD_DIM)
        keys = pl.ds(pl.multiple_of(j * tile, tile), tile)
        s = lax.dot_general(q_ref[:, lanes], k_ref[keys, lanes], NT_DIMS,
                            preferred_element_type=F32)
        e = jnp.exp2(jnp.abs(s) * (-scale * LOG2E))
        ln = jnp.maximum(s, 0.0) * (-scale) - jnp.log(1.0 + e)
        if diagonal:
            ln = jnp.where(causal, ln, 0.0)
        hi, lo = _split_bf16(ln)
        incl = jnp.dot(jnp.concatenate([hi, lo], axis=1), u, preferred_element_type=F32)
        a = jnp.exp(s * scale + incl)
        if diagonal:
            a = jnp.where(causal, a, 0.0)
        pv = jnp.dot(a.astype(BF16), v_ref[keys, lanes], preferred_element_type=F32)
        return pv, incl[:, 0:1]

    has_prev = i > 0
    jp = jnp.maximum(i - 1, 0)
    carries, accs = [], []
    for h in range(heads):
        pv_d, tot_d = tile_terms(h, i, True)
        pv_p, tot_p = tile_terms(h, jp, False)
        accs.append(pv_d + jnp.where(has_prev, jnp.exp(tot_d), 0.0) * pv_p)
        carries.append(tot_d + jnp.where(has_prev, tot_p, 0.0))

    tops = [jnp.max(c) for c in carries]
    for h in range(heads):
        def live(state):
            j, top, _, _ = state
            return (j >= 0) & (top >= SKIP_BELOW)

        def step(state, h=h):
            j, _, carry, acc = state
            pv, tot = tile_terms(h, j, False)
            carry_new = carry + tot
            return j - 1, jnp.max(carry_new), carry_new, acc + jnp.exp(carry) * pv

        _, _, _, acc = lax.while_loop(live, step, (i - 2, tops[h], carries[h], accs[h]))
        o_ref[:, h * HEAD_DIM:(h + 1) * HEAD_DIM] = acc.astype(o_ref.dtype)


def _sb_attention(p, tile=256, heads=8):
    B, T, _ = p.shape
    idx = jnp.arange(tile)
    u1 = (idx[:, None] >= idx[None, :]).astype(BF16)
    u = jnp.concatenate([u1, u1], axis=0)
    kern = functools.partial(_sb_kernel, tile=tile, heads=heads,
                             scale=1.0 / math.sqrt(HEAD_DIM))
    w = heads * HEAD_DIM
    groups = SB_HEADS // heads
    return pl.pallas_call(
        kern,
        grid=(B, groups, T // tile),
        in_specs=[
            pl.BlockSpec((None, tile, w), lambda b, g, i: (b, i, SB_Q_BLK // heads + g)),
            pl.BlockSpec((None, T, w), lambda b, g, i: (b, 0, SB_K_BLK // heads + g)),
            pl.BlockSpec((None, T, w), lambda b, g, i: (b, 0, SB_V_BLK // heads + g)),
            pl.BlockSpec((2 * tile, tile), lambda b, g, i: (0, 0)),
        ],
        out_specs=pl.BlockSpec((None, tile, w), lambda b, g, i: (b, i, g)),
        out_shape=jax.ShapeDtypeStruct((B, T, SB_HEADS * HEAD_DIM), BF16),
        compiler_params=_params(("parallel", "parallel", "arbitrary")),
        name="sb_attention",
    )(p, p, p, u)


def _hgrn_kernel(hq_ref, hf_ref, hi_ref, hg_ref, lb_ref, gn_ref, tri_ref, o_ref,
                 st_ref, *, tb, chunk, heads):
    @pl.when(pl.program_id(2) == 0)
    def _():
        st_ref[...] = jnp.zeros_like(st_ref)

    hf = hf_ref[...].astype(F32)
    lb = lb_ref[0:1, :]
    one_m_lb = lb_ref[1:2, :]

    e = jnp.exp(-jnp.abs(hf))
    r = 1.0 / (1.0 + e)
    er = e * r
    pos = hf >= 0.0
    logf = jnp.log(lb + one_m_lb * jnp.where(pos, r, er))
    kk_all = one_m_lb * jnp.where(pos, er, r)
    qq = _silu(hq_ref[...].astype(F32))

    hi, lo = _split_bf16(logf)
    g_all = jnp.dot(tri_ref[...], jnp.concatenate([hi, lo], axis=0),
                    preferred_element_type=F32)
    qg_all = (qq * jnp.exp(g_all)).astype(BF16)
    kg_all = (kk_all * jnp.exp(-g_all)).astype(BF16)

    row = lax.broadcasted_iota(jnp.int32, (tb, tb), 0)
    col = lax.broadcasted_iota(jnp.int32, (tb, tb), 1)
    keep = (col <= row) & ((row // chunk) == (col // chunk))
    chunks = [slice(n * chunk, (n + 1) * chunk) for n in range(tb // chunk)]

    for h in range(heads):
        lanes = slice(h * HEAD_DIM, (h + 1) * HEAD_DIM)
        g, kk, qg = g_all[:, lanes], kk_all[:, lanes], qg_all[:, lanes]
        v = hi_ref[:, lanes]

        scores = lax.dot_general(qg, kg_all[:, lanes], NT_DIMS, preferred_element_type=F32)
        scores = jnp.where(keep, scores, 0.0).astype(BF16)
        o = jnp.dot(scores, v, preferred_element_type=F32)

        upds, decays = [], []
        for sl in chunks:
            g_end = g[sl.stop - 1:sl.stop, :]
            kdec = (kk[sl] * jnp.exp(g_end - g[sl])).astype(BF16)
            upds.append(lax.dot_general(v[sl], kdec, TN_DIMS, preferred_element_type=F32))
            decays.append(jnp.exp(g_end))

        st = st_ref[h]
        inter = []
        for sl, upd, dec in zip(chunks, upds, decays):
            inter.append(lax.dot_general(qg[sl], st.astype(BF16), NT_DIMS,
                                         preferred_element_type=F32))
            st = dec * st + upd
        st_ref[h] = st

        o = o + jnp.concatenate(inter, axis=0)
        y = _rms(o, gn_ref[:, lanes]) * _silu(hg_ref[:, lanes].astype(F32))
        o_ref[:, lanes] = y.astype(o_ref.dtype)


def _hgrn(p, lb_rows, gn, tb=256, heads=8):
    B, T, _ = p.shape
    c = HG_CHUNK
    idx = jnp.arange(tb)
    same = (idx[:, None] // c) == (idx[None, :] // c)
    tri1 = (same & (idx[None, :] <= idx[:, None])).astype(BF16)
    tri = jnp.concatenate([tri1, tri1], axis=1)
    kern = functools.partial(_hgrn_kernel, tb=tb, chunk=c, heads=heads)
    w = heads * HEAD_DIM
    groups = HG_HEADS // heads

    def col(blk):
        return pl.BlockSpec((None, tb, w), lambda b, g, i: (b, i, blk // heads + g))

    return pl.pallas_call(
        kern,
        grid=(B, groups, T // tb),
        in_specs=[
            col(HG_Q_BLK), col(HG_F_BLK), col(HG_I_BLK), col(HG_G_BLK),
            pl.BlockSpec((2, w), lambda b, g, i: (0, g)),
            pl.BlockSpec((1, w), lambda b, g, i: (0, g)),
            pl.BlockSpec((tb, 2 * tb), lambda b, g, i: (0, 0)),
        ],
        out_specs=pl.BlockSpec((None, tb, w), lambda b, g, i: (b, i, g)),
        out_shape=jax.ShapeDtypeStruct((B, T, HG_HEADS * HEAD_DIM), BF16),
        scratch_shapes=[pltpu.VMEM((heads, HEAD_DIM, HEAD_DIM), F32)],
        compiler_params=_params(("parallel", "parallel", "arbitrary")),
        name="hgrn2",
    )(p, p, p, p, lb_rows, gn, tri)


def _mix_out_kernel(ysb_ref, yhg_ref, ga0_ref, ga1_ref, gb0_ref, gb1_ref,
                    wsb_ref, whg_ref, wout_ref, x_ref, g_ref, cond_ref, o_ref):
    ysb = ysb_ref[...]
    yhg = yhg_ref[...]
    half = ga0_ref.shape[-1]
    mo = None
    for c, (ga_ref, gb_ref) in enumerate(((ga0_ref, gb0_ref), (ga1_ref, gb1_ref))):
        cols = slice(c * half, (c + 1) * half)
        a = jnp.dot(ysb, wsb_ref[:, cols], preferred_element_type=F32)
        b = jnp.dot(yhg, whg_ref[:, cols], preferred_element_type=F32)
        m = (jax.nn.sigmoid(ga_ref[...].astype(F32)) * a
             + jax.nn.sigmoid(gb_ref[...].astype(F32)) * b)
        part = jnp.dot(m.astype(BF16), wout_ref[cols, :], preferred_element_type=F32)
        mo = part if mo is None else mo + part
    o_ref[...] = x_ref[...] + cond_ref[2:3, :] * _rms(mo, g_ref[...])


def _mix_out(y_sb, y_hg, p, w_sb, w_hg, w_out, layer, x, g, cond, tm=512):
    B, T, D = x.shape
    W = y_sb.shape[-1]
    half = D // 2
    ga_blk, gb_blk = GATE_A_COL // half, GATE_B_COL // half

    def gate(blk):
        return pl.BlockSpec((None, tm, half), lambda b, i: (b, i, blk))

    def resident(shape):
        return pl.BlockSpec((None,) + shape, lambda b, i: (layer, 0, 0),
                            pipeline_mode=pl.Buffered(1))

    return pl.pallas_call(
        _mix_out_kernel,
        grid=(B, T // tm),
        in_specs=[
            pl.BlockSpec((None, tm, W), lambda b, i: (b, i, 0)),
            pl.BlockSpec((None, tm, W), lambda b, i: (b, i, 0)),
            gate(ga_blk), gate(ga_blk + 1), gate(gb_blk), gate(gb_blk + 1),
            resident((W, D)), resident((W, D)), resident((D, D)),
            pl.BlockSpec((None, tm, D), lambda b, i: (b, i, 0)),
            pl.BlockSpec((1, D), lambda b, i: (0, 0)),
            pl.BlockSpec((None, 6, D), lambda b, i: (b, 0, 0)),
        ],
        out_specs=pl.BlockSpec((None, tm, D), lambda b, i: (b, i, 0)),
        out_shape=jax.ShapeDtypeStruct((B, T, D), F32),
        compiler_params=_params(("parallel", "parallel")),
        name="mix_out",
    )(y_sb, y_hg, p, p, p, p, w_sb, w_hg, w_out, x, g, cond)


def _mlp_kernel(x_ref, g1_ref, g2_ref, cond_ref, wup_ref, wdn_ref, o_ref, h_ref, *, sub):
    j = pl.program_id(2)

    @pl.when(j == 0)
    def _():
        _norm_modulate_to(h_ref, x_ref, g1_ref, cond_ref[3:4, :], cond_ref[4:5, :])
        o_ref[...] = jnp.zeros_like(o_ref)

    for c in range(wup_ref.shape[1] // sub):
        cols = slice(c * sub, (c + 1) * sub)
        u = jnp.maximum(jnp.dot(h_ref[...], wup_ref[:, cols], preferred_element_type=F32), 0.0)
        o_ref[...] += jnp.dot((u * u).astype(BF16), wdn_ref[cols, :],
                              preferred_element_type=F32)

    @pl.when(j == pl.num_programs(2) - 1)
    def _():
        gate = cond_ref[5:6, :]
        g2 = g2_ref[...]

        for r in range(0, o_ref.shape[0], 32):
            rows = slice(r, r + 32)
            o_ref[rows, :] = x_ref[rows, :] + gate * _rms(o_ref[rows, :], g2)


def _mlp(x, g1, g2, cond, w_up, w_dn, layer, tm=512, tf=2048, sub=1024):
    B, T, D = x.shape
    F = w_up.shape[2]
    return pl.pallas_call(
        functools.partial(_mlp_kernel, sub=sub),
        grid=(B, T // tm, F // tf),
        in_specs=[
            pl.BlockSpec((None, tm, D), lambda b, i, j: (b, i, 0)),
            pl.BlockSpec((1, D), lambda b, i, j: (0, 0)),
            pl.BlockSpec((1, D), lambda b, i, j: (0, 0)),
            pl.BlockSpec((None, 6, D), lambda b, i, j: (b, 0, 0)),
            pl.BlockSpec((None, D, tf), lambda b, i, j: (layer, 0, j)),
            pl.BlockSpec((None, tf, D), lambda b, i, j: (layer, j, 0)),
        ],
        out_specs=pl.BlockSpec((None, tm, D), lambda b, i, j: (b, i, 0)),
        out_shape=jax.ShapeDtypeStruct((B, T, D), F32),
        scratch_shapes=[pltpu.VMEM((tm, D), BF16)],
        compiler_params=_params(("parallel", "parallel", "arbitrary")),
        name="mlp",
    )(x, g1, g2, cond, w_up, w_dn)


def kernel(x, c, w_ada, b_ada, g_pre_mix, g_post_mix, w_in, hg_lb_logits, g_hg_norm,
           w_proj_sb, w_proj_hg, w_out, g_pre_mlp, g_post_mlp, w_mlp_up, w_mlp_down):
    B, T, D = x.shape
    L = w_ada.shape[0]

    lbs = jnp.cumsum(jax.nn.softmax(hg_lb_logits.astype(F32), axis=0), axis=0)
    lbs = lbs - lbs[0:1]
    lb_rows = jnp.stack([lbs, 1.0 - lbs], axis=1)

    c_pad = jnp.pad(c, ((0, 8 - B), (0, 0)))
    cond_all = _ada_cond(c_pad, w_ada, b_ada)[:, :B, :].reshape(L, B, 6, D)

    w_in, w_proj_sb, w_proj_hg, w_out, w_mlp_up, w_mlp_down = (
        w.astype(BF16) for w in (w_in, w_proj_sb, w_proj_hg, w_out, w_mlp_up, w_mlp_down))

    for l in range(L):
        cond = cond_all[l]
        p = _in_proj(x, g_pre_mix[l].reshape(1, D), cond, w_in, l)
        y_sb = _sb_attention(p)
        y_hg = _hgrn(p, lb_rows[l], g_hg_norm[l].reshape(1, -1))
        x = _mix_out(y_sb, y_hg, p, w_proj_sb, w_proj_hg, w_out, l,
                     x, g_post_mix[l].reshape(1, D), cond)
        x = _mlp(x, g_pre_mlp[l].reshape(1, D), g_post_mlp[l].reshape(1, D), cond,
                 w_mlp_up, w_mlp_down, l)
    return x
```

```python
import functools
import math

import jax
import jax.numpy as jnp
from jax import lax
from jax.experimental import pallas as pl
from jax.experimental.pallas import tpu as pltpu

F32 = jnp.float32
BF16 = jnp.bfloat16

EPS = 1e-6
HEAD_DIM = 128
SB_HEADS = 8
HG_HEADS = 8
HG_CHUNK = 32

SB_Q_BLK, SB_K_BLK, SB_V_BLK = 0, 8, 16
HG_Q_BLK, HG_F_BLK, HG_I_BLK, HG_G_BLK = 24, 32, 40, 48
GATE_A_COL, GATE_B_COL = 7168, 9216

VMEM_LIMIT = 56 * 1024 * 1024

SKIP_BELOW = -104.0
LOG2E = 1.4426950408889634

NT_DIMS = (((1,), (1,)), ((), ()))
TN_DIMS = (((0,), (0,)), ((), ()))


def _params(sem):
    return pltpu.CompilerParams(dimension_semantics=sem, vmem_limit_bytes=VMEM_LIMIT)


def _rms(xf, g):
    ms = jnp.mean(xf * xf, axis=-1, keepdims=True)
    return xf * lax.rsqrt(ms + EPS) * g


def _silu(x):
    hx = 0.5 * x
    return hx + hx * jnp.tanh(hx)


def _for_row_chunks(n_rows, chunk_rows, fn):
    def body(r, carry):
        fn(pl.ds(pl.multiple_of(r * chunk_rows, chunk_rows), chunk_rows))
        return carry
    lax.fori_loop(0, n_rows // chunk_rows, body, 0, unroll=4)


def _norm_modulate_to(h_ref, x_ref, g_ref, shift, scale):
    gmod = g_ref[...] * (1.0 + scale)

    def chunk(rows):
        x = x_ref[rows, :]
        ms = jnp.mean(x * x, axis=-1, keepdims=True)
        h_ref[rows, :] = (x * lax.rsqrt(ms + EPS) * gmod + shift).astype(h_ref.dtype)

    _for_row_chunks(x_ref.shape[0], 32, chunk)


def _split_bf16(a):
    hi = a.astype(BF16)
    lo = (a - hi.astype(F32)).astype(BF16)
    return hi, lo


def _ada_kernel(c_ref, w_ref, b_ref, o_ref):
    c = c_ref[...]
    ca = (c * jax.nn.sigmoid(c)).astype(BF16)
    o_ref[...] = jnp.dot(ca, w_ref[...].astype(BF16),
                         preferred_element_type=F32) + b_ref[...]


def _ada_cond(c_pad, w_ada, b_ada, tn=1024):
    L, D, N = w_ada.shape
    M = c_pad.shape[0]
    return pl.pallas_call(
        _ada_kernel,
        grid=(L, N // tn),
        in_specs=[
            pl.BlockSpec((M, D), lambda l, j: (0, 0)),
            pl.BlockSpec((None, D, tn), lambda l, j: (l, 0, j)),
            pl.BlockSpec((None, 1, tn), lambda l, j: (l, 0, j)),
        ],
        out_specs=pl.BlockSpec((None, M, tn), lambda l, j: (l, 0, j)),
        out_shape=jax.ShapeDtypeStruct((L, M, N), F32),
        compiler_params=_params(("parallel", "parallel")),
        name="ada_cond",
    )(c_pad, w_ada, b_ada.reshape(L, 1, N))


def _in_proj_kernel(x_ref, g_ref, cond_ref, w_ref, o_ref, h_ref):
    @pl.when(pl.program_id(2) == 0)
    def _():
        _norm_modulate_to(h_ref, x_ref, g_ref, cond_ref[0:1, :], cond_ref[1:2, :])

    o_ref[...] = jnp.dot(h_ref[...], w_ref[...],
                         preferred_element_type=F32).astype(o_ref.dtype)


def _in_proj(x, g, cond, w, layer, tm=512, tn=2816):
    B, T, D = x.shape
    N = w.shape[2]
    return pl.pallas_call(
        _in_proj_kernel,
        grid=(B, T // tm, N // tn),
        in_specs=[
            pl.BlockSpec((None, tm, D), lambda b, i, j: (b, i, 0)),
            pl.BlockSpec((1, D), lambda b, i, j: (0, 0)),
            pl.BlockSpec((None, 6, D), lambda b, i, j: (b, 0, 0)),
            pl.BlockSpec((None, D, tn), lambda b, i, j: (layer, 0, j)),
        ],
        out_specs=pl.BlockSpec((None, tm, tn), lambda b, i, j: (b, i, j)),
        out_shape=jax.ShapeDtypeStruct((B, T, N), BF16),
        scratch_shapes=[pltpu.VMEM((tm, D), BF16)],
        compiler_params=_params(("parallel", "parallel", "arbitrary")),
        name="in_proj",
    )(x, g, cond, w)


def _sb_kernel(q_ref, k_ref, v_ref, u_ref, o_ref, *, tile, heads, scale):
    tiles_per_step = q_ref.shape[0] // tile

    def body(sub, carry):
        _sb_query_tile(q_ref, k_ref, v_ref, u_ref, o_ref,
                       pl.program_id(2) * tiles_per_step + sub,
                       pl.ds(pl.multiple_of(sub * tile, tile), tile),
                       tile=tile, heads=heads, scale=scale)
        return carry

    lax.fori_loop(0, tiles_per_step, body, 0)


def _sb_query_tile(q_ref, k_ref, v_ref, u_ref, o_ref, i, q_rows, *, tile, heads, scale):
    u = u_ref[...]
    row = lax.broadcasted_iota(jnp.int32, (tile, tile), 0)
    col = lax.broadcasted_iota(jnp.int32, (tile, tile), 1)
    causal = col < row

    def tile_terms(h, j, diagonal):
        lanes = slice(h * HEAD_DIM, (h + 1) * HEAD_DIM)
        keys = pl.ds(pl.multiple_of(j * tile, tile), tile)
        s = lax.dot_general(q_ref[q_rows, lanes], k_ref[keys, lanes], NT_DIMS,
                            preferred_element_type=F32)
        e = jnp.exp2(jnp.abs(s) * (-scale * LOG2E))
        ln = jnp.maximum(s, 0.0) * (-scale) - jnp.log(1.0 + e)
        if diagonal:
            ln = jnp.where(causal, ln, 0.0)
        hi, lo = _split_bf16(ln)
        incl = jnp.dot(jnp.concatenate([hi, lo], axis=1), u, preferred_element_type=F32)
        a = jnp.exp(s * scale + incl)
        if diagonal:
            a = jnp.where(causal, a, 0.0)
        pv = jnp.dot(a.astype(BF16), v_ref[keys, lanes], preferred_element_type=F32)
        return pv, incl[:, 0:1]

    has_prev = i > 0
    jp = jnp.maximum(i - 1, 0)
    carries, accs = [], []
    for h in range(heads):
        pv_d, tot_d = tile_terms(h, i, True)
        pv_p, tot_p = tile_terms(h, jp, False)
        accs.append(pv_d + jnp.where(has_prev, jnp.exp(tot_d), 0.0) * pv_p)
        carries.append(tot_d + jnp.where(has_prev, tot_p, 0.0))

    tops = [jnp.max(c) for c in carries]
    for h in range(heads):
        def live(state):
            j, top, _, _ = state
            return (j >= 0) & (top >= SKIP_BELOW)

        def step(state, h=h):
            j, _, carry, acc = state
            pv, tot = tile_terms(h, j, False)
            carry_new = carry + tot
            return j - 1, jnp.max(carry_new), carry_new, acc + jnp.exp(carry) * pv

        _, _, _, acc = lax.while_loop(live, step, (i - 2, tops[h], carries[h], accs[h]))
        o_ref[q_rows, h * HEAD_DIM:(h + 1) * HEAD_DIM] = acc.astype(o_ref.dtype)


def _sb_attention(p, tile=256, heads=8, tiles_per_step=4):
    B, T, _ = p.shape
    idx = jnp.arange(tile)
    u1 = (idx[:, None] >= idx[None, :]).astype(BF16)
    u = jnp.concatenate([u1, u1], axis=0)
    kern = functools.partial(_sb_kernel, tile=tile, heads=heads,
                             scale=1.0 / math.sqrt(HEAD_DIM))
    w = heads * HEAD_DIM
    groups = SB_HEADS // heads
    tq = tile * tiles_per_step
    return pl.pallas_call(
        kern,
        grid=(B, groups, T // tq),
        in_specs=[
            pl.BlockSpec((None, tq, w), lambda b, g, i: (b, i, SB_Q_BLK // heads + g)),
            pl.BlockSpec((None, T, w), lambda b, g, i: (b, 0, SB_K_BLK // heads + g)),
            pl.BlockSpec((None, T, w), lambda b, g, i: (b, 0, SB_V_BLK // heads + g)),
            pl.BlockSpec((2 * tile, tile), lambda b, g, i: (0, 0)),
        ],
        out_specs=pl.BlockSpec((None, tq, w), lambda b, g, i: (b, i, g)),
        out_shape=jax.ShapeDtypeStruct((B, T, SB_HEADS * HEAD_DIM), BF16),
        compiler_params=_params(("parallel", "parallel", "arbitrary")),
        name="sb_attention",
    )(p, p, p, u)


def _hgrn_kernel(hq_ref, hf_ref, hi_ref, hg_ref, lb_ref, gn_ref, tri_ref, o_ref,
                 st_ref, *, tb, chunk, heads):
    @pl.when(pl.program_id(2) == 0)
    def _():
        st_ref[...] = jnp.zeros_like(st_ref)

    def body(sub, carry):
        rows = pl.ds(pl.multiple_of(sub * tb, tb), tb)
        _hgrn_block(hq_ref, hf_ref, hi_ref, hg_ref, lb_ref, gn_ref, tri_ref, o_ref, st_ref,
                    rows, tb=tb, chunk=chunk, heads=heads)
        return carry

    lax.fori_loop(0, o_ref.shape[0] // tb, body, 0)


def _hgrn_block(hq_ref, hf_ref, hi_ref, hg_ref, lb_ref, gn_ref, tri_ref, o_ref, st_ref,
                rows, *, tb, chunk, heads):
    hf = hf_ref[rows, :].astype(F32)
    lb = lb_ref[0:1, :]
    one_m_lb = lb_ref[1:2, :]

    e = jnp.exp(-jnp.abs(hf))
    r = 1.0 / (1.0 + e)
    er = e * r
    pos = hf >= 0.0
    logf = jnp.log(lb + one_m_lb * jnp.where(pos, r, er))
    kk_all = one_m_lb * jnp.where(pos, er, r)
    qq = _silu(hq_ref[rows, :].astype(F32))

    hi, lo = _split_bf16(logf)
    g_all = jnp.dot(tri_ref[...], jnp.concatenate([hi, lo], axis=0),
                    preferred_element_type=F32)
    qg_all = (qq * jnp.exp(g_all)).astype(BF16)
    kg_all = (kk_all * jnp.exp(-g_all)).astype(BF16)

    row = lax.broadcasted_iota(jnp.int32, (tb, tb), 0)
    col = lax.broadcasted_iota(jnp.int32, (tb, tb), 1)
    keep = (col <= row) & ((row // chunk) == (col // chunk))
    chunks = [slice(n * chunk, (n + 1) * chunk) for n in range(tb // chunk)]

    for h in range(heads):
        lanes = slice(h * HEAD_DIM, (h + 1) * HEAD_DIM)
        g, kk, qg = g_all[:, lanes], kk_all[:, lanes], qg_all[:, lanes]
        v = hi_ref[rows, lanes]

        scores = lax.dot_general(qg, kg_all[:, lanes], NT_DIMS, preferred_element_type=F32)
        scores = jnp.where(keep, scores, 0.0).astype(BF16)
        o = jnp.dot(scores, v, preferred_element_type=F32)

        upds, decays = [], []
        for sl in chunks:
            g_end = g[sl.stop - 1:sl.stop, :]
            kdec = (kk[sl] * jnp.exp(g_end - g[sl])).astype(BF16)
            upds.append(lax.dot_general(v[sl], kdec, TN_DIMS, preferred_element_type=F32))
            decays.append(jnp.exp(g_end))

        st = st_ref[h]
        inter = []
        for sl, upd, dec in zip(chunks, upds, decays):
            inter.append(lax.dot_general(qg[sl], st.astype(BF16), NT_DIMS,
                                         preferred_element_type=F32))
            st = dec * st + upd
        st_ref[h] = st

        o = o + jnp.concatenate(inter, axis=0)
        y = _rms(o, gn_ref[:, lanes]) * _silu(hg_ref[rows, lanes].astype(F32))
        o_ref[rows, lanes] = y.astype(o_ref.dtype)


def _hgrn(p, lb_rows, gn, tb=256, heads=8, blocks_per_step=4):
    B, T, _ = p.shape
    c = HG_CHUNK
    idx = jnp.arange(tb)
    same = (idx[:, None] // c) == (idx[None, :] // c)
    tri1 = (same & (idx[None, :] <= idx[:, None])).astype(BF16)
    tri = jnp.concatenate([tri1, tri1], axis=1)
    kern = functools.partial(_hgrn_kernel, tb=tb, chunk=c, heads=heads)
    w = heads * HEAD_DIM
    groups = HG_HEADS // heads
    ts = tb * blocks_per_step

    def col(blk):
        return pl.BlockSpec((None, ts, w), lambda b, g, i: (b, i, blk // heads + g))

    return pl.pallas_call(
        kern,
        grid=(B, groups, T // ts),
        in_specs=[
            col(HG_Q_BLK), col(HG_F_BLK), col(HG_I_BLK), col(HG_G_BLK),
            pl.BlockSpec((2, w), lambda b, g, i: (0, g)),
            pl.BlockSpec((1, w), lambda b, g, i: (0, g)),
            pl.BlockSpec((tb, 2 * tb), lambda b, g, i: (0, 0)),
        ],
        out_specs=pl.BlockSpec((None, ts, w), lambda b, g, i: (b, i, g)),
        out_shape=jax.ShapeDtypeStruct((B, T, HG_HEADS * HEAD_DIM), BF16),
        scratch_shapes=[pltpu.VMEM((heads, HEAD_DIM, HEAD_DIM), F32)],
        compiler_params=_params(("parallel", "parallel", "arbitrary")),
        name="hgrn2",
    )(p, p, p, p, lb_rows, gn, tri)


def _mix_out_kernel(ysb_ref, yhg_ref, ga0_ref, ga1_ref, gb0_ref, gb1_ref,
                    wsb_ref, whg_ref, wout_ref, x_ref, g_ref, cond_ref, o_ref):
    ysb = ysb_ref[...]
    yhg = yhg_ref[...]
    half = ga0_ref.shape[-1]
    mo = None
    for c, (ga_ref, gb_ref) in enumerate(((ga0_ref, gb0_ref), (ga1_ref, gb1_ref))):
        cols = slice(c * half, (c + 1) * half)
        a = jnp.dot(ysb, wsb_ref[:, cols], preferred_element_type=F32)
        b = jnp.dot(yhg, whg_ref[:, cols], preferred_element_type=F32)
        m = (jax.nn.sigmoid(ga_ref[...].astype(F32)) * a
             + jax.nn.sigmoid(gb_ref[...].astype(F32)) * b)
        part = jnp.dot(m.astype(BF16), wout_ref[cols, :], preferred_element_type=F32)
        mo = part if mo is None else mo + part
    o_ref[...] = x_ref[...] + cond_ref[2:3, :] * _rms(mo, g_ref[...])


def _mix_out(y_sb, y_hg, p, w_sb, w_hg, w_out, layer, x, g, cond, tm=512):
    B, T, D = x.shape
    W = y_sb.shape[-1]
    half = D // 2
    ga_blk, gb_blk = GATE_A_COL // half, GATE_B_COL // half

    def gate(blk):
        return pl.BlockSpec((None, tm, half), lambda b, i: (b, i, blk))

    def resident(shape):
        return pl.BlockSpec((None,) + shape, lambda b, i: (layer, 0, 0),
                            pipeline_mode=pl.Buffered(1))

    return pl.pallas_call(
        _mix_out_kernel,
        grid=(B, T // tm),
        in_specs=[
            pl.BlockSpec((None, tm, W), lambda b, i: (b, i, 0)),
            pl.BlockSpec((None, tm, W), lambda b, i: (b, i, 0)),
            gate(ga_blk), gate(ga_blk + 1), gate(gb_blk), gate(gb_blk + 1),
            resident((W, D)), resident((W, D)), resident((D, D)),
            pl.BlockSpec((None, tm, D), lambda b, i: (b, i, 0)),
            pl.BlockSpec((1, D), lambda b, i: (0, 0)),
            pl.BlockSpec((None, 6, D), lambda b, i: (b, 0, 0)),
        ],
        out_specs=pl.BlockSpec((None, tm, D), lambda b, i: (b, i, 0)),
        out_shape=jax.ShapeDtypeStruct((B, T, D), F32),
        compiler_params=_params(("parallel", "parallel")),
        name="mix_out",
    )(y_sb, y_hg, p, p, p, p, w_sb, w_hg, w_out, x, g, cond)


def _mlp_kernel(x_ref, g1_ref, g2_ref, cond_ref, wup_ref, wdn_ref, o_ref, h_ref, *, sub):
    j = pl.program_id(2)

    @pl.when(j == 0)
    def _():
        _norm_modulate_to(h_ref, x_ref, g1_ref, cond_ref[3:4, :], cond_ref[4:5, :])
        o_ref[...] = jnp.zeros_like(o_ref)

    for c in range(wup_ref.shape[1] // sub):
        cols = slice(c * sub, (c + 1) * sub)
        u = jnp.maximum(jnp.dot(h_ref[...], wup_ref[:, cols], preferred_element_type=F32), 0.0)
        o_ref[...] += jnp.dot((u * u).astype(BF16), wdn_ref[cols, :],
                              preferred_element_type=F32)

    @pl.when(j == pl.num_programs(2) - 1)
    def _():
        gate = cond_ref[5:6, :]
        g2 = g2_ref[...]

        for r in range(0, o_ref.shape[0], 32):
            rows = slice(r, r + 32)
            o_ref[rows, :] = x_ref[rows, :] + gate * _rms(o_ref[rows, :], g2)


def _mlp(x, g1, g2, cond, w_up, w_dn, layer, tm=512, tf=2048, sub=1024):
    B, T, D = x.shape
    F = w_up.shape[2]
    return pl.pallas_call(
        functools.partial(_mlp_kernel, sub=sub),
        grid=(B, T // tm, F // tf),
        in_specs=[
            pl.BlockSpec((None, tm, D), lambda b, i, j: (b, i, 0)),
            pl.BlockSpec((1, D), lambda b, i, j: (0, 0)),
            pl.BlockSpec((1, D), lambda b, i, j: (0, 0)),
            pl.BlockSpec((None, 6, D), lambda b, i, j: (b, 0, 0)),
            pl.BlockSpec((None, D, tf), lambda b, i, j: (layer, 0, j)),
            pl.BlockSpec((None, tf, D), lambda b, i, j: (layer, j, 0)),
        ],
        out_specs=pl.BlockSpec((None, tm, D), lambda b, i, j: (b, i, 0)),
        out_shape=jax.ShapeDtypeStruct((B, T, D), F32),
        scratch_shapes=[pltpu.VMEM((tm, D), BF16)],
        compiler_params=_params(("parallel", "parallel", "arbitrary")),
        name="mlp",
    )(x, g1, g2, cond, w_up, w_dn)


def kernel(x, c, w_ada, b_ada, g_pre_mix, g_post_mix, w_in, hg_lb_logits, g_hg_norm,
           w_proj_sb, w_proj_hg, w_out, g_pre_mlp, g_post_mlp, w_mlp_up, w_mlp_down):
    B, T, D = x.shape
    L = w_ada.shape[0]

    lbs = jnp.cumsum(jax.nn.softmax(hg_lb_logits.astype(F32), axis=0), axis=0)
    lbs = lbs - lbs[0:1]
    lb_rows = jnp.stack([lbs, 1.0 - lbs], axis=1)

    c_pad = jnp.pad(c, ((0, 8 - B), (0, 0)))
    cond_all = _ada_cond(c_pad, w_ada, b_ada)[:, :B, :].reshape(L, B, 6, D)

    w_in, w_proj_sb, w_proj_hg, w_out, w_mlp_up, w_mlp_down = (
        w.astype(BF16) for w in (w_in, w_proj_sb, w_proj_hg, w_out, w_mlp_up, w_mlp_down))

    for l in range(L):
        cond = cond_all[l]
        p = _in_proj(x, g_pre_mix[l].reshape(1, D), cond, w_in, l)
        y_sb = _sb_attention(p)
        y_hg = _hgrn(p, lb_rows[l], g_hg_norm[l].reshape(1, -1))
        x = _mix_out(y_sb, y_hg, p, w_proj_sb, w_proj_hg, w_out, l,
                     x, g_post_mix[l].reshape(1, D), cond)
        x = _mlp(x, g_pre_mlp[l].reshape(1, D), g_post_mlp[l].reshape(1, D), cond,
                 w_mlp_up, w_mlp_down, l)
    return x
```

```python
import functools
import math

import jax
import jax.numpy as jnp
from jax import lax
from jax.experimental import pallas as pl
from jax.experimental.pallas import tpu as pltpu

F32 = jnp.float32
BF16 = jnp.bfloat16

EPS = 1e-6
HEAD_DIM = 128
SB_HEADS = 8
HG_HEADS = 8
HG_CHUNK = 32

SB_Q_BLK, SB_K_BLK, SB_V_BLK = 0, 8, 16
HG_Q_BLK, HG_F_BLK, HG_I_BLK, HG_G_BLK = 24, 32, 40, 48
GATE_A_COL, GATE_B_COL = 7168, 9216

VMEM_LIMIT = 56 * 1024 * 1024

SKIP_BELOW = -104.0
LOG2E = 1.4426950408889634

NT_DIMS = (((1,), (1,)), ((), ()))
TN_DIMS = (((0,), (0,)), ((), ()))


def _params(sem):
    return pltpu.CompilerParams(dimension_semantics=sem, vmem_limit_bytes=VMEM_LIMIT)


def _rms(xf, g):
    ms = jnp.mean(xf * xf, axis=-1, keepdims=True)
    return xf * lax.rsqrt(ms + EPS) * g


def _silu(x):
    hx = 0.5 * x
    return hx + hx * jnp.tanh(hx)


def _for_row_chunks(n_rows, chunk_rows, fn):
    def body(r, carry):
        fn(pl.ds(pl.multiple_of(r * chunk_rows, chunk_rows), chunk_rows))
        return carry
    lax.fori_loop(0, n_rows // chunk_rows, body, 0, unroll=4)


def _norm_modulate_to(h_ref, x_ref, g_ref, shift, scale):
    gmod = g_ref[...] * (1.0 + scale)

    def chunk(rows):
        x = x_ref[rows, :]
        ms = jnp.mean(x * x, axis=-1, keepdims=True)
        h_ref[rows, :] = (x * lax.rsqrt(ms + EPS) * gmod + shift).astype(h_ref.dtype)

    _for_row_chunks(x_ref.shape[0], 32, chunk)


def _split_bf16(a):
    hi = a.astype(BF16)
    lo = (a - hi.astype(F32)).astype(BF16)
    return hi, lo


def _ada_kernel(c_ref, w_ref, b_ref, o_ref):
    c = c_ref[...]
    ca = (c * jax.nn.sigmoid(c)).astype(BF16)
    o_ref[...] = jnp.dot(ca, w_ref[...].astype(BF16),
                         preferred_element_type=F32) + b_ref[...]


def _ada_cond(c_pad, w_ada, b_ada, tn=1024):
    L, D, N = w_ada.shape
    M = c_pad.shape[0]
    return pl.pallas_call(
        _ada_kernel,
        grid=(L, N // tn),
        in_specs=[
            pl.BlockSpec((M, D), lambda l, j: (0, 0)),
            pl.BlockSpec((None, D, tn), lambda l, j: (l, 0, j)),
            pl.BlockSpec((None, 1, tn), lambda l, j: (l, 0, j)),
        ],
        out_specs=pl.BlockSpec((None, M, tn), lambda l, j: (l, 0, j)),
        out_shape=jax.ShapeDtypeStruct((L, M, N), F32),
        compiler_params=_params(("parallel", "parallel")),
        name="ada_cond",
    )(c_pad, w_ada, b_ada.reshape(L, 1, N))


def _in_proj_kernel(x_ref, g_ref, cond_ref, w_ref, o_ref, h_ref):
    @pl.when(pl.program_id(2) == 0)
    def _():
        _norm_modulate_to(h_ref, x_ref, g_ref, cond_ref[0:1, :], cond_ref[1:2, :])

    o_ref[...] = jnp.dot(h_ref[...], w_ref[...],
                         preferred_element_type=F32).astype(o_ref.dtype)


def _in_proj(x, g, cond, w, tm=512, tn=2816):
    B, T, D = x.shape
    N = w.shape[1]
    return pl.pallas_call(
        _in_proj_kernel,
        grid=(B, T // tm, N // tn),
        in_specs=[
            pl.BlockSpec((None, tm, D), lambda b, i, j: (b, i, 0)),
            pl.BlockSpec((1, D), lambda b, i, j: (0, 0)),
            pl.BlockSpec((None, 6, D), lambda b, i, j: (b, 0, 0)),
            pl.BlockSpec((D, tn), lambda b, i, j: (0, j)),
        ],
        out_specs=pl.BlockSpec((None, tm, tn), lambda b, i, j: (b, i, j)),
        out_shape=jax.ShapeDtypeStruct((B, T, N), BF16),
        scratch_shapes=[pltpu.VMEM((tm, D), BF16)],
        compiler_params=_params(("parallel", "parallel", "arbitrary")),
        name="in_proj",
    )(x, g, cond, w)


def _sb_kernel(q_ref, k_ref, v_ref, u_ref, o_ref, *, tile, heads, scale):
    tiles_per_step = q_ref.shape[0] // tile

    def body(sub, carry):
        _sb_query_tile(q_ref, k_ref, v_ref, u_ref, o_ref,
                       pl.program_id(2) * tiles_per_step + sub,
                       pl.ds(pl.multiple_of(sub * tile, tile), tile),
                       tile=tile, heads=heads, scale=scale)
        return carry

    lax.fori_loop(0, tiles_per_step, body, 0)


def _sb_query_tile(q_ref, k_ref, v_ref, u_ref, o_ref, i, q_rows, *, tile, heads, scale):
    u = u_ref[...]
    row = lax.broadcasted_iota(jnp.int32, (tile, tile), 0)
    col = lax.broadcasted_iota(jnp.int32, (tile, tile), 1)
    causal = col < row

    def tile_terms(h, j, diagonal):
        lanes = slice(h * HEAD_DIM, (h + 1) * HEAD_DIM)
        keys = pl.ds(pl.multiple_of(j * tile, tile), tile)
        s = lax.dot_general(q_ref[q_rows, lanes], k_ref[keys, lanes], NT_DIMS,
                            preferred_element_type=F32)
        e = jnp.exp2(jnp.abs(s) * (-scale * LOG2E))
        ln = jnp.maximum(s, 0.0) * (-scale) - jnp.log(1.0 + e)
        if diagonal:
            ln = jnp.where(causal, ln, 0.0)
        hi, lo = _split_bf16(ln)
        incl = jnp.dot(jnp.concatenate([hi, lo], axis=1), u, preferred_element_type=F32)
        a = jnp.exp(s * scale + incl)
        if diagonal:
            a = jnp.where(causal, a, 0.0)
        pv = jnp.dot(a.astype(BF16), v_ref[keys, lanes], preferred_element_type=F32)
        return pv, incl[:, 0:1]

    has_prev = i > 0
    jp = jnp.maximum(i - 1, 0)
    carries, accs = [], []
    for h in range(heads):
        pv_d, tot_d = tile_terms(h, i, True)
        pv_p, tot_p = tile_terms(h, jp, False)
        accs.append(pv_d + jnp.where(has_prev, jnp.exp(tot_d), 0.0) * pv_p)
        carries.append(tot_d + jnp.where(has_prev, tot_p, 0.0))

    tops = [jnp.max(c) for c in carries]
    for h in range(heads):
        def live(state):
            j, top, _, _ = state
            return (j >= 0) & (top >= SKIP_BELOW)

        def step(state, h=h):
            j, _, carry, acc = state
            pv, tot = tile_terms(h, j, False)
            carry_new = carry + tot
            return j - 1, jnp.max(carry_new), carry_new, acc + jnp.exp(carry) * pv

        _, _, _, acc = lax.while_loop(live, step, (i - 2, tops[h], carries[h], accs[h]))
        o_ref[q_rows, h * HEAD_DIM:(h + 1) * HEAD_DIM] = acc.astype(o_ref.dtype)


def _sb_attention(p, tile=256, heads=8, tiles_per_step=4):
    B, T, _ = p.shape
    idx = jnp.arange(tile)
    u1 = (idx[:, None] >= idx[None, :]).astype(BF16)
    u = jnp.concatenate([u1, u1], axis=0)
    kern = functools.partial(_sb_kernel, tile=tile, heads=heads,
                             scale=1.0 / math.sqrt(HEAD_DIM))
    w = heads * HEAD_DIM
    groups = SB_HEADS // heads
    tq = tile * tiles_per_step
    return pl.pallas_call(
        kern,
        grid=(B, groups, T // tq),
        in_specs=[
            pl.BlockSpec((None, tq, w), lambda b, g, i: (b, i, SB_Q_BLK // heads + g)),
            pl.BlockSpec((None, T, w), lambda b, g, i: (b, 0, SB_K_BLK // heads + g)),
            pl.BlockSpec((None, T, w), lambda b, g, i: (b, 0, SB_V_BLK // heads + g)),
            pl.BlockSpec((2 * tile, tile), lambda b, g, i: (0, 0)),
        ],
        out_specs=pl.BlockSpec((None, tq, w), lambda b, g, i: (b, i, g)),
        out_shape=jax.ShapeDtypeStruct((B, T, SB_HEADS * HEAD_DIM), BF16),
        compiler_params=_params(("parallel", "parallel", "arbitrary")),
        name="sb_attention",
    )(p, p, p, u)


def _hgrn_kernel(hq_ref, hf_ref, hi_ref, hg_ref, lb_ref, gn_ref, tri_ref, *rest,
                 tb, chunk, heads, n_cast):
    cast_src, (o_ref, *cast_dst), st_ref = rest[:n_cast], rest[n_cast:-1], rest[-1]
    for src, dst in zip(cast_src, cast_dst):
        dst[...] = src[...].astype(dst.dtype)

    @pl.when(pl.program_id(2) == 0)
    def _():
        st_ref[...] = jnp.zeros_like(st_ref)

    def body(sub, carry):
        rows = pl.ds(pl.multiple_of(sub * tb, tb), tb)
        _hgrn_block(hq_ref, hf_ref, hi_ref, hg_ref, lb_ref, gn_ref, tri_ref, o_ref, st_ref,
                    rows, tb=tb, chunk=chunk, heads=heads)
        return carry

    lax.fori_loop(0, o_ref.shape[0] // tb, body, 0)


def _hgrn_block(hq_ref, hf_ref, hi_ref, hg_ref, lb_ref, gn_ref, tri_ref, o_ref, st_ref,
                rows, *, tb, chunk, heads):
    hf = hf_ref[rows, :].astype(F32)
    lb = lb_ref[0:1, :]
    one_m_lb = lb_ref[1:2, :]

    e = jnp.exp(-jnp.abs(hf))
    r = 1.0 / (1.0 + e)
    er = e * r
    pos = hf >= 0.0
    logf = jnp.log(lb + one_m_lb * jnp.where(pos, r, er))
    kk_all = one_m_lb * jnp.where(pos, er, r)
    qq = _silu(hq_ref[rows, :].astype(F32))

    hi, lo = _split_bf16(logf)
    g_all = jnp.dot(tri_ref[...], jnp.concatenate([hi, lo], axis=0),
                    preferred_element_type=F32)
    qg_all = (qq * jnp.exp(g_all)).astype(BF16)
    kg_all = (kk_all * jnp.exp(-g_all)).astype(BF16)

    row = lax.broadcasted_iota(jnp.int32, (tb, tb), 0)
    col = lax.broadcasted_iota(jnp.int32, (tb, tb), 1)
    keep = (col <= row) & ((row // chunk) == (col // chunk))
    chunks = [slice(n * chunk, (n + 1) * chunk) for n in range(tb // chunk)]

    for h in range(heads):
        lanes = slice(h * HEAD_DIM, (h + 1) * HEAD_DIM)
        g, kk, qg = g_all[:, lanes], kk_all[:, lanes], qg_all[:, lanes]
        v = hi_ref[rows, lanes]

        scores = lax.dot_general(qg, kg_all[:, lanes], NT_DIMS, preferred_element_type=F32)
        scores = jnp.where(keep, scores, 0.0).astype(BF16)
        o = jnp.dot(scores, v, preferred_element_type=F32)

        upds, decays = [], []
        for sl in chunks:
            g_end = g[sl.stop - 1:sl.stop, :]
            kdec = (kk[sl] * jnp.exp(g_end - g[sl])).astype(BF16)
            upds.append(lax.dot_general(v[sl], kdec, TN_DIMS, preferred_element_type=F32))
            decays.append(jnp.exp(g_end))

        st = st_ref[h]
        inter = []
        for sl, upd, dec in zip(chunks, upds, decays):
            inter.append(lax.dot_general(qg[sl], st.astype(BF16), NT_DIMS,
                                         preferred_element_type=F32))
            st = dec * st + upd
        st_ref[h] = st

        o = o + jnp.concatenate(inter, axis=0)
        y = _rms(o, gn_ref[:, lanes]) * _silu(hg_ref[rows, lanes].astype(F32))
        o_ref[rows, lanes] = y.astype(o_ref.dtype)


def _hgrn(p, lb_rows, gn, casts, tb=256, heads=8, blocks_per_step=2):
    B, T, _ = p.shape
    c = HG_CHUNK
    idx = jnp.arange(tb)
    same = (idx[:, None] // c) == (idx[None, :] // c)
    tri1 = (same & (idx[None, :] <= idx[:, None])).astype(BF16)
    tri = jnp.concatenate([tri1, tri1], axis=1)
    kern = functools.partial(_hgrn_kernel, tb=tb, chunk=c, heads=heads, n_cast=len(casts))
    w = heads * HEAD_DIM
    groups = HG_HEADS // heads
    ts = tb * blocks_per_step
    grid = (B, groups, T // ts)
    n_steps = grid[0] * grid[1] * grid[2]

    def col(blk):
        return pl.BlockSpec((None, ts, w), lambda b, g, i: (b, i, blk // heads + g))

    def step(b, g, i):
        return (b * grid[1] + g) * grid[2] + i

    cast_in, cast_out, cast_shapes = [], [], []
    for stack, layer in casts:
        _, R, C = stack.shape
        rows = R // n_steps
        cast_in.append(pl.BlockSpec((None, rows, C),
                                    lambda b, g, i, layer=layer: (layer, step(b, g, i), 0)))
        cast_out.append(pl.BlockSpec((rows, C), lambda b, g, i: (step(b, g, i), 0)))
        cast_shapes.append(jax.ShapeDtypeStruct((R, C), BF16))

    return pl.pallas_call(
        kern,
        grid=grid,
        in_specs=[
            col(HG_Q_BLK), col(HG_F_BLK), col(HG_I_BLK), col(HG_G_BLK),
            pl.BlockSpec((2, w), lambda b, g, i: (0, g)),
            pl.BlockSpec((1, w), lambda b, g, i: (0, g)),
            pl.BlockSpec((tb, 2 * tb), lambda b, g, i: (0, 0)),
        ] + cast_in,
        out_specs=[pl.BlockSpec((None, ts, w), lambda b, g, i: (b, i, g))] + cast_out,
        out_shape=[jax.ShapeDtypeStruct((B, T, HG_HEADS * HEAD_DIM), BF16)] + cast_shapes,
        scratch_shapes=[pltpu.VMEM((heads, HEAD_DIM, HEAD_DIM), F32)],
        compiler_params=_params(("parallel", "parallel", "arbitrary")),
        name="hgrn2",
    )(p, p, p, p, lb_rows, gn, tri, *[stack for stack, _ in casts])


def _mix_out_kernel(ysb_ref, yhg_ref, ga0_ref, ga1_ref, gb0_ref, gb1_ref,
                    wsb_ref, whg_ref, wout_ref, x_ref, g_ref, cond_ref, o_ref):
    ysb = ysb_ref[...]
    yhg = yhg_ref[...]
    half = ga0_ref.shape[-1]
    mo = None
    for c, (ga_ref, gb_ref) in enumerate(((ga0_ref, gb0_ref), (ga1_ref, gb1_ref))):
        cols = slice(c * half, (c + 1) * half)
        a = jnp.dot(ysb, wsb_ref[:, cols], preferred_element_type=F32)
        b = jnp.dot(yhg, whg_ref[:, cols], preferred_element_type=F32)
        m = (jax.nn.sigmoid(ga_ref[...].astype(F32)) * a
             + jax.nn.sigmoid(gb_ref[...].astype(F32)) * b)
        part = jnp.dot(m.astype(BF16), wout_ref[cols, :], preferred_element_type=F32)
        mo = part if mo is None else mo + part
    o_ref[...] = x_ref[...] + cond_ref[2:3, :] * _rms(mo, g_ref[...])


def _mix_out(y_sb, y_hg, p, w_sb, w_hg, w_out, layer, x, g, cond, tm=512):
    B, T, D = x.shape
    W = y_sb.shape[-1]
    half = D // 2
    ga_blk, gb_blk = GATE_A_COL // half, GATE_B_COL // half

    def gate(blk):
        return pl.BlockSpec((None, tm, half), lambda b, i: (b, i, blk))

    def resident(shape):
        return pl.BlockSpec((None,) + shape, lambda b, i: (layer, 0, 0),
                            pipeline_mode=pl.Buffered(1))

    return pl.pallas_call(
        _mix_out_kernel,
        grid=(B, T // tm),
        in_specs=[
            pl.BlockSpec((None, tm, W), lambda b, i: (b, i, 0)),
            pl.BlockSpec((None, tm, W), lambda b, i: (b, i, 0)),
            gate(ga_blk), gate(ga_blk + 1), gate(gb_blk), gate(gb_blk + 1),
            resident((W, D)), resident((W, D)), resident((D, D)),
            pl.BlockSpec((None, tm, D), lambda b, i: (b, i, 0)),
            pl.BlockSpec((1, D), lambda b, i: (0, 0)),
            pl.BlockSpec((None, 6, D), lambda b, i: (b, 0, 0)),
        ],
        out_specs=pl.BlockSpec((None, tm, D), lambda b, i: (b, i, 0)),
        out_shape=jax.ShapeDtypeStruct((B, T, D), F32),
        compiler_params=_params(("parallel", "parallel")),
        name="mix_out",
    )(y_sb, y_hg, p, p, p, p, w_sb, w_hg, w_out, x, g, cond)


def _mlp_kernel(x_ref, g1_ref, g2_ref, cond_ref, wup_ref, wdn_ref, o_ref, h_ref, *, sub):
    j = pl.program_id(2)

    @pl.when(j == 0)
    def _():
        _norm_modulate_to(h_ref, x_ref, g1_ref, cond_ref[3:4, :], cond_ref[4:5, :])
        o_ref[...] = jnp.zeros_like(o_ref)

    for c in range(wup_ref.shape[1] // sub):
        cols = slice(c * sub, (c + 1) * sub)
        u = jnp.maximum(jnp.dot(h_ref[...], wup_ref[:, cols], preferred_element_type=F32), 0.0)
        o_ref[...] += jnp.dot((u * u).astype(BF16), wdn_ref[cols, :],
                              preferred_element_type=F32)

    @pl.when(j == pl.num_programs(2) - 1)
    def _():
        gate = cond_ref[5:6, :]
        g2 = g2_ref[...]

        for r in range(0, o_ref.shape[0], 32):
            rows = slice(r, r + 32)
            o_ref[rows, :] = x_ref[rows, :] + gate * _rms(o_ref[rows, :], g2)


def _mlp(x, g1, g2, cond, w_up, w_dn, tm=512, tf=2048, sub=1024):
    B, T, D = x.shape
    F = w_up.shape[1]
    return pl.pallas_call(
        functools.partial(_mlp_kernel, sub=sub),
        grid=(B, T // tm, F // tf),
        in_specs=[
            pl.BlockSpec((None, tm, D), lambda b, i, j: (b, i, 0)),
            pl.BlockSpec((1, D), lambda b, i, j: (0, 0)),
            pl.BlockSpec((1, D), lambda b, i, j: (0, 0)),
            pl.BlockSpec((None, 6, D), lambda b, i, j: (b, 0, 0)),
            pl.BlockSpec((D, tf), lambda b, i, j: (0, j)),
            pl.BlockSpec((tf, D), lambda b, i, j: (j, 0)),
        ],
        out_specs=pl.BlockSpec((None, tm, D), lambda b, i, j: (b, i, 0)),
        out_shape=jax.ShapeDtypeStruct((B, T, D), F32),
        scratch_shapes=[pltpu.VMEM((tm, D), BF16)],
        compiler_params=_params(("parallel", "parallel", "arbitrary")),
        name="mlp",
    )(x, g1, g2, cond, w_up, w_dn)


def kernel(x, c, w_ada, b_ada, g_pre_mix, g_post_mix, w_in, hg_lb_logits, g_hg_norm,
           w_proj_sb, w_proj_hg, w_out, g_pre_mlp, g_post_mlp, w_mlp_up, w_mlp_down):
    B, T, D = x.shape
    L = w_ada.shape[0]

    lbs = jnp.cumsum(jax.nn.softmax(hg_lb_logits.astype(F32), axis=0), axis=0)
    lbs = lbs - lbs[0:1]
    lb_rows = jnp.stack([lbs, 1.0 - lbs], axis=1)

    c_pad = jnp.pad(c, ((0, 8 - B), (0, 0)))
    cond_all = _ada_cond(c_pad, w_ada, b_ada)[:, :B, :].reshape(L, B, 6, D)

    w_in_l = w_in[0].astype(BF16)
    w_proj_sb, w_proj_hg, w_out = (w.astype(BF16) for w in (w_proj_sb, w_proj_hg, w_out))

    for l in range(L):
        cond = cond_all[l]
        p = _in_proj(x, g_pre_mix[l].reshape(1, D), cond, w_in_l)
        y_sb = _sb_attention(p)
        casts = [(w_mlp_up, l), (w_mlp_down, l)] + ([(w_in, l + 1)] if l + 1 < L else [])
        y_hg, w_up_l, w_dn_l, *w_in_next = _hgrn(p, lb_rows[l], g_hg_norm[l].reshape(1, -1),
                                                 casts)
        x = _mix_out(y_sb, y_hg, p, w_proj_sb, w_proj_hg, w_out, l,
                     x, g_post_mix[l].reshape(1, D), cond)
        x = _mlp(x, g_pre_mlp[l].reshape(1, D), g_post_mlp[l].reshape(1, D), cond,
                 w_up_l, w_dn_l)
        if w_in_next:
            w_in_l = w_in_next[0]
    return x
```

```python
import functools
import math

import jax
import jax.numpy as jnp
from jax import lax
from jax.experimental import pallas as pl
from jax.experimental.pallas import tpu as pltpu

F32 = jnp.float32
BF16 = jnp.bfloat16

EPS = 1e-6
HEAD_DIM = 128
SB_HEADS = 8
HG_HEADS = 8
HG_CHUNK = 32

SB_Q_BLK, SB_K_BLK, SB_V_BLK = 0, 8, 16
HG_Q_BLK, HG_F_BLK, HG_I_BLK, HG_G_BLK = 24, 32, 40, 48
GATE_A_COL, GATE_B_COL = 7168, 9216

VMEM_LIMIT = 56 * 1024 * 1024

SKIP_BELOW = -104.0
LOG2E = 1.4426950408889634

NT_DIMS = (((1,), (1,)), ((), ()))
TN_DIMS = (((0,), (0,)), ((), ()))


def _params(sem):
    return pltpu.CompilerParams(dimension_semantics=sem, vmem_limit_bytes=VMEM_LIMIT)


def _rms(xf, g):
    ms = jnp.mean(xf * xf, axis=-1, keepdims=True)
    return xf * lax.rsqrt(ms + EPS) * g


def _silu(x):
    hx = 0.5 * x
    return hx + hx * jnp.tanh(hx)


def _for_row_chunks(n_rows, chunk_rows, fn):
    def body(r, carry):
        fn(pl.ds(pl.multiple_of(r * chunk_rows, chunk_rows), chunk_rows))
        return carry
    lax.fori_loop(0, n_rows // chunk_rows, body, 0, unroll=4)


def _norm_modulate_to(h_ref, x_ref, g_ref, shift, scale):
    gmod = g_ref[...] * (1.0 + scale)

    def chunk(rows):
        x = x_ref[rows, :]
        ms = jnp.mean(x * x, axis=-1, keepdims=True)
        h_ref[rows, :] = (x * lax.rsqrt(ms + EPS) * gmod + shift).astype(h_ref.dtype)

    _for_row_chunks(x_ref.shape[0], 32, chunk)


def _split_bf16(a):
    hi = a.astype(BF16)
    lo = (a - hi.astype(F32)).astype(BF16)
    return hi, lo


def _ada_kernel(c_ref, w_ref, b_ref, o_ref):
    c = c_ref[...]
    ca = (c * jax.nn.sigmoid(c)).astype(BF16)
    o_ref[...] = jnp.dot(ca, w_ref[...].astype(BF16),
                         preferred_element_type=F32) + b_ref[...]


def _ada_cond(c_pad, w_ada, b_ada, tn=1024):
    L, D, N = w_ada.shape
    M = c_pad.shape[0]
    return pl.pallas_call(
        _ada_kernel,
        grid=(L, N // tn),
        in_specs=[
            pl.BlockSpec((M, D), lambda l, j: (0, 0)),
            pl.BlockSpec((None, D, tn), lambda l, j: (l, 0, j)),
            pl.BlockSpec((None, 1, tn), lambda l, j: (l, 0, j)),
        ],
        out_specs=pl.BlockSpec((None, M, tn), lambda l, j: (l, 0, j)),
        out_shape=jax.ShapeDtypeStruct((L, M, N), F32),
        compiler_params=_params(("parallel", "parallel")),
        name="ada_cond",
    )(c_pad, w_ada, b_ada.reshape(L, 1, N))


def _in_proj_kernel(x_ref, g_ref, cond_ref, w_ref, o_ref, h_ref):
    @pl.when(pl.program_id(2) == 0)
    def _():
        _norm_modulate_to(h_ref, x_ref, g_ref, cond_ref[0:1, :], cond_ref[1:2, :])

    o_ref[...] = jnp.dot(h_ref[...], w_ref[...],
                         preferred_element_type=F32).astype(o_ref.dtype)


def _in_proj(x, g, cond, w, tm=512, tn=2816):
    B, T, D = x.shape
    N = w.shape[1]
    return pl.pallas_call(
        _in_proj_kernel,
        grid=(B, T // tm, N // tn),
        in_specs=[
            pl.BlockSpec((None, tm, D), lambda b, i, j: (b, i, 0)),
            pl.BlockSpec((1, D), lambda b, i, j: (0, 0)),
            pl.BlockSpec((None, 6, D), lambda b, i, j: (b, 0, 0)),
            pl.BlockSpec((D, tn), lambda b, i, j: (0, j)),
        ],
        out_specs=pl.BlockSpec((None, tm, tn), lambda b, i, j: (b, i, j)),
        out_shape=jax.ShapeDtypeStruct((B, T, N), BF16),
        scratch_shapes=[pltpu.VMEM((tm, D), BF16)],
        compiler_params=_params(("parallel", "parallel", "arbitrary")),
        name="in_proj",
    )(x, g, cond, w)


def _sb_kernel(q_ref, k_ref, v_ref, u_ref, o_ref, *, tile, heads, scale):
    tiles_per_step = q_ref.shape[0] // tile

    def body(sub, carry):
        _sb_query_tile(q_ref, k_ref, v_ref, u_ref, o_ref,
                       pl.program_id(2) * tiles_per_step + sub,
                       pl.ds(pl.multiple_of(sub * tile, tile), tile),
                       tile=tile, heads=heads, scale=scale)
        return carry

    lax.fori_loop(0, tiles_per_step, body, 0)


def _sb_query_tile(q_ref, k_ref, v_ref, u_ref, o_ref, i, q_rows, *, tile, heads, scale):
    u = u_ref[...]
    row = lax.broadcasted_iota(jnp.int32, (tile, tile), 0)
    col = lax.broadcasted_iota(jnp.int32, (tile, tile), 1)
    causal = col < row

    def tile_terms(h, j, diagonal):
        lanes = slice(h * HEAD_DIM, (h + 1) * HEAD_DIM)
        keys = pl.ds(pl.multiple_of(j * tile, tile), tile)
        s = lax.dot_general(q_ref[q_rows, lanes], k_ref[keys, lanes], NT_DIMS,
                            preferred_element_type=F32)
        e = jnp.exp2(jnp.abs(s) * (-scale * LOG2E))
        ln = jnp.maximum(s, 0.0) * (-scale) - jnp.log(1.0 + e)
        if diagonal:
            ln = jnp.where(causal, ln, 0.0)
        hi, lo = _split_bf16(ln)
        incl = jnp.dot(jnp.concatenate([hi, lo], axis=1), u, preferred_element_type=F32)
        a = jnp.exp(s * scale + incl)
        if diagonal:
            a = jnp.where(causal, a, 0.0)
        pv = jnp.dot(a.astype(BF16), v_ref[keys, lanes], preferred_element_type=F32)
        return pv, incl[:, 0:1]

    has_prev = i > 0
    jp = jnp.maximum(i - 1, 0)
    carries, accs = [], []
    for h in range(heads):
        pv_d, tot_d = tile_terms(h, i, True)
        pv_p, tot_p = tile_terms(h, jp, False)
        accs.append(pv_d + jnp.where(has_prev, jnp.exp(tot_d), 0.0) * pv_p)
        carries.append(tot_d + jnp.where(has_prev, tot_p, 0.0))

    tops = [jnp.max(c) for c in carries]
    for h in range(heads):
        def live(state):
            j, top, _, _ = state
            return (j >= 0) & (top >= SKIP_BELOW)

        def step(state, h=h):
            j, _, carry, acc = state
            pv, tot = tile_terms(h, j, False)
            carry_new = carry + tot
            return j - 1, jnp.max(carry_new), carry_new, acc + jnp.exp(carry) * pv

        _, _, _, acc = lax.while_loop(live, step, (i - 2, tops[h], carries[h], accs[h]))
        o_ref[q_rows, h * HEAD_DIM:(h + 1) * HEAD_DIM] = acc.astype(o_ref.dtype)


def _sb_attention(p, tile=256, heads=8, tiles_per_step=4):
    B, T, _ = p.shape
    idx = jnp.arange(tile)
    u1 = (idx[:, None] >= idx[None, :]).astype(BF16)
    u = jnp.concatenate([u1, u1], axis=0)
    kern = functools.partial(_sb_kernel, tile=tile, heads=heads,
                             scale=1.0 / math.sqrt(HEAD_DIM))
    w = heads * HEAD_DIM
    groups = SB_HEADS // heads
    tq = tile * tiles_per_step
    return pl.pallas_call(
        kern,
        grid=(B, groups, T // tq),
        in_specs=[
            pl.BlockSpec((None, tq, w), lambda b, g, i: (b, i, SB_Q_BLK // heads + g)),
            pl.BlockSpec((None, T, w), lambda b, g, i: (b, 0, SB_K_BLK // heads + g)),
            pl.BlockSpec((None, T, w), lambda b, g, i: (b, 0, SB_V_BLK // heads + g)),
            pl.BlockSpec((2 * tile, tile), lambda b, g, i: (0, 0)),
        ],
        out_specs=pl.BlockSpec((None, tq, w), lambda b, g, i: (b, i, g)),
        out_shape=jax.ShapeDtypeStruct((B, T, SB_HEADS * HEAD_DIM), BF16),
        compiler_params=_params(("parallel", "parallel", "arbitrary")),
        name="sb_attention",
    )(p, p, p, u)


def _hgrn_kernel(hq_ref, hf_ref, hi_ref, hg_ref, lb_ref, gn_ref, tri_ref, *rest,
                 tb, chunk, heads, n_cast):
    cast_src, (o_ref, *cast_dst), st_ref = rest[:n_cast], rest[n_cast:-1], rest[-1]
    for src, dst in zip(cast_src, cast_dst):
        dst[...] = src[...].astype(dst.dtype)

    @pl.when(pl.program_id(2) == 0)
    def _():
        st_ref[...] = jnp.zeros_like(st_ref)

    def body(sub, carry):
        rows = pl.ds(pl.multiple_of(sub * tb, tb), tb)
        _hgrn_block(hq_ref, hf_ref, hi_ref, hg_ref, lb_ref, gn_ref, tri_ref, o_ref, st_ref,
                    rows, tb=tb, chunk=chunk, heads=heads)
        return carry

    lax.fori_loop(0, o_ref.shape[0] // tb, body, 0)


def _hgrn_block(hq_ref, hf_ref, hi_ref, hg_ref, lb_ref, gn_ref, tri_ref, o_ref, st_ref,
                rows, *, tb, chunk, heads):
    hf = hf_ref[rows, :].astype(F32)
    lb = lb_ref[0:1, :]
    one_m_lb = lb_ref[1:2, :]

    e = jnp.exp(-jnp.abs(hf))
    r = 1.0 / (1.0 + e)
    er = e * r
    pos = hf >= 0.0
    logf = jnp.log(lb + one_m_lb * jnp.where(pos, r, er))
    kk_all = one_m_lb * jnp.where(pos, er, r)
    qq = _silu(hq_ref[rows, :].astype(F32))

    hi, lo = _split_bf16(logf)
    g_all = jnp.dot(tri_ref[...], jnp.concatenate([hi, lo], axis=0),
                    preferred_element_type=F32)
    qg_all = (qq * jnp.exp(g_all)).astype(BF16)
    kg_all = (kk_all * jnp.exp(-g_all)).astype(BF16)

    row = lax.broadcasted_iota(jnp.int32, (tb, tb), 0)
    col = lax.broadcasted_iota(jnp.int32, (tb, tb), 1)
    keep = (col <= row) & ((row // chunk) == (col // chunk))
    chunks = [slice(n * chunk, (n + 1) * chunk) for n in range(tb // chunk)]

    for h in range(heads):
        lanes = slice(h * HEAD_DIM, (h + 1) * HEAD_DIM)
        g, kk, qg = g_all[:, lanes], kk_all[:, lanes], qg_all[:, lanes]
        v = hi_ref[rows, lanes]

        scores = lax.dot_general(qg, kg_all[:, lanes], NT_DIMS, preferred_element_type=F32)
        scores = jnp.where(keep, scores, 0.0).astype(BF16)
        o = jnp.dot(scores, v, preferred_element_type=F32)

        upds, decays = [], []
        for sl in chunks:
            g_end = g[sl.stop - 1:sl.stop, :]
            kdec = (kk[sl] * jnp.exp(g_end - g[sl])).astype(BF16)
            upds.append(lax.dot_general(v[sl], kdec, TN_DIMS, preferred_element_type=F32))
            decays.append(jnp.exp(g_end))

        st = st_ref[h]
        inter = []
        for sl, upd, dec in zip(chunks, upds, decays):
            inter.append(lax.dot_general(qg[sl], st.astype(BF16), NT_DIMS,
                                         preferred_element_type=F32))
            st = dec * st + upd
        st_ref[h] = st

        o = o + jnp.concatenate(inter, axis=0)
        y = _rms(o, gn_ref[:, lanes]) * _silu(hg_ref[rows, lanes].astype(F32))
        o_ref[rows, lanes] = y.astype(o_ref.dtype)


def _hgrn(p, lb_rows, gn, casts, tb=256, heads=8, blocks_per_step=2):
    B, T, _ = p.shape
    c = HG_CHUNK
    idx = jnp.arange(tb)
    same = (idx[:, None] // c) == (idx[None, :] // c)
    tri1 = (same & (idx[None, :] <= idx[:, None])).astype(BF16)
    tri = jnp.concatenate([tri1, tri1], axis=1)
    kern = functools.partial(_hgrn_kernel, tb=tb, chunk=c, heads=heads, n_cast=len(casts))
    w = heads * HEAD_DIM
    groups = HG_HEADS // heads
    ts = tb * blocks_per_step
    grid = (B, groups, T // ts)
    n_steps = grid[0] * grid[1] * grid[2]

    def col(blk):
        return pl.BlockSpec((None, ts, w), lambda b, g, i: (b, i, blk // heads + g))

    def step(b, g, i):
        return (b * grid[1] + g) * grid[2] + i

    cast_in, cast_out, cast_shapes = [], [], []
    for stack, layer in casts:
        _, R, C = stack.shape
        rows = R // n_steps
        cast_in.append(pl.BlockSpec((None, rows, C),
                                    lambda b, g, i, layer=layer: (layer, step(b, g, i), 0)))
        cast_out.append(pl.BlockSpec((rows, C), lambda b, g, i: (step(b, g, i), 0)))
        cast_shapes.append(jax.ShapeDtypeStruct((R, C), BF16))

    return pl.pallas_call(
        kern,
        grid=grid,
        in_specs=[
            col(HG_Q_BLK), col(HG_F_BLK), col(HG_I_BLK), col(HG_G_BLK),
            pl.BlockSpec((2, w), lambda b, g, i: (0, g)),
            pl.BlockSpec((1, w), lambda b, g, i: (0, g)),
            pl.BlockSpec((tb, 2 * tb), lambda b, g, i: (0, 0)),
        ] + cast_in,
        out_specs=[pl.BlockSpec((None, ts, w), lambda b, g, i: (b, i, g))] + cast_out,
        out_shape=[jax.ShapeDtypeStruct((B, T, HG_HEADS * HEAD_DIM), BF16)] + cast_shapes,
        scratch_shapes=[pltpu.VMEM((heads, HEAD_DIM, HEAD_DIM), F32)],
        compiler_params=_params(("parallel", "parallel", "arbitrary")),
        name="hgrn2",
    )(p, p, p, p, lb_rows, gn, tri, *[stack for stack, _ in casts])


def _mix_out_kernel(ysb_ref, yhg_ref, ga0_ref, ga1_ref, gb0_ref, gb1_ref,
                    wsb_ref, whg_ref, wout_ref, x_ref, g_ref, cond_ref, o_ref):
    ysb = ysb_ref[...]
    yhg = yhg_ref[...]
    half = ga0_ref.shape[-1]
    mo = None
    for c, (ga_ref, gb_ref) in enumerate(((ga0_ref, gb0_ref), (ga1_ref, gb1_ref))):
        cols = slice(c * half, (c + 1) * half)
        a = jnp.dot(ysb, wsb_ref[:, cols], preferred_element_type=F32)
        b = jnp.dot(yhg, whg_ref[:, cols], preferred_element_type=F32)
        m = (jax.nn.sigmoid(ga_ref[...].astype(F32)) * a
             + jax.nn.sigmoid(gb_ref[...].astype(F32)) * b)
        part = jnp.dot(m.astype(BF16), wout_ref[cols, :], preferred_element_type=F32)
        mo = part if mo is None else mo + part
    o_ref[...] = x_ref[...] + cond_ref[2:3, :] * _rms(mo, g_ref[...])


def _mix_out(y_sb, y_hg, p, w_sb, w_hg, w_out, x, g, cond, tm=512):
    B, T, D = x.shape
    W = y_sb.shape[-1]
    half = D // 2
    ga_blk, gb_blk = GATE_A_COL // half, GATE_B_COL // half

    def gate(blk):
        return pl.BlockSpec((None, tm, half), lambda b, i: (b, i, blk))

    def resident(shape):
        return pl.BlockSpec(shape, lambda b, i: (0, 0), pipeline_mode=pl.Buffered(1))

    return pl.pallas_call(
        _mix_out_kernel,
        grid=(B, T // tm),
        in_specs=[
            pl.BlockSpec((None, tm, W), lambda b, i: (b, i, 0)),
            pl.BlockSpec((None, tm, W), lambda b, i: (b, i, 0)),
            gate(ga_blk), gate(ga_blk + 1), gate(gb_blk), gate(gb_blk + 1),
            resident((W, D)), resident((W, D)), resident((D, D)),
            pl.BlockSpec((None, tm, D), lambda b, i: (b, i, 0)),
            pl.BlockSpec((1, D), lambda b, i: (0, 0)),
            pl.BlockSpec((None, 6, D), lambda b, i: (b, 0, 0)),
        ],
        out_specs=pl.BlockSpec((None, tm, D), lambda b, i: (b, i, 0)),
        out_shape=jax.ShapeDtypeStruct((B, T, D), F32),
        compiler_params=_params(("parallel", "parallel")),
        name="mix_out",
    )(y_sb, y_hg, p, p, p, p, w_sb, w_hg, w_out, x, g, cond)


def _mlp_kernel(x_ref, g1_ref, g2_ref, cond_ref, wup_ref, wdn_ref, o_ref, h_ref, *, sub):
    j = pl.program_id(2)

    @pl.when(j == 0)
    def _():
        _norm_modulate_to(h_ref, x_ref, g1_ref, cond_ref[3:4, :], cond_ref[4:5, :])
        o_ref[...] = jnp.zeros_like(o_ref)

    for c in range(wup_ref.shape[1] // sub):
        cols = slice(c * sub, (c + 1) * sub)
        u = jnp.maximum(jnp.dot(h_ref[...], wup_ref[:, cols], preferred_element_type=F32), 0.0)
        o_ref[...] += jnp.dot((u * u).astype(BF16), wdn_ref[cols, :],
                              preferred_element_type=F32)

    @pl.when(j == pl.num_programs(2) - 1)
    def _():
        gate = cond_ref[5:6, :]
        g2 = g2_ref[...]

        for r in range(0, o_ref.shape[0], 32):
            rows = slice(r, r + 32)
            o_ref[rows, :] = x_ref[rows, :] + gate * _rms(o_ref[rows, :], g2)


def _mlp(x, g1, g2, cond, w_up, w_dn, tm=512, tf=2048, sub=1024):
    B, T, D = x.shape
    F = w_up.shape[1]
    return pl.pallas_call(
        functools.partial(_mlp_kernel, sub=sub),
        grid=(B, T // tm, F // tf),
        in_specs=[
            pl.BlockSpec((None, tm, D), lambda b, i, j: (b, i, 0)),
            pl.BlockSpec((1, D), lambda b, i, j: (0, 0)),
            pl.BlockSpec((1, D), lambda b, i, j: (0, 0)),
            pl.BlockSpec((None, 6, D), lambda b, i, j: (b, 0, 0)),
            pl.BlockSpec((D, tf), lambda b, i, j: (0, j)),
            pl.BlockSpec((tf, D), lambda b, i, j: (j, 0)),
        ],
        out_specs=pl.BlockSpec((None, tm, D), lambda b, i, j: (b, i, 0)),
        out_shape=jax.ShapeDtypeStruct((B, T, D), F32),
        scratch_shapes=[pltpu.VMEM((tm, D), BF16)],
        compiler_params=_params(("parallel", "parallel", "arbitrary")),
        name="mlp",
    )(x, g1, g2, cond, w_up, w_dn)


def kernel(x, c, w_ada, b_ada, g_pre_mix, g_post_mix, w_in, hg_lb_logits, g_hg_norm,
           w_proj_sb, w_proj_hg, w_out, g_pre_mlp, g_post_mlp, w_mlp_up, w_mlp_down):
    B, T, D = x.shape
    L = w_ada.shape[0]

    lbs = jnp.cumsum(jax.nn.softmax(hg_lb_logits.astype(F32), axis=0), axis=0)
    lbs = lbs - lbs[0:1]
    lb_rows = jnp.stack([lbs, 1.0 - lbs], axis=1)

    c_pad = jnp.pad(c, ((0, 8 - B), (0, 0)))
    cond_all = _ada_cond(c_pad, w_ada, b_ada)[:, :B, :].reshape(L, B, 6, D)

    w_in_l = w_in[0].astype(BF16)

    for l in range(L):
        cond = cond_all[l]
        p = _in_proj(x, g_pre_mix[l].reshape(1, D), cond, w_in_l)
        y_sb = _sb_attention(p)
        casts = [(w, l) for w in (w_proj_sb, w_proj_hg, w_out, w_mlp_up, w_mlp_down)]
        casts += [(w_in, l + 1)] if l + 1 < L else []
        y_hg, w_sb_l, w_hg_l, w_out_l, w_up_l, w_dn_l, *w_in_next = _hgrn(
            p, lb_rows[l], g_hg_norm[l].reshape(1, -1), casts)
        x = _mix_out(y_sb, y_hg, p, w_sb_l, w_hg_l, w_out_l,
                     x, g_post_mix[l].reshape(1, D), cond)
        x = _mlp(x, g_pre_mlp[l].reshape(1, D), g_post_mlp[l].reshape(1, D), cond,
                 w_up_l, w_dn_l)
        if w_in_next:
            w_in_l = w_in_next[0]
    return x
```

```python
import functools
import math

import jax
import jax.numpy as jnp
from jax import lax
from jax.experimental import pallas as pl
from jax.experimental.pallas import tpu as pltpu

F32 = jnp.float32
BF16 = jnp.bfloat16

EPS = 1e-6
HEAD_DIM = 128
SB_HEADS = 8
HG_HEADS = 8
HG_CHUNK = 32

SB_Q_BLK, SB_K_BLK, SB_V_BLK = 0, 8, 16
HG_Q_BLK, HG_F_BLK, HG_I_BLK, HG_G_BLK = 24, 32, 40, 48
GATE_A_COL, GATE_B_COL = 7168, 9216

VMEM_LIMIT = 56 * 1024 * 1024

SKIP_BELOW = -104.0
LOG2E = 1.4426950408889634

NT_DIMS = (((1,), (1,)), ((), ()))
TN_DIMS = (((0,), (0,)), ((), ()))


def _params(sem):
    return pltpu.CompilerParams(dimension_semantics=sem, vmem_limit_bytes=VMEM_LIMIT)


def _rms(xf, g):
    ms = jnp.mean(xf * xf, axis=-1, keepdims=True)
    return xf * lax.rsqrt(ms + EPS) * g


def _silu(x):
    hx = 0.5 * x
    return hx + hx * jnp.tanh(hx)


def _for_row_chunks(n_rows, chunk_rows, fn, first_row=0):
    def body(r, carry):
        fn(pl.ds(pl.multiple_of(first_row + r * chunk_rows, chunk_rows), chunk_rows))
        return carry
    lax.fori_loop(0, n_rows // chunk_rows, body, 0, unroll=4)


def _norm_modulate_to(h_ref, x_ref, g_ref, shift, scale, first_row=0, n_rows=None,
                      inline=False):
    gmod = g_ref[...] * (1.0 + scale)
    n_rows = x_ref.shape[0] if n_rows is None else n_rows

    def chunk(rows):
        x = x_ref[rows, :]
        ms = jnp.mean(x * x, axis=-1, keepdims=True)
        h_ref[rows, :] = (x * lax.rsqrt(ms + EPS) * gmod + shift).astype(h_ref.dtype)

    if inline:
        for r in range(first_row, first_row + n_rows, 32):
            chunk(slice(r, r + 32))
    else:
        _for_row_chunks(n_rows, 32, chunk, first_row)


def _split_bf16(a):
    hi = a.astype(BF16)
    lo = (a - hi.astype(F32)).astype(BF16)
    return hi, lo


def _ada_kernel(c_ref, w_ref, b_ref, o_ref):
    c = c_ref[...]
    ca = (c * jax.nn.sigmoid(c)).astype(BF16)
    o_ref[...] = jnp.dot(ca, w_ref[...].astype(BF16),
                         preferred_element_type=F32) + b_ref[...]


def _ada_cond(c_pad, w_ada, b_ada, tn=1024):
    L, D, N = w_ada.shape
    M = c_pad.shape[0]
    return pl.pallas_call(
        _ada_kernel,
        grid=(L, N // tn),
        in_specs=[
            pl.BlockSpec((M, D), lambda l, j: (0, 0)),
            pl.BlockSpec((None, D, tn), lambda l, j: (l, 0, j)),
            pl.BlockSpec((None, 1, tn), lambda l, j: (l, 0, j)),
        ],
        out_specs=pl.BlockSpec((None, M, tn), lambda l, j: (l, 0, j)),
        out_shape=jax.ShapeDtypeStruct((L, M, N), F32),
        compiler_params=_params(("parallel", "parallel")),
        name="ada_cond",
    )(c_pad, w_ada, b_ada.reshape(L, 1, N))


def _in_proj_kernel(x_ref, g_ref, cond_ref, w_ref, o_ref, h_ref):
    half = h_ref.shape[0] // 2

    def project(rows):
        o_ref[rows, :] = jnp.dot(h_ref[rows, :], w_ref[...],
                                 preferred_element_type=F32).astype(o_ref.dtype)

    @pl.when(pl.program_id(2) == 0)
    def _():
        _norm_modulate_to(h_ref, x_ref, g_ref, cond_ref[0:1, :], cond_ref[1:2, :],
                          first_row=0, n_rows=half)
        project(slice(0, half))
        _norm_modulate_to(h_ref, x_ref, g_ref, cond_ref[0:1, :], cond_ref[1:2, :],
                          first_row=half, n_rows=half, inline=True)
        project(slice(half, 2 * half))

    @pl.when(pl.program_id(2) > 0)
    def _():
        project(slice(None))


def _in_proj(x, g, cond, w, tm=512, tn=2816):
    B, T, D = x.shape
    N = w.shape[1]
    return pl.pallas_call(
        _in_proj_kernel,
        grid=(B, T // tm, N // tn),
        in_specs=[
            pl.BlockSpec((None, tm, D), lambda b, i, j: (b, i, 0)),
            pl.BlockSpec((1, D), lambda b, i, j: (0, 0)),
            pl.BlockSpec((None, 6, D), lambda b, i, j: (b, 0, 0)),
            pl.BlockSpec((D, tn), lambda b, i, j: (0, j)),
        ],
        out_specs=pl.BlockSpec((None, tm, tn), lambda b, i, j: (b, i, j)),
        out_shape=jax.ShapeDtypeStruct((B, T, N), BF16),
        scratch_shapes=[pltpu.VMEM((tm, D), BF16)],
        compiler_params=_params(("parallel", "parallel", "arbitrary")),
        name="in_proj",
    )(x, g, cond, w)


def _sb_kernel(q_ref, k_ref, v_ref, u_ref, o_ref, *, tile, heads, scale):
    tiles_per_step = q_ref.shape[0] // tile

    def body(sub, carry):
        _sb_query_tile(q_ref, k_ref, v_ref, u_ref, o_ref,
                       pl.program_id(2) * tiles_per_step + sub,
                       pl.ds(pl.multiple_of(sub * tile, tile), tile),
                       tile=tile, heads=heads, scale=scale)
        return carry

    lax.fori_loop(0, tiles_per_step, body, 0)


def _sb_query_tile(q_ref, k_ref, v_ref, u_ref, o_ref, i, q_rows, *, tile, heads, scale):
    u = u_ref[...]
    row = lax.broadcasted_iota(jnp.int32, (tile, tile), 0)
    col = lax.broadcasted_iota(jnp.int32, (tile, tile), 1)
    causal = col < row

    def tile_terms(h, j, diagonal):
        lanes = slice(h * HEAD_DIM, (h + 1) * HEAD_DIM)
        keys = pl.ds(pl.multiple_of(j * tile, tile), tile)
        s = lax.dot_general(q_ref[q_rows, lanes], k_ref[keys, lanes], NT_DIMS,
                            preferred_element_type=F32)
        e = jnp.exp2(jnp.abs(s) * (-scale * LOG2E))
        ln = jnp.maximum(s, 0.0) * (-scale) - jnp.log(1.0 + e)
        if diagonal:
            ln = jnp.where(causal, ln, 0.0)
        hi, lo = _split_bf16(ln)
        incl = jnp.dot(jnp.concatenate([hi, lo], axis=1), u, preferred_element_type=F32)
        a = jnp.exp(s * scale + incl)
        if diagonal:
            a = jnp.where(causal, a, 0.0)
        pv = jnp.dot(a.astype(BF16), v_ref[keys, lanes], preferred_element_type=F32)
        return pv, incl[:, 0:1]

    has_prev = i > 0
    jp = jnp.maximum(i - 1, 0)
    carries, accs = [], []
    for h in range(heads):
        pv_d, tot_d = tile_terms(h, i, True)
        pv_p, tot_p = tile_terms(h, jp, False)
        accs.append(pv_d + jnp.where(has_prev, jnp.exp(tot_d), 0.0) * pv_p)
        carries.append(tot_d + jnp.where(has_prev, tot_p, 0.0))

    tops = [jnp.max(c) for c in carries]
    for h in range(heads):
        def live(state):
            j, top, _, _ = state
            return (j >= 0) & (top >= SKIP_BELOW)

        def step(state, h=h):
            j, _, carry, acc = state
            pv, tot = tile_terms(h, j, False)
            carry_new = carry + tot
            return j - 1, jnp.max(carry_new), carry_new, acc + jnp.exp(carry) * pv

        _, _, _, acc = lax.while_loop(live, step, (i - 2, tops[h], carries[h], accs[h]))
        o_ref[q_rows, h * HEAD_DIM:(h + 1) * HEAD_DIM] = acc.astype(o_ref.dtype)


def _sb_attention(p, tile=256, heads=8, tiles_per_step=4):
    B, T, _ = p.shape
    idx = jnp.arange(tile)
    u1 = (idx[:, None] >= idx[None, :]).astype(BF16)
    u = jnp.concatenate([u1, u1], axis=0)
    kern = functools.partial(_sb_kernel, tile=tile, heads=heads,
                             scale=1.0 / math.sqrt(HEAD_DIM))
    w = heads * HEAD_DIM
    groups = SB_HEADS // heads
    tq = tile * tiles_per_step
    return pl.pallas_call(
        kern,
        grid=(B, groups, T // tq),
        in_specs=[
            pl.BlockSpec((None, tq, w), lambda b, g, i: (b, i, SB_Q_BLK // heads + g)),
            pl.BlockSpec((None, T, w), lambda b, g, i: (b, 0, SB_K_BLK // heads + g)),
            pl.BlockSpec((None, T, w), lambda b, g, i: (b, 0, SB_V_BLK // heads + g)),
            pl.BlockSpec((2 * tile, tile), lambda b, g, i: (0, 0)),
        ],
        out_specs=pl.BlockSpec((None, tq, w), lambda b, g, i: (b, i, g)),
        out_shape=jax.ShapeDtypeStruct((B, T, SB_HEADS * HEAD_DIM), BF16),
        compiler_params=_params(("parallel", "parallel", "arbitrary")),
        name="sb_attention",
    )(p, p, p, u)


def _hgrn_kernel(hq_ref, hf_ref, hi_ref, hg_ref, lb_ref, gn_ref, tri_ref, *rest,
                 tb, chunk, heads, n_cast):
    cast_src, (o_ref, *cast_dst), st_ref = rest[:n_cast], rest[n_cast:-1], rest[-1]
    for src, dst in zip(cast_src, cast_dst):
        dst[...] = src[...].astype(dst.dtype)

    @pl.when(pl.program_id(2) == 0)
    def _():
        st_ref[...] = jnp.zeros_like(st_ref)

    def body(sub, carry):
        rows = pl.ds(pl.multiple_of(sub * tb, tb), tb)
        _hgrn_block(hq_ref, hf_ref, hi_ref, hg_ref, lb_ref, gn_ref, tri_ref, o_ref, st_ref,
                    rows, tb=tb, chunk=chunk, heads=heads)
        return carry

    lax.fori_loop(0, o_ref.shape[0] // tb, body, 0)


def _hgrn_block(hq_ref, hf_ref, hi_ref, hg_ref, lb_ref, gn_ref, tri_ref, o_ref, st_ref,
                rows, *, tb, chunk, heads):
    hf = hf_ref[rows, :].astype(F32)
    lb = lb_ref[0:1, :]
    one_m_lb = lb_ref[1:2, :]

    e = jnp.exp(-jnp.abs(hf))
    r = 1.0 / (1.0 + e)
    er = e * r
    pos = hf >= 0.0
    logf = jnp.log(lb + one_m_lb * jnp.where(pos, r, er))
    kk_all = one_m_lb * jnp.where(pos, er, r)
    qq = _silu(hq_ref[rows, :].astype(F32))

    hi, lo = _split_bf16(logf)
    g_all = jnp.dot(tri_ref[...], jnp.concatenate([hi, lo], axis=0),
                    preferred_element_type=F32)
    qg_all = (qq * jnp.exp(g_all)).astype(BF16)
    kg_all = (kk_all * jnp.exp(-g_all)).astype(BF16)

    row = lax.broadcasted_iota(jnp.int32, (tb, tb), 0)
    col = lax.broadcasted_iota(jnp.int32, (tb, tb), 1)
    keep = (col <= row) & ((row // chunk) == (col // chunk))
    chunks = [slice(n * chunk, (n + 1) * chunk) for n in range(tb // chunk)]

    for h in range(heads):
        lanes = slice(h * HEAD_DIM, (h + 1) * HEAD_DIM)
        g, kk, qg = g_all[:, lanes], kk_all[:, lanes], qg_all[:, lanes]
        v = hi_ref[rows, lanes]

        scores = lax.dot_general(qg, kg_all[:, lanes], NT_DIMS, preferred_element_type=F32)
        scores = jnp.where(keep, scores, 0.0).astype(BF16)
        o = jnp.dot(scores, v, preferred_element_type=F32)

        upds, decays = [], []
        for sl in chunks:
            g_end = g[sl.stop - 1:sl.stop, :]
            kdec = (kk[sl] * jnp.exp(g_end - g[sl])).astype(BF16)
            upds.append(lax.dot_general(v[sl], kdec, TN_DIMS, preferred_element_type=F32))
            decays.append(jnp.exp(g_end))

        st = st_ref[h]
        inter = []
        for sl, upd, dec in zip(chunks, upds, decays):
            inter.append(lax.dot_general(qg[sl], st.astype(BF16), NT_DIMS,
                                         preferred_element_type=F32))
            st = dec * st + upd
        st_ref[h] = st

        o = o + jnp.concatenate(inter, axis=0)
        y = _rms(o, gn_ref[:, lanes]) * _silu(hg_ref[rows, lanes].astype(F32))
        o_ref[rows, lanes] = y.astype(o_ref.dtype)


def _hgrn(p, lb_rows, gn, casts, tb=256, heads=8, blocks_per_step=2):
    B, T, _ = p.shape
    c = HG_CHUNK
    idx = jnp.arange(tb)
    same = (idx[:, None] // c) == (idx[None, :] // c)
    tri1 = (same & (idx[None, :] <= idx[:, None])).astype(BF16)
    tri = jnp.concatenate([tri1, tri1], axis=1)
    kern = functools.partial(_hgrn_kernel, tb=tb, chunk=c, heads=heads, n_cast=len(casts))
    w = heads * HEAD_DIM
    groups = HG_HEADS // heads
    ts = tb * blocks_per_step
    grid = (B, groups, T // ts)
    n_steps = grid[0] * grid[1] * grid[2]

    def col(blk):
        return pl.BlockSpec((None, ts, w), lambda b, g, i: (b, i, blk // heads + g))

    def step(b, g, i):
        return (b * grid[1] + g) * grid[2] + i

    cast_in, cast_out, cast_shapes = [], [], []
    for stack, layer in casts:
        _, R, C = stack.shape
        rows = R // n_steps
        cast_in.append(pl.BlockSpec((None, rows, C),
                                    lambda b, g, i, layer=layer: (layer, step(b, g, i), 0)))
        cast_out.append(pl.BlockSpec((rows, C), lambda b, g, i: (step(b, g, i), 0)))
        cast_shapes.append(jax.ShapeDtypeStruct((R, C), BF16))

    return pl.pallas_call(
        kern,
        grid=grid,
        in_specs=[
            col(HG_Q_BLK), col(HG_F_BLK), col(HG_I_BLK), col(HG_G_BLK),
            pl.BlockSpec((2, w), lambda b, g, i: (0, g)),
            pl.BlockSpec((1, w), lambda b, g, i: (0, g)),
            pl.BlockSpec((tb, 2 * tb), lambda b, g, i: (0, 0)),
        ] + cast_in,
        out_specs=[pl.BlockSpec((None, ts, w), lambda b, g, i: (b, i, g))] + cast_out,
        out_shape=[jax.ShapeDtypeStruct((B, T, HG_HEADS * HEAD_DIM), BF16)] + cast_shapes,
        scratch_shapes=[pltpu.VMEM((heads, HEAD_DIM, HEAD_DIM), F32)],
        compiler_params=_params(("parallel", "parallel", "arbitrary")),
        name="hgrn2",
    )(p, p, p, p, lb_rows, gn, tri, *[stack for stack, _ in casts])


def _mix_out_kernel(ysb_ref, yhg_ref, ga0_ref, ga1_ref, gb0_ref, gb1_ref,
                    wsb_ref, whg_ref, wout_ref, x_ref, g_ref, cond_ref, o_ref):
    ysb = ysb_ref[...]
    yhg = yhg_ref[...]
    half = ga0_ref.shape[-1]
    mo = None
    for c, (ga_ref, gb_ref) in enumerate(((ga0_ref, gb0_ref), (ga1_ref, gb1_ref))):
        cols = slice(c * half, (c + 1) * half)
        a = jnp.dot(ysb, wsb_ref[:, cols], preferred_element_type=F32)
        b = jnp.dot(yhg, whg_ref[:, cols], preferred_element_type=F32)
        m = (jax.nn.sigmoid(ga_ref[...].astype(F32)) * a
             + jax.nn.sigmoid(gb_ref[...].astype(F32)) * b)
        part = jnp.dot(m.astype(BF16), wout_ref[cols, :], preferred_element_type=F32)
        mo = part if mo is None else mo + part
    o_ref[...] = x_ref[...] + cond_ref[2:3, :] * _rms(mo, g_ref[...])


def _mix_out(y_sb, y_hg, p, w_sb, w_hg, w_out, x, g, cond, tm=512):
    B, T, D = x.shape
    W = y_sb.shape[-1]
    half = D // 2
    ga_blk, gb_blk = GATE_A_COL // half, GATE_B_COL // half

    def gate(blk):
        return pl.BlockSpec((None, tm, half), lambda b, i: (b, i, blk))

    def resident(shape):
        return pl.BlockSpec(shape, lambda b, i: (0, 0), pipeline_mode=pl.Buffered(1))

    return pl.pallas_call(
        _mix_out_kernel,
        grid=(B, T // tm),
        in_specs=[
            pl.BlockSpec((None, tm, W), lambda b, i: (b, i, 0)),
            pl.BlockSpec((None, tm, W), lambda b, i: (b, i, 0)),
            gate(ga_blk), gate(ga_blk + 1), gate(gb_blk), gate(gb_blk + 1),
            resident((W, D)), resident((W, D)), resident((D, D)),
            pl.BlockSpec((None, tm, D), lambda b, i: (b, i, 0)),
            pl.BlockSpec((1, D), lambda b, i: (0, 0)),
            pl.BlockSpec((None, 6, D), lambda b, i: (b, 0, 0)),
        ],
        out_specs=pl.BlockSpec((None, tm, D), lambda b, i: (b, i, 0)),
        out_shape=jax.ShapeDtypeStruct((B, T, D), F32),
        compiler_params=_params(("parallel", "parallel")),
        name="mix_out",
    )(y_sb, y_hg, p, p, p, p, w_sb, w_hg, w_out, x, g, cond)


def _mlp_kernel(x_ref, g1_ref, g2_ref, cond_ref, wup_ref, wdn_ref, o_ref, h_ref, *, sub):
    j = pl.program_id(2)
    half = h_ref.shape[0] // 2

    def accumulate(rows, first):
        for c in range(wup_ref.shape[1] // sub):
            cols = slice(c * sub, (c + 1) * sub)
            u = jnp.maximum(jnp.dot(h_ref[rows, :], wup_ref[:, cols],
                                    preferred_element_type=F32), 0.0)
            part = jnp.dot((u * u).astype(BF16), wdn_ref[cols, :], preferred_element_type=F32)
            if first and c == 0:
                o_ref[rows, :] = part
            else:
                o_ref[rows, :] += part

    @pl.when(j == 0)
    def _():
        shift, scale = cond_ref[3:4, :], cond_ref[4:5, :]
        _norm_modulate_to(h_ref, x_ref, g1_ref, shift, scale, first_row=0, n_rows=half)
        accumulate(slice(0, half), True)
        _norm_modulate_to(h_ref, x_ref, g1_ref, shift, scale, first_row=half, n_rows=half,
                          inline=True)
        accumulate(slice(half, 2 * half), True)

    @pl.when(j > 0)
    def _():
        accumulate(slice(None), False)

    @pl.when(j == pl.num_programs(2) - 1)
    def _():
        gate = cond_ref[5:6, :]
        g2 = g2_ref[...]

        for r in range(0, o_ref.shape[0], 32):
            rows = slice(r, r + 32)
            o_ref[rows, :] = x_ref[rows, :] + gate * _rms(o_ref[rows, :], g2)


def _mlp(x, g1, g2, cond, w_up, w_dn, tm=512, tf=2048, sub=1024):
    B, T, D = x.shape
    F = w_up.shape[1]
    return pl.pallas_call(
        functools.partial(_mlp_kernel, sub=sub),
        grid=(B, T // tm, F // tf),
        in_specs=[
            pl.BlockSpec((None, tm, D), lambda b, i, j: (b, i, 0)),
            pl.BlockSpec((1, D), lambda b, i, j: (0, 0)),
            pl.BlockSpec((1, D), lambda b, i, j: (0, 0)),
            pl.BlockSpec((None, 6, D), lambda b, i, j: (b, 0, 0)),
            pl.BlockSpec((D, tf), lambda b, i, j: (0, j)),
            pl.BlockSpec((tf, D), lambda b, i, j: (j, 0)),
        ],
        out_specs=pl.BlockSpec((None, tm, D), lambda b, i, j: (b, i, 0)),
        out_shape=jax.ShapeDtypeStruct((B, T, D), F32),
        scratch_shapes=[pltpu.VMEM((tm, D), BF16)],
        compiler_params=_params(("parallel", "parallel", "arbitrary")),
        name="mlp",
    )(x, g1, g2, cond, w_up, w_dn)


def kernel(x, c, w_ada, b_ada, g_pre_mix, g_post_mix, w_in, hg_lb_logits, g_hg_norm,
           w_proj_sb, w_proj_hg, w_out, g_pre_mlp, g_post_mlp, w_mlp_up, w_mlp_down):
    B, T, D = x.shape
    L = w_ada.shape[0]

    lbs = jnp.cumsum(jax.nn.softmax(hg_lb_logits.astype(F32), axis=0), axis=0)
    lbs = lbs - lbs[0:1]
    lb_rows = jnp.stack([lbs, 1.0 - lbs], axis=1)

    c_pad = jnp.pad(c, ((0, 8 - B), (0, 0)))
    cond_all = _ada_cond(c_pad, w_ada, b_ada)[:, :B, :].reshape(L, B, 6, D)

    w_in_l = w_in[0].astype(BF16)

    for l in range(L):
        cond = cond_all[l]
        p = _in_proj(x, g_pre_mix[l].reshape(1, D), cond, w_in_l)
        y_sb = _sb_attention(p)
        casts = [(w, l) for w in (w_proj_sb, w_proj_hg, w_out, w_mlp_up, w_mlp_down)]
        casts += [(w_in, l + 1)] if l + 1 < L else []
        y_hg, w_sb_l, w_hg_l, w_out_l, w_up_l, w_dn_l, *w_in_next = _hgrn(
            p, lb_rows[l], g_hg_norm[l].reshape(1, -1), casts)
        x = _mix_out(y_sb, y_hg, p, w_sb_l, w_hg_l, w_out_l,
                     x, g_post_mix[l].reshape(1, D), cond)
        x = _mlp(x, g_pre_mlp[l].reshape(1, D), g_post_mlp[l].reshape(1, D), cond,
                 w_up_l, w_dn_l)
        if w_in_next:
            w_in_l = w_in_next[0]
    return x
```

```python
import functools
import math

import jax
import jax.numpy as jnp
from jax import lax
from jax.experimental import pallas as pl
from jax.experimental.pallas import tpu as pltpu

F32 = jnp.float32
BF16 = jnp.bfloat16

EPS = 1e-6
HEAD_DIM = 128
SB_HEADS = 8
HG_HEADS = 8
HG_CHUNK = 32

SB_Q_BLK, SB_K_BLK, SB_V_BLK = 0, 8, 16
HG_Q_BLK, HG_F_BLK, HG_I_BLK, HG_G_BLK = 24, 32, 40, 48
GATE_A_COL, GATE_B_COL = 7168, 9216

VMEM_LIMIT = 56 * 1024 * 1024

SKIP_BELOW = -104.0
LOG2E = 1.4426950408889634

NT_DIMS = (((1,), (1,)), ((), ()))
TN_DIMS = (((0,), (0,)), ((), ()))


def _params(sem):
    return pltpu.CompilerParams(dimension_semantics=sem, vmem_limit_bytes=VMEM_LIMIT)


def _rms(xf, g):
    ms = jnp.mean(xf * xf, axis=-1, keepdims=True)
    return xf * lax.rsqrt(ms + EPS) * g


def _silu(x):
    hx = 0.5 * x
    return hx + hx * jnp.tanh(hx)


def _for_row_chunks(n_rows, chunk_rows, fn, first_row=0):
    def body(r, carry):
        fn(pl.ds(pl.multiple_of(first_row + r * chunk_rows, chunk_rows), chunk_rows))
        return carry
    lax.fori_loop(0, n_rows // chunk_rows, body, 0, unroll=4)


def _norm_modulate_to(h_ref, x_ref, g_ref, shift, scale, first_row=0, n_rows=None,
                      inline=False):
    gmod = g_ref[...] * (1.0 + scale)
    n_rows = x_ref.shape[0] if n_rows is None else n_rows

    def chunk(rows):
        x = x_ref[rows, :]
        ms = jnp.mean(x * x, axis=-1, keepdims=True)
        h_ref[rows, :] = (x * lax.rsqrt(ms + EPS) * gmod + shift).astype(h_ref.dtype)

    if inline:
        for r in range(first_row, first_row + n_rows, 32):
            chunk(slice(r, r + 32))
    else:
        _for_row_chunks(n_rows, 32, chunk, first_row)


def _split_bf16(a):
    hi = a.astype(BF16)
    lo = (a - hi.astype(F32)).astype(BF16)
    return hi, lo


def _ada_kernel(c_ref, w_ref, b_ref, o_ref):
    c = c_ref[...]
    ca = (c * jax.nn.sigmoid(c)).astype(BF16)
    o_ref[...] = jnp.dot(ca, w_ref[...].astype(BF16),
                         preferred_element_type=F32) + b_ref[...]


def _ada_cond(c_pad, w_ada, b_ada, tn=1024):
    L, D, N = w_ada.shape
    M = c_pad.shape[0]
    return pl.pallas_call(
        _ada_kernel,
        grid=(L, N // tn),
        in_specs=[
            pl.BlockSpec((M, D), lambda l, j: (0, 0)),
            pl.BlockSpec((None, D, tn), lambda l, j: (l, 0, j)),
            pl.BlockSpec((None, 1, tn), lambda l, j: (l, 0, j)),
        ],
        out_specs=pl.BlockSpec((None, M, tn), lambda l, j: (l, 0, j)),
        out_shape=jax.ShapeDtypeStruct((L, M, N), F32),
        compiler_params=_params(("parallel", "parallel")),
        name="ada_cond",
    )(c_pad, w_ada, b_ada.reshape(L, 1, N))


def _in_proj_kernel(x_ref, g_ref, cond_ref, w_ref, o_ref, h_ref):
    half = h_ref.shape[0] // 2

    def project(rows):
        o_ref[rows, :] = jnp.dot(h_ref[rows, :], w_ref[...],
                                 preferred_element_type=F32).astype(o_ref.dtype)

    @pl.when(pl.program_id(2) == 0)
    def _():
        _norm_modulate_to(h_ref, x_ref, g_ref, cond_ref[0:1, :], cond_ref[1:2, :],
                          first_row=0, n_rows=half)
        project(slice(0, half))
        _norm_modulate_to(h_ref, x_ref, g_ref, cond_ref[0:1, :], cond_ref[1:2, :],
                          first_row=half, n_rows=half, inline=True)
        project(slice(half, 2 * half))

    @pl.when(pl.program_id(2) > 0)
    def _():
        project(slice(None))


def _in_proj(x, g, cond, w, tm=512, tn=2816):
    B, T, D = x.shape
    N = w.shape[1]
    return pl.pallas_call(
        _in_proj_kernel,
        grid=(B, T // tm, N // tn),
        in_specs=[
            pl.BlockSpec((None, tm, D), lambda b, i, j: (b, i, 0)),
            pl.BlockSpec((1, D), lambda b, i, j: (0, 0)),
            pl.BlockSpec((None, 6, D), lambda b, i, j: (b, 0, 0)),
            pl.BlockSpec((D, tn), lambda b, i, j: (0, j)),
        ],
        out_specs=pl.BlockSpec((None, tm, tn), lambda b, i, j: (b, i, j)),
        out_shape=jax.ShapeDtypeStruct((B, T, N), BF16),
        scratch_shapes=[pltpu.VMEM((tm, D), BF16)],
        compiler_params=_params(("parallel", "parallel", "arbitrary")),
        name="in_proj",
    )(x, g, cond, w)


def _sb_kernel(q_ref, k_ref, v_ref, u_ref, o_ref, *, tile, heads, scale):
    tiles_per_step = q_ref.shape[0] // tile

    def body(sub, carry):
        _sb_query_tile(q_ref, k_ref, v_ref, u_ref, o_ref,
                       pl.program_id(2) * tiles_per_step + sub,
                       pl.ds(pl.multiple_of(sub * tile, tile), tile),
                       tile=tile, heads=heads, scale=scale)
        return carry

    lax.fori_loop(0, tiles_per_step, body, 0)


def _sb_query_tile(q_ref, k_ref, v_ref, u_ref, o_ref, i, q_rows, *, tile, heads, scale):
    u = u_ref[...]
    row = lax.broadcasted_iota(jnp.int32, (tile, tile), 0)
    col = lax.broadcasted_iota(jnp.int32, (tile, tile), 1)
    causal = col < row

    def tile_terms(h, j, diagonal):
        lanes = slice(h * HEAD_DIM, (h + 1) * HEAD_DIM)
        keys = pl.ds(pl.multiple_of(j * tile, tile), tile)
        s = lax.dot_general(q_ref[q_rows, lanes], k_ref[keys, lanes], NT_DIMS,
                            preferred_element_type=F32)
        e = jnp.exp2(jnp.abs(s) * (-scale * LOG2E))
        ln = jnp.maximum(s, 0.0) * (-scale) - jnp.log(1.0 + e)
        if diagonal:
            ln = jnp.where(causal, ln, 0.0)
        hi, lo = _split_bf16(ln)
        incl = jnp.dot(jnp.concatenate([hi, lo], axis=1), u, preferred_element_type=F32)
        a = jnp.exp(s * scale + incl)
        if diagonal:
            a = jnp.where(causal, a, 0.0)
        pv = jnp.dot(a.astype(BF16), v_ref[keys, lanes], preferred_element_type=F32)
        return pv, incl[:, 0:1]

    has_prev = i > 0
    jp = jnp.maximum(i - 1, 0)
    carries, accs = [], []
    for h in range(heads):
        pv_d, tot_d = tile_terms(h, i, True)
        pv_p, tot_p = tile_terms(h, jp, False)
        accs.append(pv_d + jnp.where(has_prev, jnp.exp(tot_d), 0.0) * pv_p)
        carries.append(tot_d + jnp.where(has_prev, tot_p, 0.0))

    tops = [jnp.max(c) for c in carries]
    for h in range(heads):
        def live(state):
            j, top, _, _ = state
            return (j >= 0) & (top >= SKIP_BELOW)

        def step(state, h=h):
            j, _, carry, acc = state
            pv, tot = tile_terms(h, j, False)
            carry_new = carry + tot
            return j - 1, jnp.max(carry_new), carry_new, acc + jnp.exp(carry) * pv

        _, _, _, acc = lax.while_loop(live, step, (i - 2, tops[h], carries[h], accs[h]))
        o_ref[q_rows, h * HEAD_DIM:(h + 1) * HEAD_DIM] = acc.astype(o_ref.dtype)


def _sb_attention(p, tile=256, heads=8, tiles_per_step=4):
    B, T, _ = p.shape
    idx = jnp.arange(tile)
    u1 = (idx[:, None] >= idx[None, :]).astype(BF16)
    u = jnp.concatenate([u1, u1], axis=0)
    kern = functools.partial(_sb_kernel, tile=tile, heads=heads,
                             scale=1.0 / math.sqrt(HEAD_DIM))
    w = heads * HEAD_DIM
    groups = SB_HEADS // heads
    tq = tile * tiles_per_step
    return pl.pallas_call(
        kern,
        grid=(B, groups, T // tq),
        in_specs=[
            pl.BlockSpec((None, tq, w), lambda b, g, i: (b, i, SB_Q_BLK // heads + g)),
            pl.BlockSpec((None, T, w), lambda b, g, i: (b, 0, SB_K_BLK // heads + g)),
            pl.BlockSpec((None, T, w), lambda b, g, i: (b, 0, SB_V_BLK // heads + g)),
            pl.BlockSpec((2 * tile, tile), lambda b, g, i: (0, 0)),
        ],
        out_specs=pl.BlockSpec((None, tq, w), lambda b, g, i: (b, i, g)),
        out_shape=jax.ShapeDtypeStruct((B, T, SB_HEADS * HEAD_DIM), BF16),
        compiler_params=_params(("parallel", "parallel", "arbitrary")),
        name="sb_attention",
    )(p, p, p, u)


def _hgrn_kernel(hq_ref, hf_ref, hi_ref, hg_ref, lb_ref, gn_ref, tri_ref, *rest,
                 tb, chunk, heads, n_cast):
    cast_src, (o_ref, *cast_dst), st_ref = rest[:n_cast], rest[n_cast:-1], rest[-1]
    for src, dst in zip(cast_src, cast_dst):
        dst[...] = src[...].astype(dst.dtype)

    @pl.when(pl.program_id(2) == 0)
    def _():
        st_ref[...] = jnp.zeros_like(st_ref)

    def body(sub, carry):
        rows = pl.ds(pl.multiple_of(sub * tb, tb), tb)
        _hgrn_block(hq_ref, hf_ref, hi_ref, hg_ref, lb_ref, gn_ref, tri_ref, o_ref, st_ref,
                    rows, tb=tb, chunk=chunk, heads=heads)
        return carry

    lax.fori_loop(0, o_ref.shape[0] // tb, body, 0)


def _hgrn_block(hq_ref, hf_ref, hi_ref, hg_ref, lb_ref, gn_ref, tri_ref, o_ref, st_ref,
                rows, *, tb, chunk, heads):
    hf = hf_ref[rows, :].astype(F32)
    lb = lb_ref[0:1, :]
    one_m_lb = lb_ref[1:2, :]

    e = jnp.exp(-jnp.abs(hf))
    r = 1.0 / (1.0 + e)
    er = e * r
    pos = hf >= 0.0
    logf = jnp.log(lb + one_m_lb * jnp.where(pos, r, er))
    kk_all = one_m_lb * jnp.where(pos, er, r)
    qq = _silu(hq_ref[rows, :].astype(F32))

    hi, lo = _split_bf16(logf)
    g_all = jnp.dot(tri_ref[...], jnp.concatenate([hi, lo], axis=0),
                    preferred_element_type=F32)
    qg_all = (qq * jnp.exp(g_all)).astype(BF16)
    kg_all = (kk_all * jnp.exp(-g_all)).astype(BF16)

    row = lax.broadcasted_iota(jnp.int32, (tb, tb), 0)
    col = lax.broadcasted_iota(jnp.int32, (tb, tb), 1)
    keep = (col <= row) & ((row // chunk) == (col // chunk))
    chunks = [slice(n * chunk, (n + 1) * chunk) for n in range(tb // chunk)]

    for h in range(heads):
        lanes = slice(h * HEAD_DIM, (h + 1) * HEAD_DIM)
        g, kk, qg = g_all[:, lanes], kk_all[:, lanes], qg_all[:, lanes]
        v = hi_ref[rows, lanes]

        scores = lax.dot_general(qg, kg_all[:, lanes], NT_DIMS, preferred_element_type=F32)
        scores = jnp.where(keep, scores, 0.0).astype(BF16)
        o = jnp.dot(scores, v, preferred_element_type=F32)

        upds, decays = [], []
        for sl in chunks:
            g_end = g[sl.stop - 1:sl.stop, :]
            kdec = (kk[sl] * jnp.exp(g_end - g[sl])).astype(BF16)
            upds.append(lax.dot_general(v[sl], kdec, TN_DIMS, preferred_element_type=F32))
            decays.append(jnp.exp(g_end))

        st = st_ref[h]
        inter = []
        for sl, upd, dec in zip(chunks, upds, decays):
            inter.append(lax.dot_general(qg[sl], st.astype(BF16), NT_DIMS,
                                         preferred_element_type=F32))
            st = dec * st + upd
        st_ref[h] = st

        o = o + jnp.concatenate(inter, axis=0)
        y = _rms(o, gn_ref[:, lanes]) * _silu(hg_ref[rows, lanes].astype(F32))
        o_ref[rows, lanes] = y.astype(o_ref.dtype)


def _hgrn(p, lb_rows, gn, casts, tb=256, heads=8, blocks_per_step=2):
    B, T, _ = p.shape
    c = HG_CHUNK
    idx = jnp.arange(tb)
    same = (idx[:, None] // c) == (idx[None, :] // c)
    tri1 = (same & (idx[None, :] <= idx[:, None])).astype(BF16)
    tri = jnp.concatenate([tri1, tri1], axis=1)
    kern = functools.partial(_hgrn_kernel, tb=tb, chunk=c, heads=heads, n_cast=len(casts))
    w = heads * HEAD_DIM
    groups = HG_HEADS // heads
    ts = tb * blocks_per_step
    grid = (B, groups, T // ts)
    n_steps = grid[0] * grid[1] * grid[2]

    def col(blk):
        return pl.BlockSpec((None, ts, w), lambda b, g, i: (b, i, blk // heads + g))

    def step(b, g, i):
        return (b * grid[1] + g) * grid[2] + i

    cast_in, cast_out, cast_shapes = [], [], []
    for stack, layer in casts:
        _, R, C = stack.shape
        rows = R // n_steps
        cast_in.append(pl.BlockSpec((None, rows, C),
                                    lambda b, g, i, layer=layer: (layer, step(b, g, i), 0)))
        cast_out.append(pl.BlockSpec((rows, C), lambda b, g, i: (step(b, g, i), 0)))
        cast_shapes.append(jax.ShapeDtypeStruct((R, C), BF16))

    return pl.pallas_call(
        kern,
        grid=grid,
        in_specs=[
            col(HG_Q_BLK), col(HG_F_BLK), col(HG_I_BLK), col(HG_G_BLK),
            pl.BlockSpec((2, w), lambda b, g, i: (0, g)),
            pl.BlockSpec((1, w), lambda b, g, i: (0, g)),
            pl.BlockSpec((tb, 2 * tb), lambda b, g, i: (0, 0)),
        ] + cast_in,
        out_specs=[pl.BlockSpec((None, ts, w), lambda b, g, i: (b, i, g))] + cast_out,
        out_shape=[jax.ShapeDtypeStruct((B, T, HG_HEADS * HEAD_DIM), BF16)] + cast_shapes,
        scratch_shapes=[pltpu.VMEM((heads, HEAD_DIM, HEAD_DIM), F32)],
        compiler_params=_params(("parallel", "parallel", "arbitrary")),
        name="hgrn2",
    )(p, p, p, p, lb_rows, gn, tri, *[stack for stack, _ in casts])


def _mix_out_kernel(ysb_ref, yhg_ref, ga0_ref, ga1_ref, gb0_ref, gb1_ref,
                    wsb_ref, whg_ref, wout_ref, x_ref, g_ref, cond_ref, o_ref):
    ysb = ysb_ref[...]
    yhg = yhg_ref[...]
    half = ga0_ref.shape[-1]
    mo = None
    for c, (ga_ref, gb_ref) in enumerate(((ga0_ref, gb0_ref), (ga1_ref, gb1_ref))):
        cols = slice(c * half, (c + 1) * half)
        a = jnp.dot(ysb, wsb_ref[:, cols], preferred_element_type=F32)
        b = jnp.dot(yhg, whg_ref[:, cols], preferred_element_type=F32)
        m = (jax.nn.sigmoid(ga_ref[...].astype(F32)) * a
             + jax.nn.sigmoid(gb_ref[...].astype(F32)) * b)
        part = jnp.dot(m.astype(BF16), wout_ref[cols, :], preferred_element_type=F32)
        mo = part if mo is None else mo + part
    o_ref[...] = x_ref[...] + cond_ref[2:3, :] * _rms(mo, g_ref[...])


def _mix_out(y_sb, y_hg, p, w_sb, w_hg, w_out, x, g, cond, tm=512):
    B, T, D = x.shape
    W = y_sb.shape[-1]
    half = D // 2
    ga_blk, gb_blk = GATE_A_COL // half, GATE_B_COL // half

    def gate(blk):
        return pl.BlockSpec((None, tm, half), lambda b, i: (b, i, blk))

    def resident(shape):
        return pl.BlockSpec(shape, lambda b, i: (0, 0), pipeline_mode=pl.Buffered(1))

    return pl.pallas_call(
        _mix_out_kernel,
        grid=(B, T // tm),
        in_specs=[
            pl.BlockSpec((None, tm, W), lambda b, i: (b, i, 0)),
            pl.BlockSpec((None, tm, W), lambda b, i: (b, i, 0)),
            gate(ga_blk), gate(ga_blk + 1), gate(gb_blk), gate(gb_blk + 1),
            resident((W, D)), resident((W, D)), resident((D, D)),
            pl.BlockSpec((None, tm, D), lambda b, i: (b, i, 0)),
            pl.BlockSpec((1, D), lambda b, i: (0, 0)),
            pl.BlockSpec((None, 6, D), lambda b, i: (b, 0, 0)),
        ],
        out_specs=pl.BlockSpec((None, tm, D), lambda b, i: (b, i, 0)),
        out_shape=jax.ShapeDtypeStruct((B, T, D), F32),
        compiler_params=_params(("parallel", "parallel")),
        name="mix_out",
    )(y_sb, y_hg, p, p, p, p, w_sb, w_hg, w_out, x, g, cond)


def _mlp_kernel(x_ref, g1_ref, g2_ref, cond_ref, wup_ref, wdn_ref, o_ref, h_ref, *, sub):
    j = pl.program_id(2)
    half = h_ref.shape[0] // 2

    def accumulate(rows, first):
        for c in range(wup_ref.shape[1] // sub):
            cols = slice(c * sub, (c + 1) * sub)
            u = jnp.maximum(jnp.dot(h_ref[rows, :], wup_ref[:, cols],
                                    preferred_element_type=F32), 0.0)
            part = jnp.dot((u * u).astype(BF16), wdn_ref[cols, :], preferred_element_type=F32)
            if first and c == 0:
                o_ref[rows, :] = part
            else:
                o_ref[rows, :] += part

    @pl.when(j == 0)
    def _():
        shift, scale = cond_ref[3:4, :], cond_ref[4:5, :]
        _norm_modulate_to(h_ref, x_ref, g1_ref, shift, scale, first_row=0, n_rows=half)
        accumulate(slice(0, half), True)
        _norm_modulate_to(h_ref, x_ref, g1_ref, shift, scale, first_row=half, n_rows=half,
                          inline=True)
        accumulate(slice(half, 2 * half), True)

    last = pl.num_programs(2) - 1

    @pl.when((j > 0) & (j < last))
    def _():
        accumulate(slice(None), False)

    def finish(first_row):
        gate = cond_ref[5:6, :]
        g2 = g2_ref[...]
        for r in range(first_row, first_row + half, 32):
            rows = slice(r, r + 32)
            o_ref[rows, :] = x_ref[rows, :] + gate * _rms(o_ref[rows, :], g2)

    @pl.when(j == last)
    def _():
        accumulate(slice(0, half), False)
        finish(0)
        second = pl.multiple_of(jnp.minimum(j, 1) * half, half)
        accumulate(pl.ds(second, half), False)
        finish(half)


def _mlp(x, g1, g2, cond, w_up, w_dn, tm=512, tf=2048, sub=1024):
    B, T, D = x.shape
    F = w_up.shape[1]
    return pl.pallas_call(
        functools.partial(_mlp_kernel, sub=sub),
        grid=(B, T // tm, F // tf),
        in_specs=[
            pl.BlockSpec((None, tm, D), lambda b, i, j: (b, i, 0)),
            pl.BlockSpec((1, D), lambda b, i, j: (0, 0)),
            pl.BlockSpec((1, D), lambda b, i, j: (0, 0)),
            pl.BlockSpec((None, 6, D), lambda b, i, j: (b, 0, 0)),
            pl.BlockSpec((D, tf), lambda b, i, j: (0, j)),
            pl.BlockSpec((tf, D), lambda b, i, j: (j, 0)),
        ],
        out_specs=pl.BlockSpec((None, tm, D), lambda b, i, j: (b, i, 0)),
        out_shape=jax.ShapeDtypeStruct((B, T, D), F32),
        scratch_shapes=[pltpu.VMEM((tm, D), BF16)],
        compiler_params=_params(("parallel", "parallel", "arbitrary")),
        name="mlp",
    )(x, g1, g2, cond, w_up, w_dn)


def kernel(x, c, w_ada, b_ada, g_pre_mix, g_post_mix, w_in, hg_lb_logits, g_hg_norm,
           w_proj_sb, w_proj_hg, w_out, g_pre_mlp, g_post_mlp, w_mlp_up, w_mlp_down):
    B, T, D = x.shape
    L = w_ada.shape[0]

    lbs = jnp.cumsum(jax.nn.softmax(hg_lb_logits.astype(F32), axis=0), axis=0)
    lbs = lbs - lbs[0:1]
    lb_rows = jnp.stack([lbs, 1.0 - lbs], axis=1)

    c_pad = jnp.pad(c, ((0, 8 - B), (0, 0)))
    cond_all = _ada_cond(c_pad, w_ada, b_ada)[:, :B, :].reshape(L, B, 6, D)

    w_in_l = w_in[0].astype(BF16)

    for l in range(L):
        cond = cond_all[l]
        p = _in_proj(x, g_pre_mix[l].reshape(1, D), cond, w_in_l)
        y_sb = _sb_attention(p)
        casts = [(w, l) for w in (w_proj_sb, w_proj_hg, w_out, w_mlp_up, w_mlp_down)]
        casts += [(w_in, l + 1)] if l + 1 < L else []
        y_hg, w_sb_l, w_hg_l, w_out_l, w_up_l, w_dn_l, *w_in_next = _hgrn(
            p, lb_rows[l], g_hg_norm[l].reshape(1, -1), casts)
        x = _mix_out(y_sb, y_hg, p, w_sb_l, w_hg_l, w_out_l,
                     x, g_post_mix[l].reshape(1, D), cond)
        x = _mlp(x, g_pre_mlp[l].reshape(1, D), g_post_mlp[l].reshape(1, D), cond,
                 w_up_l, w_dn_l)
        if w_in_next:
            w_in_l = w_in_next[0]
    return x
```

```python
import functools
import math

import jax
import jax.numpy as jnp
from jax import lax
from jax.experimental import pallas as pl
from jax.experimental.pallas import tpu as pltpu

F32 = jnp.float32
BF16 = jnp.bfloat16

EPS = 1e-6
HEAD_DIM = 128
SB_HEADS = 8
HG_HEADS = 8
HG_CHUNK = 32

SB_Q_BLK, SB_K_BLK, SB_V_BLK = 0, 8, 16
HG_Q_BLK, HG_F_BLK, HG_I_BLK, HG_G_BLK = 24, 32, 40, 48
GATE_A_COL, GATE_B_COL = 7168, 9216

VMEM_LIMIT = 56 * 1024 * 1024

SKIP_BELOW = -104.0
LOG2E = 1.4426950408889634

EDGE_ROWS = 128

NT_DIMS = (((1,), (1,)), ((), ()))
TN_DIMS = (((0,), (0,)), ((), ()))


def _params(sem):
    return pltpu.CompilerParams(dimension_semantics=sem, vmem_limit_bytes=VMEM_LIMIT)


def _rms(xf, g):
    ms = jnp.mean(xf * xf, axis=-1, keepdims=True)
    return xf * lax.rsqrt(ms + EPS) * g


def _silu(x):
    hx = 0.5 * x
    return hx + hx * jnp.tanh(hx)


def _for_row_chunks(n_rows, chunk_rows, fn, first_row=0):
    def body(r, carry):
        fn(pl.ds(pl.multiple_of(first_row + r * chunk_rows, chunk_rows), chunk_rows))
        return carry
    lax.fori_loop(0, n_rows // chunk_rows, body, 0, unroll=4)


def _norm_modulate_to(h_ref, x_ref, g_ref, shift, scale, first_row=0, n_rows=None,
                      inline=False):
    gmod = g_ref[...] * (1.0 + scale)
    n_rows = x_ref.shape[0] if n_rows is None else n_rows

    def chunk(rows):
        x = x_ref[rows, :]
        ms = jnp.mean(x * x, axis=-1, keepdims=True)
        h_ref[rows, :] = (x * lax.rsqrt(ms + EPS) * gmod + shift).astype(h_ref.dtype)

    if inline:
        for r in range(first_row, first_row + n_rows, 32):
            chunk(slice(r, r + 32))
    else:
        _for_row_chunks(n_rows, 32, chunk, first_row)


def _split_bf16(a):
    hi = a.astype(BF16)
    lo = (a - hi.astype(F32)).astype(BF16)
    return hi, lo


def _ada_kernel(c_ref, w_ref, b_ref, o_ref):
    c = c_ref[...]
    ca = (c * jax.nn.sigmoid(c)).astype(BF16)
    o_ref[...] = jnp.dot(ca, w_ref[...].astype(BF16),
                         preferred_element_type=F32) + b_ref[...]


def _ada_cond(c_pad, w_ada, b_ada, tn=1024):
    L, D, N = w_ada.shape
    M = c_pad.shape[0]
    return pl.pallas_call(
        _ada_kernel,
        grid=(L, N // tn),
        in_specs=[
            pl.BlockSpec((M, D), lambda l, j: (0, 0)),
            pl.BlockSpec((None, D, tn), lambda l, j: (l, 0, j)),
            pl.BlockSpec((None, 1, tn), lambda l, j: (l, 0, j)),
        ],
        out_specs=pl.BlockSpec((None, M, tn), lambda l, j: (l, 0, j)),
        out_shape=jax.ShapeDtypeStruct((L, M, N), F32),
        compiler_params=_params(("parallel", "parallel")),
        name="ada_cond",
    )(c_pad, w_ada, b_ada.reshape(L, 1, N))


def _in_proj_kernel(x_ref, g_ref, cond_ref, w_ref, o_ref, h_ref):
    tm = h_ref.shape[0]

    def project(rows):
        o_ref[rows, :] = jnp.dot(h_ref[rows, :], w_ref[...],
                                 preferred_element_type=F32).astype(o_ref.dtype)

    @pl.when(pl.program_id(2) == 0)
    def _():
        shift, scale = cond_ref[0:1, :], cond_ref[1:2, :]
        _norm_modulate_to(h_ref, x_ref, g_ref, shift, scale, first_row=0, n_rows=EDGE_ROWS,
                          inline=True)
        project(slice(0, EDGE_ROWS))
        _norm_modulate_to(h_ref, x_ref, g_ref, shift, scale, first_row=EDGE_ROWS,
                          n_rows=tm - EDGE_ROWS, inline=True)
        project(slice(EDGE_ROWS, tm))

    @pl.when(pl.program_id(2) > 0)
    def _():
        project(slice(None))


def _in_proj(x, g, cond, w, tm=512, tn=2816):
    B, T, D = x.shape
    N = w.shape[1]
    return pl.pallas_call(
        _in_proj_kernel,
        grid=(B, T // tm, N // tn),
        in_specs=[
            pl.BlockSpec((None, tm, D), lambda b, i, j: (b, i, 0)),
            pl.BlockSpec((1, D), lambda b, i, j: (0, 0)),
            pl.BlockSpec((None, 6, D), lambda b, i, j: (b, 0, 0)),
            pl.BlockSpec((D, tn), lambda b, i, j: (0, j)),
        ],
        out_specs=pl.BlockSpec((None, tm, tn), lambda b, i, j: (b, i, j)),
        out_shape=jax.ShapeDtypeStruct((B, T, N), BF16),
        scratch_shapes=[pltpu.VMEM((tm, D), BF16)],
        compiler_params=_params(("parallel", "parallel", "arbitrary")),
        name="in_proj",
    )(x, g, cond, w)


def _sb_kernel(q_ref, k_ref, v_ref, u_ref, o_ref, *, tile, heads, scale):
    tiles_per_step = q_ref.shape[0] // tile

    def body(sub, carry):
        _sb_query_tile(q_ref, k_ref, v_ref, u_ref, o_ref,
                       pl.program_id(2) * tiles_per_step + sub,
                       pl.ds(pl.multiple_of(sub * tile, tile), tile),
                       tile=tile, heads=heads, scale=scale)
        return carry

    lax.fori_loop(0, tiles_per_step, body, 0)


def _sb_query_tile(q_ref, k_ref, v_ref, u_ref, o_ref, i, q_rows, *, tile, heads, scale):
    u = u_ref[...]
    row = lax.broadcasted_iota(jnp.int32, (tile, tile), 0)
    col = lax.broadcasted_iota(jnp.int32, (tile, tile), 1)
    causal = col < row

    def tile_terms(h, j, diagonal):
        lanes = slice(h * HEAD_DIM, (h + 1) * HEAD_DIM)
        keys = pl.ds(pl.multiple_of(j * tile, tile), tile)
        s = lax.dot_general(q_ref[q_rows, lanes], k_ref[keys, lanes], NT_DIMS,
                            preferred_element_type=F32)
        e = jnp.exp2(jnp.abs(s) * (-scale * LOG2E))
        ln = jnp.maximum(s, 0.0) * (-scale) - jnp.log(1.0 + e)
        if diagonal:
            ln = jnp.where(causal, ln, 0.0)
        hi, lo = _split_bf16(ln)
        incl = jnp.dot(jnp.concatenate([hi, lo], axis=1), u, preferred_element_type=F32)
        a = jnp.exp(s * scale + incl)
        if diagonal:
            a = jnp.where(causal, a, 0.0)
        pv = jnp.dot(a.astype(BF16), v_ref[keys, lanes], preferred_element_type=F32)
        return pv, incl[:, 0:1]

    has_prev = i > 0
    jp = jnp.maximum(i - 1, 0)
    carries, accs = [], []
    for h in range(heads):
        pv_d, tot_d = tile_terms(h, i, True)
        pv_p, tot_p = tile_terms(h, jp, False)
        accs.append(pv_d + jnp.where(has_prev, jnp.exp(tot_d), 0.0) * pv_p)
        carries.append(tot_d + jnp.where(has_prev, tot_p, 0.0))

    tops = [jnp.max(c) for c in carries]
    for h in range(heads):
        def live(state):
            j, top, _, _ = state
            return (j >= 0) & (top >= SKIP_BELOW)

        def step(state, h=h):
            j, _, carry, acc = state
            pv, tot = tile_terms(h, j, False)
            carry_new = carry + tot
            return j - 1, jnp.max(carry_new), carry_new, acc + jnp.exp(carry) * pv

        _, _, _, acc = lax.while_loop(live, step, (i - 2, tops[h], carries[h], accs[h]))
        o_ref[q_rows, h * HEAD_DIM:(h + 1) * HEAD_DIM] = acc.astype(o_ref.dtype)


def _sb_attention(p, tile=256, heads=8, tiles_per_step=4):
    B, T, _ = p.shape
    idx = jnp.arange(tile)
    u1 = (idx[:, None] >= idx[None, :]).astype(BF16)
    u = jnp.concatenate([u1, u1], axis=0)
    kern = functools.partial(_sb_kernel, tile=tile, heads=heads,
                             scale=1.0 / math.sqrt(HEAD_DIM))
    w = heads * HEAD_DIM
    groups = SB_HEADS // heads
    tq = tile * tiles_per_step
    return pl.pallas_call(
        kern,
        grid=(B, groups, T // tq),
        in_specs=[
            pl.BlockSpec((None, tq, w), lambda b, g, i: (b, i, SB_Q_BLK // heads + g)),
            pl.BlockSpec((None, T, w), lambda b, g, i: (b, 0, SB_K_BLK // heads + g)),
            pl.BlockSpec((None, T, w), lambda b, g, i: (b, 0, SB_V_BLK // heads + g)),
            pl.BlockSpec((2 * tile, tile), lambda b, g, i: (0, 0)),
        ],
        out_specs=pl.BlockSpec((None, tq, w), lambda b, g, i: (b, i, g)),
        out_shape=jax.ShapeDtypeStruct((B, T, SB_HEADS * HEAD_DIM), BF16),
        compiler_params=_params(("parallel", "parallel", "arbitrary")),
        name="sb_attention",
    )(p, p, p, u)


def _hgrn_kernel(hq_ref, hf_ref, hi_ref, hg_ref, lb_ref, gn_ref, tri_ref, *rest,
                 tb, chunk, heads, n_cast):
    cast_src, (o_ref, *cast_dst), st_ref = rest[:n_cast], rest[n_cast:-1], rest[-1]
    for src, dst in zip(cast_src, cast_dst):
        dst[...] = src[...].astype(dst.dtype)

    @pl.when(pl.program_id(2) == 0)
    def _():
        st_ref[...] = jnp.zeros_like(st_ref)

    def body(sub, carry):
        rows = pl.ds(pl.multiple_of(sub * tb, tb), tb)
        _hgrn_block(hq_ref, hf_ref, hi_ref, hg_ref, lb_ref, gn_ref, tri_ref, o_ref, st_ref,
                    rows, tb=tb, chunk=chunk, heads=heads)
        return carry

    lax.fori_loop(0, o_ref.shape[0] // tb, body, 0)


def _hgrn_block(hq_ref, hf_ref, hi_ref, hg_ref, lb_ref, gn_ref, tri_ref, o_ref, st_ref,
                rows, *, tb, chunk, heads):
    hf = hf_ref[rows, :].astype(F32)
    lb = lb_ref[0:1, :]
    one_m_lb = lb_ref[1:2, :]

    e = jnp.exp(-jnp.abs(hf))
    r = 1.0 / (1.0 + e)
    er = e * r
    pos = hf >= 0.0
    logf = jnp.log(lb + one_m_lb * jnp.where(pos, r, er))
    kk_all = one_m_lb * jnp.where(pos, er, r)
    qq = _silu(hq_ref[rows, :].astype(F32))

    hi, lo = _split_bf16(logf)
    g_all = jnp.dot(tri_ref[...], jnp.concatenate([hi, lo], axis=0),
                    preferred_element_type=F32)
    qg_all = (qq * jnp.exp(g_all)).astype(BF16)
    kg_all = (kk_all * jnp.exp(-g_all)).astype(BF16)

    row = lax.broadcasted_iota(jnp.int32, (tb, tb), 0)
    col = lax.broadcasted_iota(jnp.int32, (tb, tb), 1)
    keep = (col <= row) & ((row // chunk) == (col // chunk))
    chunks = [slice(n * chunk, (n + 1) * chunk) for n in range(tb // chunk)]

    for h in range(heads):
        lanes = slice(h * HEAD_DIM, (h + 1) * HEAD_DIM)
        g, kk, qg = g_all[:, lanes], kk_all[:, lanes], qg_all[:, lanes]
        v = hi_ref[rows, lanes]

        scores = lax.dot_general(qg, kg_all[:, lanes], NT_DIMS, preferred_element_type=F32)
        scores = jnp.where(keep, scores, 0.0).astype(BF16)
        o = jnp.dot(scores, v, preferred_element_type=F32)

        upds, decays = [], []
        for sl in chunks:
            g_end = g[sl.stop - 1:sl.stop, :]
            kdec = (kk[sl] * jnp.exp(g_end - g[sl])).astype(BF16)
            upds.append(lax.dot_general(v[sl], kdec, TN_DIMS, preferred_element_type=F32))
            decays.append(jnp.exp(g_end))

        st = st_ref[h]
        inter = []
        for sl, upd, dec in zip(chunks, upds, decays):
            inter.append(lax.dot_general(qg[sl], st.astype(BF16), NT_DIMS,
                                         preferred_element_type=F32))
            st = dec * st + upd
        st_ref[h] = st

        o = o + jnp.concatenate(inter, axis=0)
        y = _rms(o, gn_ref[:, lanes]) * _silu(hg_ref[rows, lanes].astype(F32))
        o_ref[rows, lanes] = y.astype(o_ref.dtype)


def _hgrn(p, lb_rows, gn, casts, tb=256, heads=8, blocks_per_step=2):
    B, T, _ = p.shape
    c = HG_CHUNK
    idx = jnp.arange(tb)
    same = (idx[:, None] // c) == (idx[None, :] // c)
    tri1 = (same & (idx[None, :] <= idx[:, None])).astype(BF16)
    tri = jnp.concatenate([tri1, tri1], axis=1)
    kern = functools.partial(_hgrn_kernel, tb=tb, chunk=c, heads=heads, n_cast=len(casts))
    w = heads * HEAD_DIM
    groups = HG_HEADS // heads
    ts = tb * blocks_per_step
    grid = (B, groups, T // ts)
    n_steps = grid[0] * grid[1] * grid[2]

    def col(blk):
        return pl.BlockSpec((None, ts, w), lambda b, g, i: (b, i, blk // heads + g))

    def step(b, g, i):
        return (b * grid[1] + g) * grid[2] + i

    cast_in, cast_out, cast_shapes = [], [], []
    for stack, layer in casts:
        _, R, C = stack.shape
        rows = R // n_steps
        cast_in.append(pl.BlockSpec((None, rows, C),
                                    lambda b, g, i, layer=layer: (layer, step(b, g, i), 0)))
        cast_out.append(pl.BlockSpec((rows, C), lambda b, g, i: (step(b, g, i), 0)))
        cast_shapes.append(jax.ShapeDtypeStruct((R, C), BF16))

    return pl.pallas_call(
        kern,
        grid=grid,
        in_specs=[
            col(HG_Q_BLK), col(HG_F_BLK), col(HG_I_BLK), col(HG_G_BLK),
            pl.BlockSpec((2, w), lambda b, g, i: (0, g)),
            pl.BlockSpec((1, w), lambda b, g, i: (0, g)),
            pl.BlockSpec((tb, 2 * tb), lambda b, g, i: (0, 0)),
        ] + cast_in,
        out_specs=[pl.BlockSpec((None, ts, w), lambda b, g, i: (b, i, g))] + cast_out,
        out_shape=[jax.ShapeDtypeStruct((B, T, HG_HEADS * HEAD_DIM), BF16)] + cast_shapes,
        scratch_shapes=[pltpu.VMEM((heads, HEAD_DIM, HEAD_DIM), F32)],
        compiler_params=_params(("parallel", "parallel", "arbitrary")),
        name="hgrn2",
    )(p, p, p, p, lb_rows, gn, tri, *[stack for stack, _ in casts])


def _mix_out_kernel(ysb_ref, yhg_ref, ga0_ref, ga1_ref, gb0_ref, gb1_ref,
                    wsb_ref, whg_ref, wout_ref, x_ref, g_ref, cond_ref, o_ref):
    ysb = ysb_ref[...]
    yhg = yhg_ref[...]
    half = ga0_ref.shape[-1]
    mo = None
    for c, (ga_ref, gb_ref) in enumerate(((ga0_ref, gb0_ref), (ga1_ref, gb1_ref))):
        cols = slice(c * half, (c + 1) * half)
        a = jnp.dot(ysb, wsb_ref[:, cols], preferred_element_type=F32)
        b = jnp.dot(yhg, whg_ref[:, cols], preferred_element_type=F32)
        m = (jax.nn.sigmoid(ga_ref[...].astype(F32)) * a
             + jax.nn.sigmoid(gb_ref[...].astype(F32)) * b)
        part = jnp.dot(m.astype(BF16), wout_ref[cols, :], preferred_element_type=F32)
        mo = part if mo is None else mo + part
    o_ref[...] = x_ref[...] + cond_ref[2:3, :] * _rms(mo, g_ref[...])


def _mix_out(y_sb, y_hg, p, w_sb, w_hg, w_out, x, g, cond, tm=512):
    B, T, D = x.shape
    W = y_sb.shape[-1]
    half = D // 2
    ga_blk, gb_blk = GATE_A_COL // half, GATE_B_COL // half

    def gate(blk):
        return pl.BlockSpec((None, tm, half), lambda b, i: (b, i, blk))

    def resident(shape):
        return pl.BlockSpec(shape, lambda b, i: (0, 0), pipeline_mode=pl.Buffered(1))

    return pl.pallas_call(
        _mix_out_kernel,
        grid=(B, T // tm),
        in_specs=[
            pl.BlockSpec((None, tm, W), lambda b, i: (b, i, 0)),
            pl.BlockSpec((None, tm, W), lambda b, i: (b, i, 0)),
            gate(ga_blk), gate(ga_blk + 1), gate(gb_blk), gate(gb_blk + 1),
            resident((W, D)), resident((W, D)), resident((D, D)),
            pl.BlockSpec((None, tm, D), lambda b, i: (b, i, 0)),
            pl.BlockSpec((1, D), lambda b, i: (0, 0)),
            pl.BlockSpec((None, 6, D), lambda b, i: (b, 0, 0)),
        ],
        out_specs=pl.BlockSpec((None, tm, D), lambda b, i: (b, i, 0)),
        out_shape=jax.ShapeDtypeStruct((B, T, D), F32),
        compiler_params=_params(("parallel", "parallel")),
        name="mix_out",
    )(y_sb, y_hg, p, p, p, p, w_sb, w_hg, w_out, x, g, cond)


def _mlp_kernel(x_ref, g1_ref, g2_ref, cond_ref, wup_ref, wdn_ref, o_ref, h_ref, *, sub):
    j = pl.program_id(2)
    tm = h_ref.shape[0]

    def accumulate(rows, first):
        for c in range(wup_ref.shape[1] // sub):
            cols = slice(c * sub, (c + 1) * sub)
            u = jnp.maximum(jnp.dot(h_ref[rows, :], wup_ref[:, cols],
                                    preferred_element_type=F32), 0.0)
            part = jnp.dot((u * u).astype(BF16), wdn_ref[cols, :], preferred_element_type=F32)
            if first and c == 0:
                o_ref[rows, :] = part
            else:
                o_ref[rows, :] += part

    @pl.when(j == 0)
    def _():
        shift, scale = cond_ref[3:4, :], cond_ref[4:5, :]
        _norm_modulate_to(h_ref, x_ref, g1_ref, shift, scale, first_row=0, n_rows=EDGE_ROWS,
                          inline=True)
        accumulate(slice(0, EDGE_ROWS), True)
        _norm_modulate_to(h_ref, x_ref, g1_ref, shift, scale, first_row=EDGE_ROWS,
                          n_rows=tm - EDGE_ROWS, inline=True)
        accumulate(slice(EDGE_ROWS, tm), True)

    last = pl.num_programs(2) - 1

    @pl.when((j > 0) & (j < last))
    def _():
        accumulate(slice(None), False)

    def finish(first_row, n_rows):
        gate = cond_ref[5:6, :]
        g2 = g2_ref[...]
        for r in range(first_row, first_row + n_rows, 32):
            rows = slice(r, r + 32)
            o_ref[rows, :] = x_ref[rows, :] + gate * _rms(o_ref[rows, :], g2)

    @pl.when(j == last)
    def _():
        half = tm // 2
        accumulate(slice(0, half), False)
        finish(0, half)
        second = pl.multiple_of(jnp.minimum(j, 1) * half, half)
        accumulate(pl.ds(second, half), False)
        finish(half, half)


def _mlp(x, g1, g2, cond, w_up, w_dn, tm=512, tf=2048, sub=1024):
    B, T, D = x.shape
    F = w_up.shape[1]
    return pl.pallas_call(
        functools.partial(_mlp_kernel, sub=sub),
        grid=(B, T // tm, F // tf),
        in_specs=[
            pl.BlockSpec((None, tm, D), lambda b, i, j: (b, i, 0)),
            pl.BlockSpec((1, D), lambda b, i, j: (0, 0)),
            pl.BlockSpec((1, D), lambda b, i, j: (0, 0)),
            pl.BlockSpec((None, 6, D), lambda b, i, j: (b, 0, 0)),
            pl.BlockSpec((D, tf), lambda b, i, j: (0, j)),
            pl.BlockSpec((tf, D), lambda b, i, j: (j, 0)),
        ],
        out_specs=pl.BlockSpec((None, tm, D), lambda b, i, j: (b, i, 0)),
        out_shape=jax.ShapeDtypeStruct((B, T, D), F32),
        scratch_shapes=[pltpu.VMEM((tm, D), BF16)],
        compiler_params=_params(("parallel", "parallel", "arbitrary")),
        name="mlp",
    )(x, g1, g2, cond, w_up, w_dn)


def kernel(x, c, w_ada, b_ada, g_pre_mix, g_post_mix, w_in, hg_lb_logits, g_hg_norm,
           w_proj_sb, w_proj_hg, w_out, g_pre_mlp, g_post_mlp, w_mlp_up, w_mlp_down):
    B, T, D = x.shape
    L = w_ada.shape[0]

    lbs = jnp.cumsum(jax.nn.softmax(hg_lb_logits.astype(F32), axis=0), axis=0)
    lbs = lbs - lbs[0:1]
    lb_rows = jnp.stack([lbs, 1.0 - lbs], axis=1)

    c_pad = jnp.pad(c, ((0, 8 - B), (0, 0)))
    cond_all = _ada_cond(c_pad, w_ada, b_ada)[:, :B, :].reshape(L, B, 6, D)

    w_in_l = w_in[0].astype(BF16)

    for l in range(L):
        cond = cond_all[l]
        p = _in_proj(x, g_pre_mix[l].reshape(1, D), cond, w_in_l)
        y_sb = _sb_attention(p)
        casts = [(w, l) for w in (w_proj_sb, w_proj_hg, w_out, w_mlp_up, w_mlp_down)]
        casts += [(w_in, l + 1)] if l + 1 < L else []
        y_hg, w_sb_l, w_hg_l, w_out_l, w_up_l, w_dn_l, *w_in_next = _hgrn(
            p, lb_rows[l], g_hg_norm[l].reshape(1, -1), casts)
        x = _mix_out(y_sb, y_hg, p, w_sb_l, w_hg_l, w_out_l,
                     x, g_post_mix[l].reshape(1, D), cond)
        x = _mlp(x, g_pre_mlp[l].reshape(1, D), g_post_mlp[l].reshape(1, D), cond,
                 w_up_l, w_dn_l)
        if w_in_next:
            w_in_l = w_in_next[0]
    return x
```

```python
import functools
import math

import jax
import jax.numpy as jnp
from jax import lax
from jax.experimental import pallas as pl
from jax.experimental.pallas import tpu as pltpu

F32 = jnp.float32
BF16 = jnp.bfloat16

EPS = 1e-6
HEAD_DIM = 128
SB_HEADS = 8
HG_HEADS = 8
HG_CHUNK = 32

SB_Q_BLK, SB_K_BLK, SB_V_BLK = 0, 8, 16
HG_Q_BLK, HG_F_BLK, HG_I_BLK, HG_G_BLK = 24, 32, 40, 48
GATE_A_COL, GATE_B_COL = 7168, 9216

VMEM_LIMIT = 56 * 1024 * 1024

SKIP_BELOW = -104.0
LOG2E = 1.4426950408889634

EDGE_ROWS = 128

NT_DIMS = (((1,), (1,)), ((), ()))
TN_DIMS = (((0,), (0,)), ((), ()))


def _params(sem):
    return pltpu.CompilerParams(dimension_semantics=sem, vmem_limit_bytes=VMEM_LIMIT)


def _rms(xf, g):
    ms = jnp.mean(xf * xf, axis=-1, keepdims=True)
    return xf * lax.rsqrt(ms + EPS) * g


def _silu(x):
    hx = 0.5 * x
    return hx + hx * jnp.tanh(hx)


def _for_row_chunks(n_rows, chunk_rows, fn, first_row=0):
    def body(r, carry):
        fn(pl.ds(pl.multiple_of(first_row + r * chunk_rows, chunk_rows), chunk_rows))
        return carry
    lax.fori_loop(0, n_rows // chunk_rows, body, 0, unroll=4)


def _norm_modulate_to(h_ref, x_ref, g_ref, shift, scale, first_row=0, n_rows=None,
                      inline=False):
    gmod = g_ref[...] * (1.0 + scale)
    n_rows = x_ref.shape[0] if n_rows is None else n_rows

    def chunk(rows):
        x = x_ref[rows, :]
        ms = jnp.mean(x * x, axis=-1, keepdims=True)
        h_ref[rows, :] = (x * lax.rsqrt(ms + EPS) * gmod + shift).astype(h_ref.dtype)

    if inline:
        for r in range(first_row, first_row + n_rows, 32):
            chunk(slice(r, r + 32))
    else:
        _for_row_chunks(n_rows, 32, chunk, first_row)


def _split_bf16(a):
    hi = a.astype(BF16)
    lo = (a - hi.astype(F32)).astype(BF16)
    return hi, lo


def _ada_kernel(c_ref, w_ref, b_ref, o_ref):
    c = c_ref[...]
    ca = (c * jax.nn.sigmoid(c)).astype(BF16)
    o_ref[...] = jnp.dot(ca, w_ref[...].astype(BF16),
                         preferred_element_type=F32) + b_ref[...]


def _ada_cond(c_pad, w_ada, b_ada, tn=1024):
    L, D, N = w_ada.shape
    M = c_pad.shape[0]
    return pl.pallas_call(
        _ada_kernel,
        grid=(L, N // tn),
        in_specs=[
            pl.BlockSpec((M, D), lambda l, j: (0, 0)),
            pl.BlockSpec((None, D, tn), lambda l, j: (l, 0, j)),
            pl.BlockSpec((None, 1, tn), lambda l, j: (l, 0, j)),
        ],
        out_specs=pl.BlockSpec((None, M, tn), lambda l, j: (l, 0, j)),
        out_shape=jax.ShapeDtypeStruct((L, M, N), F32),
        compiler_params=_params(("parallel", "parallel")),
        name="ada_cond",
    )(c_pad, w_ada, b_ada.reshape(L, 1, N))


def _in_proj_kernel(x_ref, g_ref, cond_ref, w_ref, o_ref, h_ref):
    tm = h_ref.shape[0]

    def project(rows):
        o_ref[rows, :] = jnp.dot(h_ref[rows, :], w_ref[...],
                                 preferred_element_type=F32).astype(o_ref.dtype)

    @pl.when(pl.program_id(2) == 0)
    def _():
        shift, scale = cond_ref[0:1, :], cond_ref[1:2, :]
        _norm_modulate_to(h_ref, x_ref, g_ref, shift, scale, first_row=0, n_rows=EDGE_ROWS,
                          inline=True)
        project(slice(0, EDGE_ROWS))
        _norm_modulate_to(h_ref, x_ref, g_ref, shift, scale, first_row=EDGE_ROWS,
                          n_rows=tm - EDGE_ROWS, inline=True)
        project(slice(EDGE_ROWS, tm))

    @pl.when(pl.program_id(2) > 0)
    def _():
        project(slice(None))


def _in_proj(x, g, cond, w, tm=512, tn=2816):
    B, T, D = x.shape
    N = w.shape[1]
    assert T % tm == 0 and N % tn == 0 and EDGE_ROWS < tm
    return pl.pallas_call(
        _in_proj_kernel,
        grid=(B, T // tm, N // tn),
        in_specs=[
            pl.BlockSpec((None, tm, D), lambda b, i, j: (b, i, 0)),
            pl.BlockSpec((1, D), lambda b, i, j: (0, 0)),
            pl.BlockSpec((None, 6, D), lambda b, i, j: (b, 0, 0)),
            pl.BlockSpec((D, tn), lambda b, i, j: (0, j)),
        ],
        out_specs=pl.BlockSpec((None, tm, tn), lambda b, i, j: (b, i, j)),
        out_shape=jax.ShapeDtypeStruct((B, T, N), BF16),
        scratch_shapes=[pltpu.VMEM((tm, D), BF16)],
        compiler_params=_params(("parallel", "parallel", "arbitrary")),
        name="in_proj",
    )(x, g, cond, w)


def _sb_kernel(q_ref, k_ref, v_ref, u_ref, o_ref, *, tile, heads, scale):
    tiles_per_step = q_ref.shape[0] // tile

    def body(sub, carry):
        _sb_query_tile(q_ref, k_ref, v_ref, u_ref, o_ref,
                       pl.program_id(2) * tiles_per_step + sub,
                       pl.ds(pl.multiple_of(sub * tile, tile), tile),
                       tile=tile, heads=heads, scale=scale)
        return carry

    lax.fori_loop(0, tiles_per_step, body, 0)


def _sb_query_tile(q_ref, k_ref, v_ref, u_ref, o_ref, i, q_rows, *, tile, heads, scale):
    u = u_ref[...]
    row = lax.broadcasted_iota(jnp.int32, (tile, tile), 0)
    col = lax.broadcasted_iota(jnp.int32, (tile, tile), 1)
    causal = col < row

    def tile_terms(h, j, diagonal):
        lanes = slice(h * HEAD_DIM, (h + 1) * HEAD_DIM)
        keys = pl.ds(pl.multiple_of(j * tile, tile), tile)
        s = lax.dot_general(q_ref[q_rows, lanes], k_ref[keys, lanes], NT_DIMS,
                            preferred_element_type=F32)
        e = jnp.exp2(jnp.abs(s) * (-scale * LOG2E))
        ln = jnp.maximum(s, 0.0) * (-scale) - jnp.log(1.0 + e)
        if diagonal:
            ln = jnp.where(causal, ln, 0.0)
        hi, lo = _split_bf16(ln)
        incl = jnp.dot(jnp.concatenate([hi, lo], axis=1), u, preferred_element_type=F32)
        a = jnp.exp(s * scale + incl)
        if diagonal:
            a = jnp.where(causal, a, 0.0)
        pv = jnp.dot(a.astype(BF16), v_ref[keys, lanes], preferred_element_type=F32)
        return pv, incl[:, 0:1]

    has_prev = i > 0
    jp = jnp.maximum(i - 1, 0)
    carries, accs = [], []
    for h in range(heads):
        pv_d, tot_d = tile_terms(h, i, True)
        pv_p, tot_p = tile_terms(h, jp, False)
        accs.append(pv_d + jnp.where(has_prev, jnp.exp(tot_d), 0.0) * pv_p)
        carries.append(tot_d + jnp.where(has_prev, tot_p, 0.0))

    tops = [jnp.max(c) for c in carries]
    for h in range(heads):
        def live(state):
            j, top, _, _ = state
            return (j >= 0) & (top >= SKIP_BELOW)

        def step(state, h=h):
            j, _, carry, acc = state
            pv, tot = tile_terms(h, j, False)
            carry_new = carry + tot
            return j - 1, jnp.max(carry_new), carry_new, acc + jnp.exp(carry) * pv

        _, _, _, acc = lax.while_loop(live, step, (i - 2, tops[h], carries[h], accs[h]))
        o_ref[q_rows, h * HEAD_DIM:(h + 1) * HEAD_DIM] = acc.astype(o_ref.dtype)


def _sb_attention(p, tile=256, heads=8, tiles_per_step=4):
    B, T, _ = p.shape
    idx = jnp.arange(tile)
    u1 = (idx[:, None] >= idx[None, :]).astype(BF16)
    u = jnp.concatenate([u1, u1], axis=0)
    kern = functools.partial(_sb_kernel, tile=tile, heads=heads,
                             scale=1.0 / math.sqrt(HEAD_DIM))
    w = heads * HEAD_DIM
    groups = SB_HEADS // heads
    tq = tile * tiles_per_step
    assert T % tq == 0 and SB_HEADS % heads == 0
    return pl.pallas_call(
        kern,
        grid=(B, groups, T // tq),
        in_specs=[
            pl.BlockSpec((None, tq, w), lambda b, g, i: (b, i, SB_Q_BLK // heads + g)),
            pl.BlockSpec((None, T, w), lambda b, g, i: (b, 0, SB_K_BLK // heads + g)),
            pl.BlockSpec((None, T, w), lambda b, g, i: (b, 0, SB_V_BLK // heads + g)),
            pl.BlockSpec((2 * tile, tile), lambda b, g, i: (0, 0)),
        ],
        out_specs=pl.BlockSpec((None, tq, w), lambda b, g, i: (b, i, g)),
        out_shape=jax.ShapeDtypeStruct((B, T, SB_HEADS * HEAD_DIM), BF16),
        compiler_params=_params(("parallel", "parallel", "arbitrary")),
        name="sb_attention",
    )(p, p, p, u)


def _hgrn_kernel(hq_ref, hf_ref, hi_ref, hg_ref, lb_ref, gn_ref, tri_ref, *rest,
                 tb, chunk, heads, n_cast):
    cast_src, (o_ref, *cast_dst), st_ref = rest[:n_cast], rest[n_cast:-1], rest[-1]
    for src, dst in zip(cast_src, cast_dst):
        dst[...] = src[...].astype(dst.dtype)

    @pl.when(pl.program_id(2) == 0)
    def _():
        st_ref[...] = jnp.zeros_like(st_ref)

    def body(sub, carry):
        rows = pl.ds(pl.multiple_of(sub * tb, tb), tb)
        _hgrn_block(hq_ref, hf_ref, hi_ref, hg_ref, lb_ref, gn_ref, tri_ref, o_ref, st_ref,
                    rows, tb=tb, chunk=chunk, heads=heads)
        return carry

    lax.fori_loop(0, o_ref.shape[0] // tb, body, 0)


def _hgrn_block(hq_ref, hf_ref, hi_ref, hg_ref, lb_ref, gn_ref, tri_ref, o_ref, st_ref,
                rows, *, tb, chunk, heads):
    hf = hf_ref[rows, :].astype(F32)
    lb = lb_ref[0:1, :]
    one_m_lb = lb_ref[1:2, :]

    e = jnp.exp(-jnp.abs(hf))
    r = 1.0 / (1.0 + e)
    er = e * r
    pos = hf >= 0.0
    logf = jnp.log(lb + one_m_lb * jnp.where(pos, r, er))
    kk_all = one_m_lb * jnp.where(pos, er, r)
    qq = _silu(hq_ref[rows, :].astype(F32))

    hi, lo = _split_bf16(logf)
    g_all = jnp.dot(tri_ref[...], jnp.concatenate([hi, lo], axis=0),
                    preferred_element_type=F32)
    qg_all = (qq * jnp.exp(g_all)).astype(BF16)
    kg_all = (kk_all * jnp.exp(-g_all)).astype(BF16)

    row = lax.broadcasted_iota(jnp.int32, (tb, tb), 0)
    col = lax.broadcasted_iota(jnp.int32, (tb, tb), 1)
    keep = (col <= row) & ((row // chunk) == (col // chunk))
    chunks = [slice(n * chunk, (n + 1) * chunk) for n in range(tb // chunk)]

    for h in range(heads):
        lanes = slice(h * HEAD_DIM, (h + 1) * HEAD_DIM)
        g, kk, qg = g_all[:, lanes], kk_all[:, lanes], qg_all[:, lanes]
        v = hi_ref[rows, lanes]

        scores = lax.dot_general(qg, kg_all[:, lanes], NT_DIMS, preferred_element_type=F32)
        scores = jnp.where(keep, scores, 0.0).astype(BF16)
        o = jnp.dot(scores, v, preferred_element_type=F32)

        upds, decays = [], []
        for sl in chunks:
            g_end = g[sl.stop - 1:sl.stop, :]
            kdec = (kk[sl] * jnp.exp(g_end - g[sl])).astype(BF16)
            upds.append(lax.dot_general(v[sl], kdec, TN_DIMS, preferred_element_type=F32))
            decays.append(jnp.exp(g_end))

        st = st_ref[h]
        inter = []
        for sl, upd, dec in zip(chunks, upds, decays):
            inter.append(lax.dot_general(qg[sl], st.astype(BF16), NT_DIMS,
                                         preferred_element_type=F32))
            st = dec * st + upd
        st_ref[h] = st

        o = o + jnp.concatenate(inter, axis=0)
        y = _rms(o, gn_ref[:, lanes]) * _silu(hg_ref[rows, lanes].astype(F32))
        o_ref[rows, lanes] = y.astype(o_ref.dtype)


def _hgrn(p, lb_rows, gn, casts, tb=256, heads=8, blocks_per_step=2):
    B, T, _ = p.shape
    c = HG_CHUNK
    idx = jnp.arange(tb)
    same = (idx[:, None] // c) == (idx[None, :] // c)
    tri1 = (same & (idx[None, :] <= idx[:, None])).astype(BF16)
    tri = jnp.concatenate([tri1, tri1], axis=1)
    kern = functools.partial(_hgrn_kernel, tb=tb, chunk=c, heads=heads, n_cast=len(casts))
    w = heads * HEAD_DIM
    groups = HG_HEADS // heads
    ts = tb * blocks_per_step
    grid = (B, groups, T // ts)
    n_steps = grid[0] * grid[1] * grid[2]
    assert T % ts == 0 and tb % c == 0 and HG_HEADS % heads == 0
    assert all(stack.shape[1] % (16 * n_steps) == 0 for stack, _ in casts)

    def col(blk):
        return pl.BlockSpec((None, ts, w), lambda b, g, i: (b, i, blk // heads + g))

    def step(b, g, i):
        return (b * grid[1] + g) * grid[2] + i

    cast_in, cast_out, cast_shapes = [], [], []
    for stack, layer in casts:
        _, R, C = stack.shape
        rows = R // n_steps
        cast_in.append(pl.BlockSpec((None, rows, C),
                                    lambda b, g, i, layer=layer: (layer, step(b, g, i), 0)))
        cast_out.append(pl.BlockSpec((rows, C), lambda b, g, i: (step(b, g, i), 0)))
        cast_shapes.append(jax.ShapeDtypeStruct((R, C), BF16))

    return pl.pallas_call(
        kern,
        grid=grid,
        in_specs=[
            col(HG_Q_BLK), col(HG_F_BLK), col(HG_I_BLK), col(HG_G_BLK),
            pl.BlockSpec((2, w), lambda b, g, i: (0, g)),
            pl.BlockSpec((1, w), lambda b, g, i: (0, g)),
            pl.BlockSpec((tb, 2 * tb), lambda b, g, i: (0, 0)),
        ] + cast_in,
        out_specs=[pl.BlockSpec((None, ts, w), lambda b, g, i: (b, i, g))] + cast_out,
        out_shape=[jax.ShapeDtypeStruct((B, T, HG_HEADS * HEAD_DIM), BF16)] + cast_shapes,
        scratch_shapes=[pltpu.VMEM((heads, HEAD_DIM, HEAD_DIM), F32)],
        compiler_params=_params(("parallel", "parallel", "arbitrary")),
        name="hgrn2",
    )(p, p, p, p, lb_rows, gn, tri, *[stack for stack, _ in casts])


def _mix_out_kernel(ysb_ref, yhg_ref, ga0_ref, ga1_ref, gb0_ref, gb1_ref,
                    wsb_ref, whg_ref, wout_ref, x_ref, g_ref, cond_ref, o_ref):
    ysb = ysb_ref[...]
    yhg = yhg_ref[...]
    half = ga0_ref.shape[-1]
    mo = None
    for c, (ga_ref, gb_ref) in enumerate(((ga0_ref, gb0_ref), (ga1_ref, gb1_ref))):
        cols = slice(c * half, (c + 1) * half)
        a = jnp.dot(ysb, wsb_ref[:, cols], preferred_element_type=F32)
        b = jnp.dot(yhg, whg_ref[:, cols], preferred_element_type=F32)
        m = (jax.nn.sigmoid(ga_ref[...].astype(F32)) * a
             + jax.nn.sigmoid(gb_ref[...].astype(F32)) * b)
        part = jnp.dot(m.astype(BF16), wout_ref[cols, :], preferred_element_type=F32)
        mo = part if mo is None else mo + part
    o_ref[...] = x_ref[...] + cond_ref[2:3, :] * _rms(mo, g_ref[...])


def _mix_out(y_sb, y_hg, p, w_sb, w_hg, w_out, x, g, cond, tm=512):
    B, T, D = x.shape
    W = y_sb.shape[-1]
    half = D // 2
    ga_blk, gb_blk = GATE_A_COL // half, GATE_B_COL // half
    assert T % tm == 0 and GATE_A_COL % half == 0 and GATE_B_COL % half == 0

    def gate(blk):
        return pl.BlockSpec((None, tm, half), lambda b, i: (b, i, blk))

    def resident(shape):
        return pl.BlockSpec(shape, lambda b, i: (0, 0), pipeline_mode=pl.Buffered(1))

    return pl.pallas_call(
        _mix_out_kernel,
        grid=(B, T // tm),
        in_specs=[
            pl.BlockSpec((None, tm, W), lambda b, i: (b, i, 0)),
            pl.BlockSpec((None, tm, W), lambda b, i: (b, i, 0)),
            gate(ga_blk), gate(ga_blk + 1), gate(gb_blk), gate(gb_blk + 1),
            resident((W, D)), resident((W, D)), resident((D, D)),
            pl.BlockSpec((None, tm, D), lambda b, i: (b, i, 0)),
            pl.BlockSpec((1, D), lambda b, i: (0, 0)),
            pl.BlockSpec((None, 6, D), lambda b, i: (b, 0, 0)),
        ],
        out_specs=pl.BlockSpec((None, tm, D), lambda b, i: (b, i, 0)),
        out_shape=jax.ShapeDtypeStruct((B, T, D), F32),
        compiler_params=_params(("parallel", "parallel")),
        name="mix_out",
    )(y_sb, y_hg, p, p, p, p, w_sb, w_hg, w_out, x, g, cond)


def _mlp_kernel(x_ref, g1_ref, g2_ref, cond_ref, wup_hbm, wdn_hbm, o_ref,
                h_ref, wup_buf, wdn_buf, sem, *, tf, sub):
    tm = h_ref.shape[0]
    n_tiles = wup_hbm.shape[1] // tf
    step = pl.program_id(0) * pl.num_programs(1) + pl.program_id(1)
    n_steps = pl.num_programs(0) * pl.num_programs(1)

    def weight_copies(j, slot):
        return (pltpu.make_async_copy(wup_hbm.at[:, pl.ds(j * tf, tf)], wup_buf.at[slot],
                                      sem.at[0, slot]),
                pltpu.make_async_copy(wdn_hbm.at[pl.ds(j * tf, tf), :], wdn_buf.at[slot],
                                      sem.at[1, slot]))

    def start(j, slot):
        for cp in weight_copies(j, slot):
            cp.start()

    def wait(j, slot):
        for cp in weight_copies(j, slot):
            cp.wait()

    @pl.when(step == 0)
    def _():
        start(0, 0)

    def accumulate(slot, rows, first):
        for c in range(tf // sub):
            cols = slice(c * sub, (c + 1) * sub)
            u = jnp.maximum(jnp.dot(h_ref[rows, :], wup_buf[slot, :, cols],
                                    preferred_element_type=F32), 0.0)
            part = jnp.dot((u * u).astype(BF16), wdn_buf[slot, cols, :],
                           preferred_element_type=F32)
            if first and c == 0:
                o_ref[rows, :] = part
            else:
                o_ref[rows, :] += part

    def finish(first_row, n_rows):
        gate = cond_ref[5:6, :]
        g2 = g2_ref[...]
        for r in range(first_row, first_row + n_rows, 32):
            rows = slice(r, r + 32)
            o_ref[rows, :] = x_ref[rows, :] + gate * _rms(o_ref[rows, :], g2)

    for j in range(n_tiles):
        slot = j % 2
        wait(j, slot)
        if j + 1 < n_tiles:
            start(j + 1, 1 - slot)
        else:
            @pl.when(step + 1 < n_steps)
            def _():
                start(0, 1 - slot)

        if j == 0:
            shift, scale = cond_ref[3:4, :], cond_ref[4:5, :]
            _norm_modulate_to(h_ref, x_ref, g1_ref, shift, scale, first_row=0,
                              n_rows=EDGE_ROWS, inline=True)
            accumulate(slot, slice(0, EDGE_ROWS), True)
            _norm_modulate_to(h_ref, x_ref, g1_ref, shift, scale, first_row=EDGE_ROWS,
                              n_rows=tm - EDGE_ROWS, inline=True)
            accumulate(slot, slice(EDGE_ROWS, tm), True)
        elif j + 1 < n_tiles:
            accumulate(slot, slice(None), False)
        else:
            half = tm // 2
            accumulate(slot, slice(0, half), False)
            finish(0, half)
            second = pl.multiple_of(jnp.minimum(step + 1, 1) * half, half)
            accumulate(slot, pl.ds(second, half), False)
            finish(half, half)


def _mlp(x, g1, g2, cond, w_up, w_dn, tm=512, tf=2048, sub=1024):
    B, T, D = x.shape
    F = w_up.shape[1]
    n_tiles = F // tf
    assert T % tm == 0 and F % tf == 0 and tf % sub == 0
    assert n_tiles >= 2 and n_tiles % 2 == 0
    assert EDGE_ROWS < tm and tm % (2 * EDGE_ROWS) == 0
    return pl.pallas_call(
        functools.partial(_mlp_kernel, tf=tf, sub=sub),
        grid=(B, T // tm),
        in_specs=[
            pl.BlockSpec((None, tm, D), lambda b, i: (b, i, 0)),
            pl.BlockSpec((1, D), lambda b, i: (0, 0)),
            pl.BlockSpec((1, D), lambda b, i: (0, 0)),
            pl.BlockSpec((None, 6, D), lambda b, i: (b, 0, 0)),
            pl.BlockSpec(memory_space=pl.ANY),
            pl.BlockSpec(memory_space=pl.ANY),
        ],
        out_specs=pl.BlockSpec((None, tm, D), lambda b, i: (b, i, 0)),
        out_shape=jax.ShapeDtypeStruct((B, T, D), F32),
        scratch_shapes=[
            pltpu.VMEM((tm, D), BF16),
            pltpu.VMEM((2, D, tf), BF16),
            pltpu.VMEM((2, tf, D), BF16),
            pltpu.SemaphoreType.DMA((2, 2)),
        ],
        compiler_params=_params(("arbitrary", "arbitrary")),
        name="mlp",
    )(x, g1, g2, cond, w_up, w_dn)


def kernel(x, c, w_ada, b_ada, g_pre_mix, g_post_mix, w_in, hg_lb_logits, g_hg_norm,
           w_proj_sb, w_proj_hg, w_out, g_pre_mlp, g_post_mlp, w_mlp_up, w_mlp_down):
    B, T, D = x.shape
    L = w_ada.shape[0]
    assert GATE_A_COL == (HG_G_BLK + HG_HEADS) * HEAD_DIM and GATE_B_COL == GATE_A_COL + D
    assert w_in.shape[1:] == (D, GATE_B_COL + D) and B <= 8

    lbs = jnp.cumsum(jax.nn.softmax(hg_lb_logits.astype(F32), axis=0), axis=0)
    lbs = lbs - lbs[0:1]
    lb_rows = jnp.stack([lbs, 1.0 - lbs], axis=1)

    c_pad = jnp.pad(c, ((0, 8 - B), (0, 0)))
    cond_all = _ada_cond(c_pad, w_ada, b_ada)[:, :B, :].reshape(L, B, 6, D)

    w_in_l = w_in[0].astype(BF16)

    for l in range(L):
        cond = cond_all[l]
        p = _in_proj(x, g_pre_mix[l].reshape(1, D), cond, w_in_l)
        y_sb = _sb_attention(p)
        casts = [(w, l) for w in (w_proj_sb, w_proj_hg, w_out, w_mlp_up, w_mlp_down)]
        casts += [(w_in, l + 1)] if l + 1 < L else []
        y_hg, w_sb_l, w_hg_l, w_out_l, w_up_l, w_dn_l, *w_in_next = _hgrn(
            p, lb_rows[l], g_hg_norm[l].reshape(1, -1), casts)
        x = _mix_out(y_sb, y_hg, p, w_sb_l, w_hg_l, w_out_l,
                     x, g_post_mix[l].reshape(1, D), cond)
        x = _mlp(x, g_pre_mlp[l].reshape(1, D), g_post_mlp[l].reshape(1, D), cond,
                 w_up_l, w_dn_l)
        if w_in_next:
            w_in_l = w_in_next[0]
    return x
```

```python
import functools
import math

import jax
import jax.numpy as jnp
import numpy as np
from jax import lax
from jax.experimental import pallas as pl
from jax.experimental.pallas import tpu as pltpu

F32 = jnp.float32
BF16 = jnp.bfloat16

EPS = 1e-6
HEAD_DIM = 128
SB_HEADS = 8
HG_HEADS = 8
HG_CHUNK = 32

SB_Q_BLK, SB_K_BLK, SB_V_BLK = 0, 8, 16
HG_Q_BLK, HG_F_BLK, HG_I_BLK, HG_G_BLK = 24, 32, 40, 48
GATE_A_COL, GATE_B_COL = 7168, 9216

VMEM_LIMIT = 56 * 1024 * 1024

SKIP_BELOW = -104.0
LOG2E = 1.4426950408889634

EDGE_ROWS = 128

NT_DIMS = (((1,), (1,)), ((), ()))
TN_DIMS = (((0,), (0,)), ((), ()))


def _params(sem):
    return pltpu.CompilerParams(dimension_semantics=sem, vmem_limit_bytes=VMEM_LIMIT)


def _rms(xf, g):
    ms = jnp.mean(xf * xf, axis=-1, keepdims=True)
    return xf * lax.rsqrt(ms + EPS) * g


def _silu(x):
    hx = 0.5 * x
    return hx + hx * jnp.tanh(hx)


def _for_row_chunks(n_rows, chunk_rows, fn, first_row=0):
    def body(r, carry):
        fn(pl.ds(pl.multiple_of(first_row + r * chunk_rows, chunk_rows), chunk_rows))
        return carry
    lax.fori_loop(0, n_rows // chunk_rows, body, 0, unroll=4)


def _norm_modulate_to(h_ref, x_ref, g_ref, shift, scale, first_row=0, n_rows=None,
                      inline=False):
    gmod = g_ref[...] * (1.0 + scale)
    n_rows = x_ref.shape[0] if n_rows is None else n_rows

    def chunk(rows):
        x = x_ref[rows, :]
        ms = jnp.mean(x * x, axis=-1, keepdims=True)
        h_ref[rows, :] = (x * lax.rsqrt(ms + EPS) * gmod + shift).astype(h_ref.dtype)

    if inline:
        for r in range(first_row, first_row + n_rows, 32):
            chunk(slice(r, r + 32))
    else:
        _for_row_chunks(n_rows, 32, chunk, first_row)


def _split_bf16(a):
    hi = a.astype(BF16)
    lo = (a - hi.astype(F32)).astype(BF16)
    return hi, lo


def _ada_kernel(c_ref, w_ref, b_ref, o_ref):
    c = c_ref[...]
    ca = (c * jax.nn.sigmoid(c)).astype(BF16)
    o_ref[...] = jnp.dot(ca, w_ref[...].astype(BF16),
                         preferred_element_type=F32) + b_ref[...]


def _ada_cond(c_pad, w_ada, b_ada, tn=1024):
    L, D, N = w_ada.shape
    M = c_pad.shape[0]
    return pl.pallas_call(
        _ada_kernel,
        grid=(L, N // tn),
        in_specs=[
            pl.BlockSpec((M, D), lambda l, j: (0, 0)),
            pl.BlockSpec((None, D, tn), lambda l, j: (l, 0, j)),
            pl.BlockSpec((None, 1, tn), lambda l, j: (l, 0, j)),
        ],
        out_specs=pl.BlockSpec((None, M, tn), lambda l, j: (l, 0, j)),
        out_shape=jax.ShapeDtypeStruct((L, M, N), F32),
        compiler_params=_params(("parallel", "parallel")),
        name="ada_cond",
    )(c_pad, w_ada, b_ada.reshape(L, 1, N))


def _in_proj_kernel(x_ref, g_ref, cond_ref, w_ref, o_ref, h_ref):
    tm = h_ref.shape[0]

    def project(rows):
        o_ref[rows, :] = jnp.dot(h_ref[rows, :], w_ref[...],
                                 preferred_element_type=F32).astype(o_ref.dtype)

    @pl.when(pl.program_id(2) == 0)
    def _():
        shift, scale = cond_ref[0:1, :], cond_ref[1:2, :]
        _norm_modulate_to(h_ref, x_ref, g_ref, shift, scale, first_row=0, n_rows=EDGE_ROWS,
                          inline=True)
        project(slice(0, EDGE_ROWS))
        _norm_modulate_to(h_ref, x_ref, g_ref, shift, scale, first_row=EDGE_ROWS,
                          n_rows=tm - EDGE_ROWS, inline=True)
        project(slice(EDGE_ROWS, tm))

    @pl.when(pl.program_id(2) > 0)
    def _():
        project(slice(None))


def _in_proj(x, g, cond, w, layer, tm=512, tn=2816):
    B, T, D = x.shape
    N = w.shape[1]
    assert T % tm == 0 and N % tn == 0 and EDGE_ROWS < tm
    return pl.pallas_call(
        _in_proj_kernel,
        grid=(B, T // tm, N // tn),
        in_specs=[
            pl.BlockSpec((None, tm, D), lambda b, i, j: (b, i, 0)),
            pl.BlockSpec((None, 1, D), lambda b, i, j: (layer, 0, 0)),
            pl.BlockSpec((None, None, 6, D), lambda b, i, j: (layer, b, 0, 0)),
            pl.BlockSpec((D, tn), lambda b, i, j: (0, j)),
        ],
        out_specs=pl.BlockSpec((None, tm, tn), lambda b, i, j: (b, i, j)),
        out_shape=jax.ShapeDtypeStruct((B, T, N), BF16),
        scratch_shapes=[pltpu.VMEM((tm, D), BF16)],
        compiler_params=_params(("parallel", "parallel", "arbitrary")),
        name="in_proj",
    )(x, g, cond, w)


def _sb_kernel(q_ref, k_ref, v_ref, u_ref, o_ref, *, tile, heads, scale):
    tiles_per_step = q_ref.shape[0] // tile

    def body(sub, carry):
        _sb_query_tile(q_ref, k_ref, v_ref, u_ref, o_ref,
                       pl.program_id(2) * tiles_per_step + sub,
                       pl.ds(pl.multiple_of(sub * tile, tile), tile),
                       tile=tile, heads=heads, scale=scale)
        return carry

    lax.fori_loop(0, tiles_per_step, body, 0)


def _sb_query_tile(q_ref, k_ref, v_ref, u_ref, o_ref, i, q_rows, *, tile, heads, scale):
    u = u_ref[...]
    row = lax.broadcasted_iota(jnp.int32, (tile, tile), 0)
    col = lax.broadcasted_iota(jnp.int32, (tile, tile), 1)
    causal = col < row

    def tile_terms(h, j, diagonal):
        lanes = slice(h * HEAD_DIM, (h + 1) * HEAD_DIM)
        keys = pl.ds(pl.multiple_of(j * tile, tile), tile)
        s = lax.dot_general(q_ref[q_rows, lanes], k_ref[keys, lanes], NT_DIMS,
                            preferred_element_type=F32)
        e = jnp.exp2(jnp.abs(s) * (-scale * LOG2E))
        ln = jnp.maximum(s, 0.0) * (-scale) - jnp.log(1.0 + e)
        if diagonal:
            ln = jnp.where(causal, ln, 0.0)
        hi, lo = _split_bf16(ln)
        incl = jnp.dot(jnp.concatenate([hi, lo], axis=1), u, preferred_element_type=F32)
        a = jnp.exp(s * scale + incl)
        if diagonal:
            a = jnp.where(causal, a, 0.0)
        pv = jnp.dot(a.astype(BF16), v_ref[keys, lanes], preferred_element_type=F32)
        return pv, incl[:, 0:1]

    has_prev = i > 0
    jp = jnp.maximum(i - 1, 0)
    carries, accs = [], []
    for h in range(heads):
        pv_d, tot_d = tile_terms(h, i, True)
        pv_p, tot_p = tile_terms(h, jp, False)
        accs.append(pv_d + jnp.where(has_prev, jnp.exp(tot_d), 0.0) * pv_p)
        carries.append(tot_d + jnp.where(has_prev, tot_p, 0.0))

    tops = [jnp.max(c) for c in carries]
    for h in range(heads):
        def live(state):
            j, top, _, _ = state
            return (j >= 0) & (top >= SKIP_BELOW)

        def step(state, h=h):
            j, _, carry, acc = state
            pv, tot = tile_terms(h, j, False)
            carry_new = carry + tot
            return j - 1, jnp.max(carry_new), carry_new, acc + jnp.exp(carry) * pv

        _, _, _, acc = lax.while_loop(live, step, (i - 2, tops[h], carries[h], accs[h]))
        o_ref[q_rows, h * HEAD_DIM:(h + 1) * HEAD_DIM] = acc.astype(o_ref.dtype)


def _sb_attention(p, tile=256, heads=8, tiles_per_step=4):
    B, T, _ = p.shape
    idx = np.arange(tile)
    u1 = (idx[:, None] >= idx[None, :]).astype(np.float32)
    u = jnp.asarray(np.concatenate([u1, u1], axis=0), BF16)
    kern = functools.partial(_sb_kernel, tile=tile, heads=heads,
                             scale=1.0 / math.sqrt(HEAD_DIM))
    w = heads * HEAD_DIM
    groups = SB_HEADS // heads
    tq = tile * tiles_per_step
    assert T % tq == 0 and SB_HEADS % heads == 0
    return pl.pallas_call(
        kern,
        grid=(B, groups, T // tq),
        in_specs=[
            pl.BlockSpec((None, tq, w), lambda b, g, i: (b, i, SB_Q_BLK // heads + g)),
            pl.BlockSpec((None, T, w), lambda b, g, i: (b, 0, SB_K_BLK // heads + g)),
            pl.BlockSpec((None, T, w), lambda b, g, i: (b, 0, SB_V_BLK // heads + g)),
            pl.BlockSpec((2 * tile, tile), lambda b, g, i: (0, 0)),
        ],
        out_specs=pl.BlockSpec((None, tq, w), lambda b, g, i: (b, i, g)),
        out_shape=jax.ShapeDtypeStruct((B, T, SB_HEADS * HEAD_DIM), BF16),
        compiler_params=_params(("parallel", "parallel", "arbitrary")),
        name="sb_attention",
    )(p, p, p, u)


def _hgrn_kernel(hq_ref, hf_ref, hi_ref, hg_ref, lb_ref, gn_ref, tri_ref, *rest,
                 tb, chunk, heads, n_cast):
    cast_src, (o_ref, *cast_dst), st_ref = rest[:n_cast], rest[n_cast:-1], rest[-1]
    for src, dst in zip(cast_src, cast_dst):
        dst[...] = src[...].astype(dst.dtype)

    @pl.when(pl.program_id(2) == 0)
    def _():
        st_ref[...] = jnp.zeros_like(st_ref)

    def body(sub, carry):
        rows = pl.ds(pl.multiple_of(sub * tb, tb), tb)
        _hgrn_block(hq_ref, hf_ref, hi_ref, hg_ref, lb_ref, gn_ref, tri_ref, o_ref, st_ref,
                    rows, tb=tb, chunk=chunk, heads=heads)
        return carry

    lax.fori_loop(0, o_ref.shape[0] // tb, body, 0)


def _hgrn_block(hq_ref, hf_ref, hi_ref, hg_ref, lb_ref, gn_ref, tri_ref, o_ref, st_ref,
                rows, *, tb, chunk, heads):
    hf = hf_ref[rows, :].astype(F32)
    lb = lb_ref[0:1, :]
    one_m_lb = lb_ref[1:2, :]

    e = jnp.exp(-jnp.abs(hf))
    r = 1.0 / (1.0 + e)
    er = e * r
    pos = hf >= 0.0
    logf = jnp.log(lb + one_m_lb * jnp.where(pos, r, er))
    kk_all = one_m_lb * jnp.where(pos, er, r)
    qq = _silu(hq_ref[rows, :].astype(F32))

    hi, lo = _split_bf16(logf)
    g_all = jnp.dot(tri_ref[...], jnp.concatenate([hi, lo], axis=0),
                    preferred_element_type=F32)
    qg_all = (qq * jnp.exp(g_all)).astype(BF16)
    kg_all = (kk_all * jnp.exp(-g_all)).astype(BF16)

    row = lax.broadcasted_iota(jnp.int32, (tb, tb), 0)
    col = lax.broadcasted_iota(jnp.int32, (tb, tb), 1)
    keep = (col <= row) & ((row // chunk) == (col // chunk))
    chunks = [slice(n * chunk, (n + 1) * chunk) for n in range(tb // chunk)]

    for h in range(heads):
        lanes = slice(h * HEAD_DIM, (h + 1) * HEAD_DIM)
        g, kk, qg = g_all[:, lanes], kk_all[:, lanes], qg_all[:, lanes]
        v = hi_ref[rows, lanes]

        scores = lax.dot_general(qg, kg_all[:, lanes], NT_DIMS, preferred_element_type=F32)
        scores = jnp.where(keep, scores, 0.0).astype(BF16)
        o = jnp.dot(scores, v, preferred_element_type=F32)

        upds, decays = [], []
        for sl in chunks:
            g_end = g[sl.stop - 1:sl.stop, :]
            kdec = (kk[sl] * jnp.exp(g_end - g[sl])).astype(BF16)
            upds.append(lax.dot_general(v[sl], kdec, TN_DIMS, preferred_element_type=F32))
            decays.append(jnp.exp(g_end))

        st = st_ref[h]
        inter = []
        for sl, upd, dec in zip(chunks, upds, decays):
            inter.append(lax.dot_general(qg[sl], st.astype(BF16), NT_DIMS,
                                         preferred_element_type=F32))
            st = dec * st + upd
        st_ref[h] = st

        o = o + jnp.concatenate(inter, axis=0)
        y = _rms(o, gn_ref[:, lanes]) * _silu(hg_ref[rows, lanes].astype(F32))
        o_ref[rows, lanes] = y.astype(o_ref.dtype)


def _hgrn(p, lb_rows, gn, casts, layer, tb=256, heads=8, blocks_per_step=2):
    B, T, _ = p.shape
    c = HG_CHUNK
    idx = np.arange(tb)
    same = (idx[:, None] // c) == (idx[None, :] // c)
    tri1 = (same & (idx[None, :] <= idx[:, None])).astype(np.float32)
    tri = jnp.asarray(np.concatenate([tri1, tri1], axis=1), BF16)
    kern = functools.partial(_hgrn_kernel, tb=tb, chunk=c, heads=heads, n_cast=len(casts))
    w = heads * HEAD_DIM
    groups = HG_HEADS // heads
    ts = tb * blocks_per_step
    grid = (B, groups, T // ts)
    n_steps = grid[0] * grid[1] * grid[2]
    assert T % ts == 0 and tb % c == 0 and HG_HEADS % heads == 0
    assert all(stack.shape[1] % (16 * n_steps) == 0 for stack, _ in casts)

    def col(blk):
        return pl.BlockSpec((None, ts, w), lambda b, g, i: (b, i, blk // heads + g))

    def step(b, g, i):
        return (b * grid[1] + g) * grid[2] + i

    cast_in, cast_out, cast_shapes = [], [], []
    for stack, src_layer in casts:
        _, R, C = stack.shape
        rows = R // n_steps
        cast_in.append(pl.BlockSpec((None, rows, C),
                                    lambda b, g, i, src=src_layer: (src, step(b, g, i), 0)))
        cast_out.append(pl.BlockSpec((rows, C), lambda b, g, i: (step(b, g, i), 0)))
        cast_shapes.append(jax.ShapeDtypeStruct((R, C), BF16))

    return pl.pallas_call(
        kern,
        grid=grid,
        in_specs=[
            col(HG_Q_BLK), col(HG_F_BLK), col(HG_I_BLK), col(HG_G_BLK),
            pl.BlockSpec((None, 2, w), lambda b, g, i: (layer, 0, g)),
            pl.BlockSpec((None, 1, w), lambda b, g, i: (layer, 0, g)),
            pl.BlockSpec((tb, 2 * tb), lambda b, g, i: (0, 0)),
        ] + cast_in,
        out_specs=[pl.BlockSpec((None, ts, w), lambda b, g, i: (b, i, g))] + cast_out,
        out_shape=[jax.ShapeDtypeStruct((B, T, HG_HEADS * HEAD_DIM), BF16)] + cast_shapes,
        scratch_shapes=[pltpu.VMEM((heads, HEAD_DIM, HEAD_DIM), F32)],
        compiler_params=_params(("parallel", "parallel", "arbitrary")),
        name="hgrn2",
    )(p, p, p, p, lb_rows, gn, tri, *[stack for stack, _ in casts])


def _mix_out_kernel(ysb_ref, yhg_ref, ga0_ref, ga1_ref, gb0_ref, gb1_ref,
                    wsb_ref, whg_ref, wout_ref, x_ref, g_ref, cond_ref, o_ref):
    ysb = ysb_ref[...]
    yhg = yhg_ref[...]
    half = ga0_ref.shape[-1]
    mo = None
    for c, (ga_ref, gb_ref) in enumerate(((ga0_ref, gb0_ref), (ga1_ref, gb1_ref))):
        cols = slice(c * half, (c + 1) * half)
        a = jnp.dot(ysb, wsb_ref[:, cols], preferred_element_type=F32)
        b = jnp.dot(yhg, whg_ref[:, cols], preferred_element_type=F32)
        m = (jax.nn.sigmoid(ga_ref[...].astype(F32)) * a
             + jax.nn.sigmoid(gb_ref[...].astype(F32)) * b)
        part = jnp.dot(m.astype(BF16), wout_ref[cols, :], preferred_element_type=F32)
        mo = part if mo is None else mo + part
    o_ref[...] = x_ref[...] + cond_ref[2:3, :] * _rms(mo, g_ref[...])


def _mix_out(y_sb, y_hg, p, w_sb, w_hg, w_out, x, g, cond, layer, tm=512):
    B, T, D = x.shape
    W = y_sb.shape[-1]
    half = D // 2
    ga_blk, gb_blk = GATE_A_COL // half, GATE_B_COL // half
    assert T % tm == 0 and GATE_A_COL % half == 0 and GATE_B_COL % half == 0

    def gate(blk):
        return pl.BlockSpec((None, tm, half), lambda b, i: (b, i, blk))

    def resident(shape):
        return pl.BlockSpec(shape, lambda b, i: (0, 0), pipeline_mode=pl.Buffered(1))

    return pl.pallas_call(
        _mix_out_kernel,
        grid=(B, T // tm),
        in_specs=[
            pl.BlockSpec((None, tm, W), lambda b, i: (b, i, 0)),
            pl.BlockSpec((None, tm, W), lambda b, i: (b, i, 0)),
            gate(ga_blk), gate(ga_blk + 1), gate(gb_blk), gate(gb_blk + 1),
            resident((W, D)), resident((W, D)), resident((D, D)),
            pl.BlockSpec((None, tm, D), lambda b, i: (b, i, 0)),
            pl.BlockSpec((None, 1, D), lambda b, i: (layer, 0, 0)),
            pl.BlockSpec((None, None, 6, D), lambda b, i: (layer, b, 0, 0)),
        ],
        out_specs=pl.BlockSpec((None, tm, D), lambda b, i: (b, i, 0)),
        out_shape=jax.ShapeDtypeStruct((B, T, D), F32),
        compiler_params=_params(("parallel", "parallel")),
        name="mix_out",
    )(y_sb, y_hg, p, p, p, p, w_sb, w_hg, w_out, x, g, cond)


def _mlp_kernel(x_ref, g1_ref, g2_ref, cond_ref, wup_hbm, wdn_hbm, o_ref,
                h_ref, wup_buf, wdn_buf, sem, *, tf, sub):
    tm = h_ref.shape[0]
    n_tiles = wup_hbm.shape[1] // tf
    step = pl.program_id(0) * pl.num_programs(1) + pl.program_id(1)
    n_steps = pl.num_programs(0) * pl.num_programs(1)

    def weight_copies(j, slot):
        return (pltpu.make_async_copy(wup_hbm.at[:, pl.ds(j * tf, tf)], wup_buf.at[slot],
                                      sem.at[0, slot]),
                pltpu.make_async_copy(wdn_hbm.at[pl.ds(j * tf, tf), :], wdn_buf.at[slot],
                                      sem.at[1, slot]))

    def start(j, slot):
        for cp in weight_copies(j, slot):
            cp.start()

    def wait(j, slot):
        for cp in weight_copies(j, slot):
            cp.wait()

    @pl.when(step == 0)
    def _():
        start(0, 0)

    def accumulate(slot, rows, first):
        for c in range(tf // sub):
            cols = slice(c * sub, (c + 1) * sub)
            u = jnp.maximum(jnp.dot(h_ref[rows, :], wup_buf[slot, :, cols],
                                    preferred_element_type=F32), 0.0)
            part = jnp.dot((u * u).astype(BF16), wdn_buf[slot, cols, :],
                           preferred_element_type=F32)
            if first and c == 0:
                o_ref[rows, :] = part
            else:
                o_ref[rows, :] += part

    def finish(first_row, n_rows):
        gate = cond_ref[5:6, :]
        g2 = g2_ref[...]
        for r in range(first_row, first_row + n_rows, 32):
            rows = slice(r, r + 32)
            o_ref[rows, :] = x_ref[rows, :] + gate * _rms(o_ref[rows, :], g2)

    for j in range(n_tiles):
        slot = j % 2
        wait(j, slot)
        if j + 1 < n_tiles:
            start(j + 1, 1 - slot)
        else:
            @pl.when(step + 1 < n_steps)
            def _():
                start(0, 1 - slot)

        if j == 0:
            shift, scale = cond_ref[3:4, :], cond_ref[4:5, :]
            _norm_modulate_to(h_ref, x_ref, g1_ref, shift, scale, first_row=0,
                              n_rows=EDGE_ROWS, inline=True)
            accumulate(slot, slice(0, EDGE_ROWS), True)
            _norm_modulate_to(h_ref, x_ref, g1_ref, shift, scale, first_row=EDGE_ROWS,
                              n_rows=tm - EDGE_ROWS, inline=True)
            accumulate(slot, slice(EDGE_ROWS, tm), True)
        elif j + 1 < n_tiles:
            accumulate(slot, slice(None), False)
        else:
            half = tm // 2
            accumulate(slot, slice(0, half), False)
            finish(0, half)
            second = pl.multiple_of(jnp.minimum(step + 1, 1) * half, half)
            accumulate(slot, pl.ds(second, half), False)
            finish(half, half)


def _mlp(x, g1, g2, cond, w_up, w_dn, layer, tm=512, tf=2048, sub=1024):
    B, T, D = x.shape
    F = w_up.shape[1]
    n_tiles = F // tf
    assert T % tm == 0 and F % tf == 0 and tf % sub == 0
    assert n_tiles >= 2 and n_tiles % 2 == 0
    assert EDGE_ROWS < tm and tm % (2 * EDGE_ROWS) == 0
    return pl.pallas_call(
        functools.partial(_mlp_kernel, tf=tf, sub=sub),
        grid=(B, T // tm),
        in_specs=[
            pl.BlockSpec((None, tm, D), lambda b, i: (b, i, 0)),
            pl.BlockSpec((None, 1, D), lambda b, i: (layer, 0, 0)),
            pl.BlockSpec((None, 1, D), lambda b, i: (layer, 0, 0)),
            pl.BlockSpec((None, None, 6, D), lambda b, i: (layer, b, 0, 0)),
            pl.BlockSpec(memory_space=pl.ANY),
            pl.BlockSpec(memory_space=pl.ANY),
        ],
        out_specs=pl.BlockSpec((None, tm, D), lambda b, i: (b, i, 0)),
        out_shape=jax.ShapeDtypeStruct((B, T, D), F32),
        scratch_shapes=[
            pltpu.VMEM((tm, D), BF16),
            pltpu.VMEM((2, D, tf), BF16),
            pltpu.VMEM((2, tf, D), BF16),
            pltpu.SemaphoreType.DMA((2, 2)),
        ],
        compiler_params=_params(("arbitrary", "arbitrary")),
        name="mlp",
    )(x, g1, g2, cond, w_up, w_dn)


def kernel(x, c, w_ada, b_ada, g_pre_mix, g_post_mix, w_in, hg_lb_logits, g_hg_norm,
           w_proj_sb, w_proj_hg, w_out, g_pre_mlp, g_post_mlp, w_mlp_up, w_mlp_down):
    B, T, D = x.shape
    L = w_ada.shape[0]
    assert GATE_A_COL == (HG_G_BLK + HG_HEADS) * HEAD_DIM and GATE_B_COL == GATE_A_COL + D
    assert w_in.shape[1:] == (D, GATE_B_COL + D) and B <= 8

    lbs = jnp.cumsum(jax.nn.softmax(hg_lb_logits.astype(F32), axis=0), axis=0)
    lbs = lbs - lbs[0:1]
    lb_rows = jnp.stack([lbs, 1.0 - lbs], axis=1)

    c_pad = jnp.pad(c, ((0, 8 - B), (0, 0)))
    cond = _ada_cond(c_pad, w_ada, b_ada).reshape(L, 8, 6, D)

    g_pre_mix, g_post_mix, g_pre_mlp, g_post_mlp, g_hg_norm = (
        g.reshape(L, 1, -1) for g in (g_pre_mix, g_post_mix, g_pre_mlp, g_post_mlp, g_hg_norm))

    w_in_l = w_in[0].astype(BF16)

    for l in range(L):
        p = _in_proj(x, g_pre_mix, cond, w_in_l, l)
        y_sb = _sb_attention(p)
        casts = [(w, l) for w in (w_proj_sb, w_proj_hg, w_out, w_mlp_up, w_mlp_down)]
        casts += [(w_in, l + 1)] if l + 1 < L else []
        y_hg, w_sb_l, w_hg_l, w_out_l, w_up_l, w_dn_l, *w_in_next = _hgrn(
            p, lb_rows, g_hg_norm, casts, l)
        x = _mix_out(y_sb, y_hg, p, w_sb_l, w_hg_l, w_out_l, x, g_post_mix, cond, l)
        x = _mlp(x, g_pre_mlp, g_post_mlp, cond, w_up_l, w_dn_l, l)
        if w_in_next:
            w_in_l = w_in_next[0]
    return x
```

```python
import functools
import math

import jax
import jax.numpy as jnp
import numpy as np
from jax import lax
from jax.experimental import pallas as pl
from jax.experimental.pallas import tpu as pltpu

F32 = jnp.float32
BF16 = jnp.bfloat16

EPS = 1e-6
HEAD_DIM = 128
SB_HEADS = 8
HG_HEADS = 8
HG_CHUNK = 64

SB_Q_BLK, SB_K_BLK, SB_V_BLK = 0, 8, 16
HG_Q_BLK, HG_F_BLK, HG_I_BLK, HG_G_BLK = 24, 32, 40, 48
GATE_A_COL, GATE_B_COL = 7168, 9216

VMEM_LIMIT = 56 * 1024 * 1024

SKIP_BELOW = -104.0
LOG2E = 1.4426950408889634

EDGE_ROWS = 128

NT_DIMS = (((1,), (1,)), ((), ()))
TN_DIMS = (((0,), (0,)), ((), ()))


def _params(sem):
    return pltpu.CompilerParams(dimension_semantics=sem, vmem_limit_bytes=VMEM_LIMIT)


def _rms(xf, g):
    ms = jnp.mean(xf * xf, axis=-1, keepdims=True)
    return xf * lax.rsqrt(ms + EPS) * g


def _silu(x):
    hx = 0.5 * x
    return hx + hx * jnp.tanh(hx)


def _for_row_chunks(n_rows, chunk_rows, fn, first_row=0):
    def body(r, carry):
        fn(pl.ds(pl.multiple_of(first_row + r * chunk_rows, chunk_rows), chunk_rows))
        return carry
    lax.fori_loop(0, n_rows // chunk_rows, body, 0, unroll=4)


def _norm_modulate_to(h_ref, x_ref, g_ref, shift, scale, first_row=0, n_rows=None,
                      inline=False):
    gmod = g_ref[...] * (1.0 + scale)
    n_rows = x_ref.shape[0] if n_rows is None else n_rows

    def chunk(rows):
        x = x_ref[rows, :]
        ms = jnp.mean(x * x, axis=-1, keepdims=True)
        h_ref[rows, :] = (x * lax.rsqrt(ms + EPS) * gmod + shift).astype(h_ref.dtype)

    if inline:
        for r in range(first_row, first_row + n_rows, 32):
            chunk(slice(r, r + 32))
    else:
        _for_row_chunks(n_rows, 32, chunk, first_row)


def _split_bf16(a):
    hi = a.astype(BF16)
    lo = (a - hi.astype(F32)).astype(BF16)
    return hi, lo


def _ada_kernel(c_ref, w_ref, b_ref, o_ref):
    c = c_ref[...]
    ca = (c * jax.nn.sigmoid(c)).astype(BF16)
    o_ref[...] = jnp.dot(ca, w_ref[...].astype(BF16),
                         preferred_element_type=F32) + b_ref[...]


def _ada_cond(c_pad, w_ada, b_ada, tn=1024):
    L, D, N = w_ada.shape
    M = c_pad.shape[0]
    return pl.pallas_call(
        _ada_kernel,
        grid=(L, N // tn),
        in_specs=[
            pl.BlockSpec((M, D), lambda l, j: (0, 0)),
            pl.BlockSpec((None, D, tn), lambda l, j: (l, 0, j)),
            pl.BlockSpec((None, 1, tn), lambda l, j: (l, 0, j)),
        ],
        out_specs=pl.BlockSpec((None, M, tn), lambda l, j: (l, 0, j)),
        out_shape=jax.ShapeDtypeStruct((L, M, N), F32),
        compiler_params=_params(("parallel", "parallel")),
        name="ada_cond",
    )(c_pad, w_ada, b_ada.reshape(L, 1, N))


def _in_proj_kernel(x_ref, g_ref, cond_ref, w_ref, o_ref, h_ref):
    tm = h_ref.shape[0]

    def project(rows):
        o_ref[rows, :] = jnp.dot(h_ref[rows, :], w_ref[...],
                                 preferred_element_type=F32).astype(o_ref.dtype)

    @pl.when(pl.program_id(2) == 0)
    def _():
        shift, scale = cond_ref[0:1, :], cond_ref[1:2, :]
        _norm_modulate_to(h_ref, x_ref, g_ref, shift, scale, first_row=0, n_rows=EDGE_ROWS,
                          inline=True)
        project(slice(0, EDGE_ROWS))
        _norm_modulate_to(h_ref, x_ref, g_ref, shift, scale, first_row=EDGE_ROWS,
                          n_rows=tm - EDGE_ROWS, inline=True)
        project(slice(EDGE_ROWS, tm))

    @pl.when(pl.program_id(2) > 0)
    def _():
        project(slice(None))


def _in_proj(x, g, cond, w, layer, tm=512, tn=2816):
    B, T, D = x.shape
    N = w.shape[1]
    assert T % tm == 0 and N % tn == 0 and EDGE_ROWS < tm
    return pl.pallas_call(
        _in_proj_kernel,
        grid=(B, T // tm, N // tn),
        in_specs=[
            pl.BlockSpec((None, tm, D), lambda b, i, j: (b, i, 0)),
            pl.BlockSpec((None, 1, D), lambda b, i, j: (layer, 0, 0)),
            pl.BlockSpec((None, None, 6, D), lambda b, i, j: (layer, b, 0, 0)),
            pl.BlockSpec((D, tn), lambda b, i, j: (0, j)),
        ],
        out_specs=pl.BlockSpec((None, tm, tn), lambda b, i, j: (b, i, j)),
        out_shape=jax.ShapeDtypeStruct((B, T, N), BF16),
        scratch_shapes=[pltpu.VMEM((tm, D), BF16)],
        compiler_params=_params(("parallel", "parallel", "arbitrary")),
        name="in_proj",
    )(x, g, cond, w)


def _sb_kernel(q_ref, k_ref, v_ref, u_ref, o_ref, *, tile, heads, scale):
    tiles_per_step = q_ref.shape[0] // tile

    def body(sub, carry):
        _sb_query_tile(q_ref, k_ref, v_ref, u_ref, o_ref,
                       pl.program_id(2) * tiles_per_step + sub,
                       pl.ds(pl.multiple_of(sub * tile, tile), tile),
                       tile=tile, heads=heads, scale=scale)
        return carry

    lax.fori_loop(0, tiles_per_step, body, 0)


def _sb_query_tile(q_ref, k_ref, v_ref, u_ref, o_ref, i, q_rows, *, tile, heads, scale):
    u = u_ref[...]
    row = lax.broadcasted_iota(jnp.int32, (tile, tile), 0)
    col = lax.broadcasted_iota(jnp.int32, (tile, tile), 1)
    causal = col < row

    def tile_terms(h, j, diagonal):
        lanes = slice(h * HEAD_DIM, (h + 1) * HEAD_DIM)
        keys = pl.ds(pl.multiple_of(j * tile, tile), tile)
        s = lax.dot_general(q_ref[q_rows, lanes], k_ref[keys, lanes], NT_DIMS,
                            preferred_element_type=F32)
        e = jnp.exp2(jnp.abs(s) * (-scale * LOG2E))
        ln = jnp.maximum(s, 0.0) * (-scale) - jnp.log(1.0 + e)
        if diagonal:
            ln = jnp.where(causal, ln, 0.0)
        hi, lo = _split_bf16(ln)
        incl = jnp.dot(jnp.concatenate([hi, lo], axis=1), u, preferred_element_type=F32)
        a = jnp.exp(s * scale + incl)
        if diagonal:
            a = jnp.where(causal, a, 0.0)
        pv = jnp.dot(a.astype(BF16), v_ref[keys, lanes], preferred_element_type=F32)
        return pv, incl[:, 0:1]

    has_prev = i > 0
    jp = jnp.maximum(i - 1, 0)
    carries, accs = [], []
    for h in range(heads):
        pv_d, tot_d = tile_terms(h, i, True)
        pv_p, tot_p = tile_terms(h, jp, False)
        accs.append(pv_d + jnp.where(has_prev, jnp.exp(tot_d), 0.0) * pv_p)
        carries.append(tot_d + jnp.where(has_prev, tot_p, 0.0))

    tops = [jnp.max(c) for c in carries]
    for h in range(heads):
        def live(state):
            j, top, _, _ = state
            return (j >= 0) & (top >= SKIP_BELOW)

        def step(state, h=h):
            j, _, carry, acc = state
            pv, tot = tile_terms(h, j, False)
            carry_new = carry + tot
            return j - 1, jnp.max(carry_new), carry_new, acc + jnp.exp(carry) * pv

        _, _, _, acc = lax.while_loop(live, step, (i - 2, tops[h], carries[h], accs[h]))
        o_ref[q_rows, h * HEAD_DIM:(h + 1) * HEAD_DIM] = acc.astype(o_ref.dtype)


def _sb_attention(p, tile=256, heads=8, tiles_per_step=4):
    B, T, _ = p.shape
    idx = np.arange(tile)
    u1 = (idx[:, None] >= idx[None, :]).astype(np.float32)
    u = jnp.asarray(np.concatenate([u1, u1], axis=0), BF16)
    kern = functools.partial(_sb_kernel, tile=tile, heads=heads,
                             scale=1.0 / math.sqrt(HEAD_DIM))
    w = heads * HEAD_DIM
    groups = SB_HEADS // heads
    tq = tile * tiles_per_step
    assert T % tq == 0 and SB_HEADS % heads == 0
    return pl.pallas_call(
        kern,
        grid=(B, groups, T // tq),
        in_specs=[
            pl.BlockSpec((None, tq, w), lambda b, g, i: (b, i, SB_Q_BLK // heads + g)),
            pl.BlockSpec((None, T, w), lambda b, g, i: (b, 0, SB_K_BLK // heads + g)),
            pl.BlockSpec((None, T, w), lambda b, g, i: (b, 0, SB_V_BLK // heads + g)),
            pl.BlockSpec((2 * tile, tile), lambda b, g, i: (0, 0)),
        ],
        out_specs=pl.BlockSpec((None, tq, w), lambda b, g, i: (b, i, g)),
        out_shape=jax.ShapeDtypeStruct((B, T, SB_HEADS * HEAD_DIM), BF16),
        compiler_params=_params(("parallel", "parallel", "arbitrary")),
        name="sb_attention",
    )(p, p, p, u)


def _hgrn_kernel(hq_ref, hf_ref, hi_ref, hg_ref, lb_ref, gn_ref, tri_ref, *rest,
                 tb, chunk, heads, n_cast):
    cast_src, (o_ref, *cast_dst), st_ref = rest[:n_cast], rest[n_cast:-1], rest[-1]
    for src, dst in zip(cast_src, cast_dst):
        dst[...] = src[...].astype(dst.dtype)

    @pl.when(pl.program_id(2) == 0)
    def _():
        st_ref[...] = jnp.zeros_like(st_ref)

    def body(sub, carry):
        rows = pl.ds(pl.multiple_of(sub * tb, tb), tb)
        _hgrn_block(hq_ref, hf_ref, hi_ref, hg_ref, lb_ref, gn_ref, tri_ref, o_ref, st_ref,
                    rows, tb=tb, chunk=chunk, heads=heads)
        return carry

    lax.fori_loop(0, o_ref.shape[0] // tb, body, 0)


def _hgrn_block(hq_ref, hf_ref, hi_ref, hg_ref, lb_ref, gn_ref, tri_ref, o_ref, st_ref,
                rows, *, tb, chunk, heads):
    hf = hf_ref[rows, :].astype(F32)
    lb = lb_ref[0:1, :]
    one_m_lb = lb_ref[1:2, :]

    e = jnp.exp(-jnp.abs(hf))
    r = 1.0 / (1.0 + e)
    er = e * r
    pos = hf >= 0.0
    logf = jnp.log(lb + one_m_lb * jnp.where(pos, r, er))
    kk_all = one_m_lb * jnp.where(pos, er, r)
    qq = _silu(hq_ref[rows, :].astype(F32))

    hi, lo = _split_bf16(logf)
    g_all = jnp.dot(tri_ref[...], jnp.concatenate([hi, lo], axis=0),
                    preferred_element_type=F32)
    chunks = [slice(n * chunk, (n + 1) * chunk) for n in range(tb // chunk)]
    g_mid = jnp.concatenate(
        [jnp.broadcast_to(g_all[sl.start + chunk // 2 - 1:sl.start + chunk // 2, :],
                          (chunk, g_all.shape[1])) for sl in chunks], axis=0)
    rel = g_all - g_mid
    qg_all = (qq * jnp.exp(rel)).astype(BF16)
    kg_all = (kk_all * jnp.exp(-rel)).astype(BF16)
    q0_all = (qq * jnp.exp(g_all)).astype(BF16)

    row = lax.broadcasted_iota(jnp.int32, (tb, tb), 0)
    col = lax.broadcasted_iota(jnp.int32, (tb, tb), 1)
    keep = (col <= row) & ((row // chunk) == (col // chunk))

    for h in range(heads):
        lanes = slice(h * HEAD_DIM, (h + 1) * HEAD_DIM)
        g, kk, qg = g_all[:, lanes], kk_all[:, lanes], q0_all[:, lanes]
        v = hi_ref[rows, lanes]

        scores = lax.dot_general(qg_all[:, lanes], kg_all[:, lanes], NT_DIMS,
                                 preferred_element_type=F32)
        scores = jnp.where(keep, scores, 0.0).astype(BF16)
        o = jnp.dot(scores, v, preferred_element_type=F32)

        upds, decays = [], []
        for sl in chunks:
            g_end = g[sl.stop - 1:sl.stop, :]
            kdec = (kk[sl] * jnp.exp(g_end - g[sl])).astype(BF16)
            upds.append(lax.dot_general(v[sl], kdec, TN_DIMS, preferred_element_type=F32))
            decays.append(jnp.exp(g_end))

        st = st_ref[h]
        inter = []
        for sl, upd, dec in zip(chunks, upds, decays):
            inter.append(lax.dot_general(qg[sl], st.astype(BF16), NT_DIMS,
                                         preferred_element_type=F32))
            st = dec * st + upd
        st_ref[h] = st

        o = o + jnp.concatenate(inter, axis=0)
        y = _rms(o, gn_ref[:, lanes]) * _silu(hg_ref[rows, lanes].astype(F32))
        o_ref[rows, lanes] = y.astype(o_ref.dtype)


def _hgrn(p, lb_rows, gn, casts, layer, tb=256, heads=8, blocks_per_step=2):
    B, T, _ = p.shape
    c = HG_CHUNK
    idx = np.arange(tb)
    same = (idx[:, None] // c) == (idx[None, :] // c)
    tri1 = (same & (idx[None, :] <= idx[:, None])).astype(np.float32)
    tri = jnp.asarray(np.concatenate([tri1, tri1], axis=1), BF16)
    kern = functools.partial(_hgrn_kernel, tb=tb, chunk=c, heads=heads, n_cast=len(casts))
    w = heads * HEAD_DIM
    groups = HG_HEADS // heads
    ts = tb * blocks_per_step
    grid = (B, groups, T // ts)
    n_steps = grid[0] * grid[1] * grid[2]
    assert T % ts == 0 and tb % c == 0 and HG_HEADS % heads == 0
    assert all(stack.shape[1] % (16 * n_steps) == 0 for stack, _ in casts)

    def col(blk):
        return pl.BlockSpec((None, ts, w), lambda b, g, i: (b, i, blk // heads + g))

    def step(b, g, i):
        return (b * grid[1] + g) * grid[2] + i

    cast_in, cast_out, cast_shapes = [], [], []
    for stack, src_layer in casts:
        _, R, C = stack.shape
        rows = R // n_steps
        cast_in.append(pl.BlockSpec((None, rows, C),
                                    lambda b, g, i, src=src_layer: (src, step(b, g, i), 0)))
        cast_out.append(pl.BlockSpec((rows, C), lambda b, g, i: (step(b, g, i), 0)))
        cast_shapes.append(jax.ShapeDtypeStruct((R, C), BF16))

    return pl.pallas_call(
        kern,
        grid=grid,
        in_specs=[
            col(HG_Q_BLK), col(HG_F_BLK), col(HG_I_BLK), col(HG_G_BLK),
            pl.BlockSpec((None, 2, w), lambda b, g, i: (layer, 0, g)),
            pl.BlockSpec((None, 1, w), lambda b, g, i: (layer, 0, g)),
            pl.BlockSpec((tb, 2 * tb), lambda b, g, i: (0, 0)),
        ] + cast_in,
        out_specs=[pl.BlockSpec((None, ts, w), lambda b, g, i: (b, i, g))] + cast_out,
        out_shape=[jax.ShapeDtypeStruct((B, T, HG_HEADS * HEAD_DIM), BF16)] + cast_shapes,
        scratch_shapes=[pltpu.VMEM((heads, HEAD_DIM, HEAD_DIM), F32)],
        compiler_params=_params(("parallel", "parallel", "arbitrary")),
        name="hgrn2",
    )(p, p, p, p, lb_rows, gn, tri, *[stack for stack, _ in casts])


def _mix_out_kernel(ysb_ref, yhg_ref, ga0_ref, ga1_ref, gb0_ref, gb1_ref,
                    wsb_ref, whg_ref, wout_ref, x_ref, g_ref, cond_ref, o_ref):
    ysb = ysb_ref[...]
    yhg = yhg_ref[...]
    half = ga0_ref.shape[-1]
    mo = None
    for c, (ga_ref, gb_ref) in enumerate(((ga0_ref, gb0_ref), (ga1_ref, gb1_ref))):
        cols = slice(c * half, (c + 1) * half)
        a = jnp.dot(ysb, wsb_ref[:, cols], preferred_element_type=F32)
        b = jnp.dot(yhg, whg_ref[:, cols], preferred_element_type=F32)
        m = (jax.nn.sigmoid(ga_ref[...].astype(F32)) * a
             + jax.nn.sigmoid(gb_ref[...].astype(F32)) * b)
        part = jnp.dot(m.astype(BF16), wout_ref[cols, :], preferred_element_type=F32)
        mo = part if mo is None else mo + part
    o_ref[...] = x_ref[...] + cond_ref[2:3, :] * _rms(mo, g_ref[...])


def _mix_out(y_sb, y_hg, p, w_sb, w_hg, w_out, x, g, cond, layer, tm=512):
    B, T, D = x.shape
    W = y_sb.shape[-1]
    half = D // 2
    ga_blk, gb_blk = GATE_A_COL // half, GATE_B_COL // half
    assert T % tm == 0 and GATE_A_COL % half == 0 and GATE_B_COL % half == 0

    def gate(blk):
        return pl.BlockSpec((None, tm, half), lambda b, i: (b, i, blk))

    def resident(shape):
        return pl.BlockSpec(shape, lambda b, i: (0, 0), pipeline_mode=pl.Buffered(1))

    return pl.pallas_call(
        _mix_out_kernel,
        grid=(B, T // tm),
        in_specs=[
            pl.BlockSpec((None, tm, W), lambda b, i: (b, i, 0)),
            pl.BlockSpec((None, tm, W), lambda b, i: (b, i, 0)),
            gate(ga_blk), gate(ga_blk + 1), gate(gb_blk), gate(gb_blk + 1),
            resident((W, D)), resident((W, D)), resident((D, D)),
            pl.BlockSpec((None, tm, D), lambda b, i: (b, i, 0)),
            pl.BlockSpec((None, 1, D), lambda b, i: (layer, 0, 0)),
            pl.BlockSpec((None, None, 6, D), lambda b, i: (layer, b, 0, 0)),
        ],
        out_specs=pl.BlockSpec((None, tm, D), lambda b, i: (b, i, 0)),
        out_shape=jax.ShapeDtypeStruct((B, T, D), F32),
        compiler_params=_params(("parallel", "parallel")),
        name="mix_out",
    )(y_sb, y_hg, p, p, p, p, w_sb, w_hg, w_out, x, g, cond)


def _mlp_kernel(x_ref, g1_ref, g2_ref, cond_ref, wup_hbm, wdn_hbm, o_ref,
                h_ref, wup_buf, wdn_buf, sem, *, tf, sub):
    tm = h_ref.shape[0]
    n_tiles = wup_hbm.shape[1] // tf
    step = pl.program_id(0) * pl.num_programs(1) + pl.program_id(1)
    n_steps = pl.num_programs(0) * pl.num_programs(1)

    def weight_copies(j, slot):
        return (pltpu.make_async_copy(wup_hbm.at[:, pl.ds(j * tf, tf)], wup_buf.at[slot],
                                      sem.at[0, slot]),
                pltpu.make_async_copy(wdn_hbm.at[pl.ds(j * tf, tf), :], wdn_buf.at[slot],
                                      sem.at[1, slot]))

    def start(j, slot):
        for cp in weight_copies(j, slot):
            cp.start()

    def wait(j, slot):
        for cp in weight_copies(j, slot):
            cp.wait()

    @pl.when(step == 0)
    def _():
        start(0, 0)

    def accumulate(slot, rows, first):
        for c in range(tf // sub):
            cols = slice(c * sub, (c + 1) * sub)
            u = jnp.maximum(jnp.dot(h_ref[rows, :], wup_buf[slot, :, cols],
                                    preferred_element_type=F32), 0.0)
            part = jnp.dot((u * u).astype(BF16), wdn_buf[slot, cols, :],
                           preferred_element_type=F32)
            if first and c == 0:
                o_ref[rows, :] = part
            else:
                o_ref[rows, :] += part

    def finish(first_row, n_rows):
        gate = cond_ref[5:6, :]
        g2 = g2_ref[...]
        for r in range(first_row, first_row + n_rows, 32):
            rows = slice(r, r + 32)
            o_ref[rows, :] = x_ref[rows, :] + gate * _rms(o_ref[rows, :], g2)

    for j in range(n_tiles):
        slot = j % 2
        wait(j, slot)
        if j + 1 < n_tiles:
            start(j + 1, 1 - slot)
        else:
            @pl.when(step + 1 < n_steps)
            def _():
                start(0, 1 - slot)

        if j == 0:
            shift, scale = cond_ref[3:4, :], cond_ref[4:5, :]
            _norm_modulate_to(h_ref, x_ref, g1_ref, shift, scale, first_row=0,
                              n_rows=EDGE_ROWS, inline=True)
            accumulate(slot, slice(0, EDGE_ROWS), True)
            _norm_modulate_to(h_ref, x_ref, g1_ref, shift, scale, first_row=EDGE_ROWS,
                              n_rows=tm - EDGE_ROWS, inline=True)
            accumulate(slot, slice(EDGE_ROWS, tm), True)
        elif j + 1 < n_tiles:
            accumulate(slot, slice(None), False)
        else:
            half = tm // 2
            accumulate(slot, slice(0, half), False)
            finish(0, half)
            second = pl.multiple_of(jnp.minimum(step + 1, 1) * half, half)
            accumulate(slot, pl.ds(second, half), False)
            finish(half, half)


def _mlp(x, g1, g2, cond, w_up, w_dn, layer, tm=512, tf=2048, sub=1024):
    B, T, D = x.shape
    F = w_up.shape[1]
    n_tiles = F // tf
    assert T % tm == 0 and F % tf == 0 and tf % sub == 0
    assert n_tiles >= 2 and n_tiles % 2 == 0
    assert EDGE_ROWS < tm and tm % (2 * EDGE_ROWS) == 0
    return pl.pallas_call(
        functools.partial(_mlp_kernel, tf=tf, sub=sub),
        grid=(B, T // tm),
        in_specs=[
            pl.BlockSpec((None, tm, D), lambda b, i: (b, i, 0)),
            pl.BlockSpec((None, 1, D), lambda b, i: (layer, 0, 0)),
            pl.BlockSpec((None, 1, D), lambda b, i: (layer, 0, 0)),
            pl.BlockSpec((None, None, 6, D), lambda b, i: (layer, b, 0, 0)),
            pl.BlockSpec(memory_space=pl.ANY),
            pl.BlockSpec(memory_space=pl.ANY),
        ],
        out_specs=pl.BlockSpec((None, tm, D), lambda b, i: (b, i, 0)),
        out_shape=jax.ShapeDtypeStruct((B, T, D), F32),
        scratch_shapes=[
            pltpu.VMEM((tm, D), BF16),
            pltpu.VMEM((2, D, tf), BF16),
            pltpu.VMEM((2, tf, D), BF16),
            pltpu.SemaphoreType.DMA((2, 2)),
        ],
        compiler_params=_params(("arbitrary", "arbitrary")),
        name="mlp",
    )(x, g1, g2, cond, w_up, w_dn)


def kernel(x, c, w_ada, b_ada, g_pre_mix, g_post_mix, w_in, hg_lb_logits, g_hg_norm,
           w_proj_sb, w_proj_hg, w_out, g_pre_mlp, g_post_mlp, w_mlp_up, w_mlp_down):
    B, T, D = x.shape
    L = w_ada.shape[0]
    assert GATE_A_COL == (HG_G_BLK + HG_HEADS) * HEAD_DIM and GATE_B_COL == GATE_A_COL + D
    assert w_in.shape[1:] == (D, GATE_B_COL + D) and B <= 8

    lbs = jnp.cumsum(jax.nn.softmax(hg_lb_logits.astype(F32), axis=0), axis=0)
    lbs = lbs - lbs[0:1]
    lb_rows = jnp.stack([lbs, 1.0 - lbs], axis=1)

    c_pad = jnp.pad(c, ((0, 8 - B), (0, 0)))
    cond = _ada_cond(c_pad, w_ada, b_ada).reshape(L, 8, 6, D)

    g_pre_mix, g_post_mix, g_pre_mlp, g_post_mlp, g_hg_norm = (
        g.reshape(L, 1, -1) for g in (g_pre_mix, g_post_mix, g_pre_mlp, g_post_mlp, g_hg_norm))

    w_in_l = w_in[0].astype(BF16)

    for l in range(L):
        p = _in_proj(x, g_pre_mix, cond, w_in_l, l)
        y_sb = _sb_attention(p)
        casts = [(w, l) for w in (w_proj_sb, w_proj_hg, w_out, w_mlp_up, w_mlp_down)]
        casts += [(w_in, l + 1)] if l + 1 < L else []
        y_hg, w_sb_l, w_hg_l, w_out_l, w_up_l, w_dn_l, *w_in_next = _hgrn(
            p, lb_rows, g_hg_norm, casts, l)
        x = _mix_out(y_sb, y_hg, p, w_sb_l, w_hg_l, w_out_l, x, g_post_mix, cond, l)
        x = _mlp(x, g_pre_mlp, g_post_mlp, cond, w_up_l, w_dn_l, l)
        if w_in_next:
            w_in_l = w_in_next[0]
    return x
```

```python
import functools
import math

import jax
import jax.numpy as jnp
import numpy as np
from jax import lax
from jax.experimental import pallas as pl
from jax.experimental.pallas import tpu as pltpu

F32 = jnp.float32
BF16 = jnp.bfloat16

EPS = 1e-6
HEAD_DIM = 128
SB_HEADS = 8
HG_HEADS = 8
HG_CHUNK = 64

SB_Q_BLK, SB_K_BLK, SB_V_BLK = 0, 8, 16
HG_Q_BLK, HG_F_BLK, HG_I_BLK, HG_G_BLK = 24, 32, 40, 48
GATE_A_COL, GATE_B_COL = 7168, 9216

VMEM_LIMIT = 56 * 1024 * 1024

SKIP_BELOW = -104.0
LOG2E = 1.4426950408889634

EDGE_ROWS = 128

NT_DIMS = (((1,), (1,)), ((), ()))
TN_DIMS = (((0,), (0,)), ((), ()))


def _params(sem):
    return pltpu.CompilerParams(dimension_semantics=sem, vmem_limit_bytes=VMEM_LIMIT)


def _rms(xf, g):
    ms = jnp.mean(xf * xf, axis=-1, keepdims=True)
    return xf * lax.rsqrt(ms + EPS) * g


def _silu(x):
    hx = 0.5 * x
    return hx + hx * jnp.tanh(hx)


def _for_row_chunks(n_rows, chunk_rows, fn, first_row=0):
    def body(r, carry):
        fn(pl.ds(pl.multiple_of(first_row + r * chunk_rows, chunk_rows), chunk_rows))
        return carry
    lax.fori_loop(0, n_rows // chunk_rows, body, 0, unroll=4)


def _norm_modulate_to(h_ref, x_ref, g_ref, shift, scale, first_row=0, n_rows=None,
                      inline=False):
    gmod = g_ref[...] * (1.0 + scale)
    n_rows = x_ref.shape[0] if n_rows is None else n_rows

    def chunk(rows):
        x = x_ref[rows, :]
        ms = jnp.mean(x * x, axis=-1, keepdims=True)
        h_ref[rows, :] = (x * lax.rsqrt(ms + EPS) * gmod + shift).astype(h_ref.dtype)

    if inline:
        for r in range(first_row, first_row + n_rows, 32):
            chunk(slice(r, r + 32))
    else:
        _for_row_chunks(n_rows, 32, chunk, first_row)


def _split_bf16(a):
    hi = a.astype(BF16)
    lo = (a - hi.astype(F32)).astype(BF16)
    return hi, lo


def _ada_kernel(c_ref, w_ref, b_ref, o_ref):
    c = c_ref[...]
    ca = (c * jax.nn.sigmoid(c)).astype(BF16)
    o_ref[...] = jnp.dot(ca, w_ref[...].astype(BF16),
                         preferred_element_type=F32) + b_ref[...]


def _ada_cond(c_pad, w_ada, b_ada, tn=1024):
    L, D, N = w_ada.shape
    M = c_pad.shape[0]
    return pl.pallas_call(
        _ada_kernel,
        grid=(L, N // tn),
        in_specs=[
            pl.BlockSpec((M, D), lambda l, j: (0, 0)),
            pl.BlockSpec((None, D, tn), lambda l, j: (l, 0, j)),
            pl.BlockSpec((None, 1, tn), lambda l, j: (l, 0, j)),
        ],
        out_specs=pl.BlockSpec((None, M, tn), lambda l, j: (l, 0, j)),
        out_shape=jax.ShapeDtypeStruct((L, M, N), F32),
        compiler_params=_params(("parallel", "parallel")),
        name="ada_cond",
    )(c_pad, w_ada, b_ada.reshape(L, 1, N))


def _in_proj_kernel(x_ref, g_ref, cond_ref, w_ref, o_ref, h_ref):
    tm = h_ref.shape[0]

    def project(rows):
        o_ref[rows, :] = jnp.dot(h_ref[rows, :], w_ref[...],
                                 preferred_element_type=F32).astype(o_ref.dtype)

    @pl.when(pl.program_id(2) == 0)
    def _():
        shift, scale = cond_ref[0:1, :], cond_ref[1:2, :]
        _norm_modulate_to(h_ref, x_ref, g_ref, shift, scale, first_row=0, n_rows=EDGE_ROWS,
                          inline=True)
        project(slice(0, EDGE_ROWS))
        _norm_modulate_to(h_ref, x_ref, g_ref, shift, scale, first_row=EDGE_ROWS,
                          n_rows=tm - EDGE_ROWS, inline=True)
        project(slice(EDGE_ROWS, tm))

    @pl.when(pl.program_id(2) > 0)
    def _():
        project(slice(None))


def _in_proj(x, g, cond, w, layer, tm=512, tn=2816):
    B, T, D = x.shape
    N = w.shape[1]
    assert T % tm == 0 and N % tn == 0 and EDGE_ROWS < tm
    return pl.pallas_call(
        _in_proj_kernel,
        grid=(B, T // tm, N // tn),
        in_specs=[
            pl.BlockSpec((None, tm, D), lambda b, i, j: (b, i, 0)),
            pl.BlockSpec((None, 1, D), lambda b, i, j: (layer, 0, 0)),
            pl.BlockSpec((None, None, 6, D), lambda b, i, j: (layer, b, 0, 0)),
            pl.BlockSpec((D, tn), lambda b, i, j: (0, j)),
        ],
        out_specs=pl.BlockSpec((None, tm, tn), lambda b, i, j: (b, i, j)),
        out_shape=jax.ShapeDtypeStruct((B, T, N), BF16),
        scratch_shapes=[pltpu.VMEM((tm, D), BF16)],
        compiler_params=_params(("parallel", "parallel", "arbitrary")),
        name="in_proj",
    )(x, g, cond, w)


def _sb_kernel(q_ref, k_ref, v_ref, u_ref, o_ref, *, tile, heads, scale):
    tiles_per_step = q_ref.shape[0] // tile

    def body(sub, carry):
        _sb_query_tile(q_ref, k_ref, v_ref, u_ref, o_ref,
                       pl.program_id(2) * tiles_per_step + sub,
                       pl.ds(pl.multiple_of(sub * tile, tile), tile),
                       tile=tile, heads=heads, scale=scale)
        return carry

    lax.fori_loop(0, tiles_per_step, body, 0)


def _sb_query_tile(q_ref, k_ref, v_ref, u_ref, o_ref, i, q_rows, *, tile, heads, scale):
    u = u_ref[...]
    row = lax.broadcasted_iota(jnp.int32, (tile, tile), 0)
    col = lax.broadcasted_iota(jnp.int32, (tile, tile), 1)
    causal = col < row

    def tile_terms(h, j, diagonal):
        lanes = slice(h * HEAD_DIM, (h + 1) * HEAD_DIM)
        keys = pl.ds(pl.multiple_of(j * tile, tile), tile)
        s = lax.dot_general(q_ref[q_rows, lanes], k_ref[keys, lanes], NT_DIMS,
                            preferred_element_type=F32)
        e = jnp.exp2(jnp.abs(s) * (-scale * LOG2E))
        ln = jnp.maximum(s, 0.0) * (-scale) - jnp.log(1.0 + e)
        if diagonal:
            ln = jnp.where(causal, ln, 0.0)
        hi, lo = _split_bf16(ln)
        incl = jnp.dot(jnp.concatenate([hi, lo], axis=1), u, preferred_element_type=F32)
        a = jnp.exp(s * scale + incl)
        if diagonal:
            a = jnp.where(causal, a, 0.0)
        pv = jnp.dot(a.astype(BF16), v_ref[keys, lanes], preferred_element_type=F32)
        return pv, incl[:, 0:1]

    has_prev = i > 0
    jp = jnp.maximum(i - 1, 0)
    carries, accs = [], []
    for h in range(heads):
        pv_d, tot_d = tile_terms(h, i, True)
        pv_p, tot_p = tile_terms(h, jp, False)
        accs.append(pv_d + jnp.where(has_prev, jnp.exp(tot_d), 0.0) * pv_p)
        carries.append(tot_d + jnp.where(has_prev, tot_p, 0.0))

    tops = [jnp.max(c) for c in carries]
    for h in range(heads):
        def live(state):
            j, top, _, _ = state
            return (j >= 0) & (top >= SKIP_BELOW)

        def step(state, h=h):
            j, _, carry, acc = state
            pv, tot = tile_terms(h, j, False)
            carry_new = carry + tot
            return j - 1, jnp.max(carry_new), carry_new, acc + jnp.exp(carry) * pv

        _, _, _, acc = lax.while_loop(live, step, (i - 2, tops[h], carries[h], accs[h]))
        o_ref[q_rows, h * HEAD_DIM:(h + 1) * HEAD_DIM] = acc.astype(o_ref.dtype)


def _sb_attention(p, tile=256, heads=8, tiles_per_step=4):
    B, T, _ = p.shape
    idx = np.arange(tile)
    u1 = (idx[:, None] >= idx[None, :]).astype(np.float32)
    u = jnp.asarray(np.concatenate([u1, u1], axis=0), BF16)
    kern = functools.partial(_sb_kernel, tile=tile, heads=heads,
                             scale=1.0 / math.sqrt(HEAD_DIM))
    w = heads * HEAD_DIM
    groups = SB_HEADS // heads
    tq = tile * tiles_per_step
    assert T % tq == 0 and SB_HEADS % heads == 0
    return pl.pallas_call(
        kern,
        grid=(B, groups, T // tq),
        in_specs=[
            pl.BlockSpec((None, tq, w), lambda b, g, i: (b, i, SB_Q_BLK // heads + g)),
            pl.BlockSpec((None, T, w), lambda b, g, i: (b, 0, SB_K_BLK // heads + g)),
            pl.BlockSpec((None, T, w), lambda b, g, i: (b, 0, SB_V_BLK // heads + g)),
            pl.BlockSpec((2 * tile, tile), lambda b, g, i: (0, 0)),
        ],
        out_specs=pl.BlockSpec((None, tq, w), lambda b, g, i: (b, i, g)),
        out_shape=jax.ShapeDtypeStruct((B, T, SB_HEADS * HEAD_DIM), BF16),
        compiler_params=_params(("parallel", "parallel", "arbitrary")),
        name="sb_attention",
    )(p, p, p, u)


def _hgrn_kernel(hq_ref, hf_ref, hi_ref, hg_ref, lb_ref, gn_ref, tri_ref, *rest,
                 tb, chunk, heads, n_cast):
    cast_src, (o_ref, *cast_dst), st_ref = rest[:n_cast], rest[n_cast:-1], rest[-1]
    for src, dst in zip(cast_src, cast_dst):
        dst[...] = src[...].astype(dst.dtype)

    @pl.when(pl.program_id(2) == 0)
    def _():
        st_ref[...] = jnp.zeros_like(st_ref)

    def body(sub, carry):
        rows = pl.ds(pl.multiple_of(sub * tb, tb), tb)
        _hgrn_block(hq_ref, hf_ref, hi_ref, hg_ref, lb_ref, gn_ref, tri_ref, o_ref, st_ref,
                    rows, tb=tb, chunk=chunk, heads=heads)
        return carry

    lax.fori_loop(0, o_ref.shape[0] // tb, body, 0)


def _hgrn_block(hq_ref, hf_ref, hi_ref, hg_ref, lb_ref, gn_ref, tri_ref, o_ref, st_ref,
                rows, *, tb, chunk, heads):
    hf = hf_ref[rows, :].astype(F32)
    lb = lb_ref[0:1, :]
    one_m_lb = lb_ref[1:2, :]

    e = jnp.exp(-jnp.abs(hf))
    r = 1.0 / (1.0 + e)
    er = e * r
    pos = hf >= 0.0
    logf = jnp.log(lb + one_m_lb * jnp.where(pos, r, er))
    kk_all = one_m_lb * jnp.where(pos, er, r)
    qq = _silu(hq_ref[rows, :].astype(F32))

    hi, lo = _split_bf16(logf)
    g_all = jnp.dot(tri_ref[...], jnp.concatenate([hi, lo], axis=0),
                    preferred_element_type=F32)
    chunks = [slice(n * chunk, (n + 1) * chunk) for n in range(tb // chunk)]
    mids = [g_all[sl.start + chunk // 2 - 1:sl.start + chunk // 2, :] for sl in chunks]
    ends = [g_all[sl.stop - 1:sl.stop, :] for sl in chunks]

    def per_chunk(chunk_rows):
        return jnp.concatenate(
            [jnp.broadcast_to(r, (chunk, r.shape[1])) for r in chunk_rows], axis=0)

    rel = g_all - per_chunk(mids)
    q_mid = qq * jnp.exp(rel)
    k_mid = kk_all * jnp.exp(-rel)
    qg_all = q_mid.astype(BF16)
    kg_all = k_mid.astype(BF16)
    q0_all = (q_mid * per_chunk([jnp.exp(m) for m in mids])).astype(BF16)
    kdec_all = (k_mid * per_chunk([jnp.exp(e - m) for e, m in zip(ends, mids)])).astype(BF16)
    decays = [jnp.exp(e) for e in ends]

    row = lax.broadcasted_iota(jnp.int32, (tb, tb), 0)
    col = lax.broadcasted_iota(jnp.int32, (tb, tb), 1)
    keep = (col <= row) & ((row // chunk) == (col // chunk))

    for h in range(heads):
        lanes = slice(h * HEAD_DIM, (h + 1) * HEAD_DIM)
        q0, kdec = q0_all[:, lanes], kdec_all[:, lanes]
        v = hi_ref[rows, lanes]

        scores = lax.dot_general(qg_all[:, lanes], kg_all[:, lanes], NT_DIMS,
                                 preferred_element_type=F32)
        scores = jnp.where(keep, scores, 0.0).astype(BF16)
        o = jnp.dot(scores, v, preferred_element_type=F32)

        upds = [lax.dot_general(v[sl], kdec[sl], TN_DIMS, preferred_element_type=F32)
                for sl in chunks]

        st = st_ref[h]
        inter = []
        for sl, upd, dec in zip(chunks, upds, decays):
            inter.append(lax.dot_general(q0[sl], st.astype(BF16), NT_DIMS,
                                         preferred_element_type=F32))
            st = dec[:, lanes] * st + upd
        st_ref[h] = st

        o = o + jnp.concatenate(inter, axis=0)
        y = _rms(o, gn_ref[:, lanes]) * _silu(hg_ref[rows, lanes].astype(F32))
        o_ref[rows, lanes] = y.astype(o_ref.dtype)


def _hgrn(p, lb_rows, gn, casts, layer, tb=256, heads=8, blocks_per_step=2):
    B, T, _ = p.shape
    c = HG_CHUNK
    idx = np.arange(tb)
    same = (idx[:, None] // c) == (idx[None, :] // c)
    tri1 = (same & (idx[None, :] <= idx[:, None])).astype(np.float32)
    tri = jnp.asarray(np.concatenate([tri1, tri1], axis=1), BF16)
    kern = functools.partial(_hgrn_kernel, tb=tb, chunk=c, heads=heads, n_cast=len(casts))
    w = heads * HEAD_DIM
    groups = HG_HEADS // heads
    ts = tb * blocks_per_step
    grid = (B, groups, T // ts)
    n_steps = grid[0] * grid[1] * grid[2]
    assert T % ts == 0 and tb % c == 0 and HG_HEADS % heads == 0
    assert all(stack.shape[1] % (16 * n_steps) == 0 for stack, _ in casts)

    def col(blk):
        return pl.BlockSpec((None, ts, w), lambda b, g, i: (b, i, blk // heads + g))

    def step(b, g, i):
        return (b * grid[1] + g) * grid[2] + i

    cast_in, cast_out, cast_shapes = [], [], []
    for stack, src_layer in casts:
        _, R, C = stack.shape
        rows = R // n_steps
        cast_in.append(pl.BlockSpec((None, rows, C),
                                    lambda b, g, i, src=src_layer: (src, step(b, g, i), 0)))
        cast_out.append(pl.BlockSpec((rows, C), lambda b, g, i: (step(b, g, i), 0)))
        cast_shapes.append(jax.ShapeDtypeStruct((R, C), BF16))

    return pl.pallas_call(
        kern,
        grid=grid,
        in_specs=[
            col(HG_Q_BLK), col(HG_F_BLK), col(HG_I_BLK), col(HG_G_BLK),
            pl.BlockSpec((None, 2, w), lambda b, g, i: (layer, 0, g)),
            pl.BlockSpec((None, 1, w), lambda b, g, i: (layer, 0, g)),
            pl.BlockSpec((tb, 2 * tb), lambda b, g, i: (0, 0)),
        ] + cast_in,
        out_specs=[pl.BlockSpec((None, ts, w), lambda b, g, i: (b, i, g))] + cast_out,
        out_shape=[jax.ShapeDtypeStruct((B, T, HG_HEADS * HEAD_DIM), BF16)] + cast_shapes,
        scratch_shapes=[pltpu.VMEM((heads, HEAD_DIM, HEAD_DIM), F32)],
        compiler_params=_params(("parallel", "parallel", "arbitrary")),
        name="hgrn2",
    )(p, p, p, p, lb_rows, gn, tri, *[stack for stack, _ in casts])


def _mix_out_kernel(ysb_ref, yhg_ref, ga0_ref, ga1_ref, gb0_ref, gb1_ref,
                    wsb_ref, whg_ref, wout_ref, x_ref, g_ref, cond_ref, o_ref):
    ysb = ysb_ref[...]
    yhg = yhg_ref[...]
    half = ga0_ref.shape[-1]
    mo = None
    for c, (ga_ref, gb_ref) in enumerate(((ga0_ref, gb0_ref), (ga1_ref, gb1_ref))):
        cols = slice(c * half, (c + 1) * half)
        a = jnp.dot(ysb, wsb_ref[:, cols], preferred_element_type=F32)
        b = jnp.dot(yhg, whg_ref[:, cols], preferred_element_type=F32)
        m = (jax.nn.sigmoid(ga_ref[...].astype(F32)) * a
             + jax.nn.sigmoid(gb_ref[...].astype(F32)) * b)
        part = jnp.dot(m.astype(BF16), wout_ref[cols, :], preferred_element_type=F32)
        mo = part if mo is None else mo + part
    o_ref[...] = x_ref[...] + cond_ref[2:3, :] * _rms(mo, g_ref[...])


def _mix_out(y_sb, y_hg, p, w_sb, w_hg, w_out, x, g, cond, layer, tm=512):
    B, T, D = x.shape
    W = y_sb.shape[-1]
    half = D // 2
    ga_blk, gb_blk = GATE_A_COL // half, GATE_B_COL // half
    assert T % tm == 0 and GATE_A_COL % half == 0 and GATE_B_COL % half == 0

    def gate(blk):
        return pl.BlockSpec((None, tm, half), lambda b, i: (b, i, blk))

    def resident(shape):
        return pl.BlockSpec(shape, lambda b, i: (0, 0), pipeline_mode=pl.Buffered(1))

    return pl.pallas_call(
        _mix_out_kernel,
        grid=(B, T // tm),
        in_specs=[
            pl.BlockSpec((None, tm, W), lambda b, i: (b, i, 0)),
            pl.BlockSpec((None, tm, W), lambda b, i: (b, i, 0)),
            gate(ga_blk), gate(ga_blk + 1), gate(gb_blk), gate(gb_blk + 1),
            resident((W, D)), resident((W, D)), resident((D, D)),
            pl.BlockSpec((None, tm, D), lambda b, i: (b, i, 0)),
            pl.BlockSpec((None, 1, D), lambda b, i: (layer, 0, 0)),
            pl.BlockSpec((None, None, 6, D), lambda b, i: (layer, b, 0, 0)),
        ],
        out_specs=pl.BlockSpec((None, tm, D), lambda b, i: (b, i, 0)),
        out_shape=jax.ShapeDtypeStruct((B, T, D), F32),
        compiler_params=_params(("parallel", "parallel")),
        name="mix_out",
    )(y_sb, y_hg, p, p, p, p, w_sb, w_hg, w_out, x, g, cond)


def _mlp_kernel(x_ref, g1_ref, g2_ref, cond_ref, wup_hbm, wdn_hbm, o_ref,
                h_ref, wup_buf, wdn_buf, sem, *, tf, sub):
    tm = h_ref.shape[0]
    n_tiles = wup_hbm.shape[1] // tf
    step = pl.program_id(0) * pl.num_programs(1) + pl.program_id(1)
    n_steps = pl.num_programs(0) * pl.num_programs(1)

    def weight_copies(j, slot):
        return (pltpu.make_async_copy(wup_hbm.at[:, pl.ds(j * tf, tf)], wup_buf.at[slot],
                                      sem.at[0, slot]),
                pltpu.make_async_copy(wdn_hbm.at[pl.ds(j * tf, tf), :], wdn_buf.at[slot],
                                      sem.at[1, slot]))

    def start(j, slot):
        for cp in weight_copies(j, slot):
            cp.start()

    def wait(j, slot):
        for cp in weight_copies(j, slot):
            cp.wait()

    @pl.when(step == 0)
    def _():
        start(0, 0)

    def accumulate(slot, rows, first):
        for c in range(tf // sub):
            cols = slice(c * sub, (c + 1) * sub)
            u = jnp.maximum(jnp.dot(h_ref[rows, :], wup_buf[slot, :, cols],
                                    preferred_element_type=F32), 0.0)
            part = jnp.dot((u * u).astype(BF16), wdn_buf[slot, cols, :],
                           preferred_element_type=F32)
            if first and c == 0:
                o_ref[rows, :] = part
            else:
                o_ref[rows, :] += part

    def finish(first_row, n_rows):
        gate = cond_ref[5:6, :]
        g2 = g2_ref[...]
        for r in range(first_row, first_row + n_rows, 32):
            rows = slice(r, r + 32)
            o_ref[rows, :] = x_ref[rows, :] + gate * _rms(o_ref[rows, :], g2)

    for j in range(n_tiles):
        slot = j % 2
        wait(j, slot)
        if j + 1 < n_tiles:
            start(j + 1, 1 - slot)
        else:
            @pl.when(step + 1 < n_steps)
            def _():
                start(0, 1 - slot)

        if j == 0:
            shift, scale = cond_ref[3:4, :], cond_ref[4:5, :]
            _norm_modulate_to(h_ref, x_ref, g1_ref, shift, scale, first_row=0,
                              n_rows=EDGE_ROWS, inline=True)
            accumulate(slot, slice(0, EDGE_ROWS), True)
            _norm_modulate_to(h_ref, x_ref, g1_ref, shift, scale, first_row=EDGE_ROWS,
                              n_rows=tm - EDGE_ROWS, inline=True)
            accumulate(slot, slice(EDGE_ROWS, tm), True)
        elif j + 1 < n_tiles:
            accumulate(slot, slice(None), False)
        else:
            half = tm // 2
            accumulate(slot, slice(0, half), False)
            finish(0, half)
            second = pl.multiple_of(jnp.minimum(step + 1, 1) * half, half)
            accumulate(slot, pl.ds(second, half), False)
            finish(half, half)


def _mlp(x, g1, g2, cond, w_up, w_dn, layer, tm=512, tf=2048, sub=1024):
    B, T, D = x.shape
    F = w_up.shape[1]
    n_tiles = F // tf
    assert T % tm == 0 and F % tf == 0 and tf % sub == 0
    assert n_tiles >= 2 and n_tiles % 2 == 0
    assert EDGE_ROWS < tm and tm % (2 * EDGE_ROWS) == 0
    return pl.pallas_call(
        functools.partial(_mlp_kernel, tf=tf, sub=sub),
        grid=(B, T // tm),
        in_specs=[
            pl.BlockSpec((None, tm, D), lambda b, i: (b, i, 0)),
            pl.BlockSpec((None, 1, D), lambda b, i: (layer, 0, 0)),
            pl.BlockSpec((None, 1, D), lambda b, i: (layer, 0, 0)),
            pl.BlockSpec((None, None, 6, D), lambda b, i: (layer, b, 0, 0)),
            pl.BlockSpec(memory_space=pl.ANY),
            pl.BlockSpec(memory_space=pl.ANY),
        ],
        out_specs=pl.BlockSpec((None, tm, D), lambda b, i: (b, i, 0)),
        out_shape=jax.ShapeDtypeStruct((B, T, D), F32),
        scratch_shapes=[
            pltpu.VMEM((tm, D), BF16),
            pltpu.VMEM((2, D, tf), BF16),
            pltpu.VMEM((2, tf, D), BF16),
            pltpu.SemaphoreType.DMA((2, 2)),
        ],
        compiler_params=_params(("arbitrary", "arbitrary")),
        name="mlp",
    )(x, g1, g2, cond, w_up, w_dn)


def kernel(x, c, w_ada, b_ada, g_pre_mix, g_post_mix, w_in, hg_lb_logits, g_hg_norm,
           w_proj_sb, w_proj_hg, w_out, g_pre_mlp, g_post_mlp, w_mlp_up, w_mlp_down):
    B, T, D = x.shape
    L = w_ada.shape[0]
    assert GATE_A_COL == (HG_G_BLK + HG_HEADS) * HEAD_DIM and GATE_B_COL == GATE_A_COL + D
    assert w_in.shape[1:] == (D, GATE_B_COL + D) and B <= 8

    lbs = jnp.cumsum(jax.nn.softmax(hg_lb_logits.astype(F32), axis=0), axis=0)
    lbs = lbs - lbs[0:1]
    lb_rows = jnp.stack([lbs, 1.0 - lbs], axis=1)

    c_pad = jnp.pad(c, ((0, 8 - B), (0, 0)))
    cond = _ada_cond(c_pad, w_ada, b_ada).reshape(L, 8, 6, D)

    g_pre_mix, g_post_mix, g_pre_mlp, g_post_mlp, g_hg_norm = (
        g.reshape(L, 1, -1) for g in (g_pre_mix, g_post_mix, g_pre_mlp, g_post_mlp, g_hg_norm))

    w_in_l = w_in[0].astype(BF16)

    for l in range(L):
        p = _in_proj(x, g_pre_mix, cond, w_in_l, l)
        y_sb = _sb_attention(p)
        casts = [(w, l) for w in (w_proj_sb, w_proj_hg, w_out, w_mlp_up, w_mlp_down)]
        casts += [(w_in, l + 1)] if l + 1 < L else []
        y_hg, w_sb_l, w_hg_l, w_out_l, w_up_l, w_dn_l, *w_in_next = _hgrn(
            p, lb_rows, g_hg_norm, casts, l)
        x = _mix_out(y_sb, y_hg, p, w_sb_l, w_hg_l, w_out_l, x, g_post_mix, cond, l)
        x = _mlp(x, g_pre_mlp, g_post_mlp, cond, w_up_l, w_dn_l, l)
        if w_in_next:
            w_in_l = w_in_next[0]
    return x
```

```python
import functools
import math

import jax
import jax.numpy as jnp
import numpy as np
from jax import lax
from jax.experimental import pallas as pl
from jax.experimental.pallas import tpu as pltpu

F32 = jnp.float32
BF16 = jnp.bfloat16

EPS = 1e-6
HEAD_DIM = 128
SB_HEADS = 8
HG_HEADS = 8
HG_CHUNK = 64

SB_Q_BLK, SB_K_BLK, SB_V_BLK = 0, 8, 16
HG_Q_BLK, HG_F_BLK, HG_I_BLK, HG_G_BLK = 24, 32, 40, 48
GATE_A_COL, GATE_B_COL = 7168, 9216

VMEM_LIMIT = 56 * 1024 * 1024

SKIP_BELOW = -104.0
LOG2E = 1.4426950408889634

EDGE_ROWS = 128

NT_DIMS = (((1,), (1,)), ((), ()))
TN_DIMS = (((0,), (0,)), ((), ()))


def _params(sem):
    return pltpu.CompilerParams(dimension_semantics=sem, vmem_limit_bytes=VMEM_LIMIT)


def _rms(xf, g):
    ms = jnp.mean(xf * xf, axis=-1, keepdims=True)
    return xf * lax.rsqrt(ms + EPS) * g


def _silu(x):
    hx = 0.5 * x
    return hx + hx * jnp.tanh(hx)


def _for_row_chunks(n_rows, chunk_rows, fn, first_row=0):
    def body(r, carry):
        fn(pl.ds(pl.multiple_of(first_row + r * chunk_rows, chunk_rows), chunk_rows))
        return carry
    lax.fori_loop(0, n_rows // chunk_rows, body, 0, unroll=4)


def _norm_modulate_to(h_ref, x_ref, g_ref, shift, scale, first_row=0, n_rows=None,
                      inline=False):
    gmod = g_ref[...] * (1.0 + scale)
    n_rows = x_ref.shape[0] if n_rows is None else n_rows

    def chunk(rows):
        x = x_ref[rows, :]
        ms = jnp.mean(x * x, axis=-1, keepdims=True)
        h_ref[rows, :] = (x * lax.rsqrt(ms + EPS) * gmod + shift).astype(h_ref.dtype)

    if inline:
        for r in range(first_row, first_row + n_rows, 32):
            chunk(slice(r, r + 32))
    else:
        _for_row_chunks(n_rows, 32, chunk, first_row)


def _split_bf16(a):
    hi = a.astype(BF16)
    lo = (a - hi.astype(F32)).astype(BF16)
    return hi, lo


def _ada_kernel(c_ref, w_ref, b_ref, o_ref):
    c = c_ref[...]
    ca = (c * jax.nn.sigmoid(c)).astype(BF16)
    o_ref[...] = jnp.dot(ca, w_ref[...].astype(BF16),
                         preferred_element_type=F32) + b_ref[...]


def _ada_cond(c_pad, w_ada, b_ada, tn=1024):
    L, D, N = w_ada.shape
    M = c_pad.shape[0]
    return pl.pallas_call(
        _ada_kernel,
        grid=(L, N // tn),
        in_specs=[
            pl.BlockSpec((M, D), lambda l, j: (0, 0)),
            pl.BlockSpec((None, D, tn), lambda l, j: (l, 0, j)),
            pl.BlockSpec((None, 1, tn), lambda l, j: (l, 0, j)),
        ],
        out_specs=pl.BlockSpec((None, M, tn), lambda l, j: (l, 0, j)),
        out_shape=jax.ShapeDtypeStruct((L, M, N), F32),
        compiler_params=_params(("parallel", "parallel")),
        name="ada_cond",
    )(c_pad, w_ada, b_ada.reshape(L, 1, N))


def _in_proj_kernel(x_ref, g_ref, cond_ref, w_ref, o_ref, h_ref):
    tm = h_ref.shape[0]

    def project(rows):
        o_ref[rows, :] = jnp.dot(h_ref[rows, :], w_ref[...],
                                 preferred_element_type=F32).astype(o_ref.dtype)

    @pl.when(pl.program_id(2) == 0)
    def _():
        shift, scale = cond_ref[0:1, :], cond_ref[1:2, :]
        _norm_modulate_to(h_ref, x_ref, g_ref, shift, scale, first_row=0, n_rows=EDGE_ROWS,
                          inline=True)
        project(slice(0, EDGE_ROWS))
        _norm_modulate_to(h_ref, x_ref, g_ref, shift, scale, first_row=EDGE_ROWS,
                          n_rows=tm - EDGE_ROWS, inline=True)
        project(slice(EDGE_ROWS, tm))

    @pl.when(pl.program_id(2) > 0)
    def _():
        project(slice(None))


def _in_proj(x, g, cond, w, layer, tm=512, tn=2816):
    B, T, D = x.shape
    N = w.shape[1]
    assert T % tm == 0 and N % tn == 0 and EDGE_ROWS < tm
    return pl.pallas_call(
        _in_proj_kernel,
        grid=(B, T // tm, N // tn),
        in_specs=[
            pl.BlockSpec((None, tm, D), lambda b, i, j: (b, i, 0)),
            pl.BlockSpec((None, 1, D), lambda b, i, j: (layer, 0, 0)),
            pl.BlockSpec((None, None, 6, D), lambda b, i, j: (layer, b, 0, 0)),
            pl.BlockSpec((D, tn), lambda b, i, j: (0, j)),
        ],
        out_specs=pl.BlockSpec((None, tm, tn), lambda b, i, j: (b, i, j)),
        out_shape=jax.ShapeDtypeStruct((B, T, N), BF16),
        scratch_shapes=[pltpu.VMEM((tm, D), BF16)],
        compiler_params=_params(("parallel", "parallel", "arbitrary")),
        name="in_proj",
    )(x, g, cond, w)


def _sb_kernel(q_ref, k_ref, v_ref, u_ref, o_ref, *, tile, heads, scale):
    tiles_per_step = q_ref.shape[0] // tile

    def body(sub, carry):
        _sb_query_tile(q_ref, k_ref, v_ref, u_ref, o_ref,
                       pl.program_id(2) * tiles_per_step + sub,
                       pl.ds(pl.multiple_of(sub * tile, tile), tile),
                       tile=tile, heads=heads, scale=scale)
        return carry

    lax.fori_loop(0, tiles_per_step, body, 0)


def _sb_query_tile(q_ref, k_ref, v_ref, u_ref, o_ref, i, q_rows, *, tile, heads, scale):
    u = u_ref[...]
    row = lax.broadcasted_iota(jnp.int32, (tile, tile), 0)
    col = lax.broadcasted_iota(jnp.int32, (tile, tile), 1)
    causal = col < row

    def tile_terms(h, j, diagonal):
        lanes = slice(h * HEAD_DIM, (h + 1) * HEAD_DIM)
        keys = pl.ds(pl.multiple_of(j * tile, tile), tile)
        s = lax.dot_general(q_ref[q_rows, lanes], k_ref[keys, lanes], NT_DIMS,
                            preferred_element_type=F32)
        e = jnp.exp2(jnp.abs(s) * (-scale * LOG2E))
        ln = jnp.maximum(s, 0.0) * (-scale) - jnp.log(1.0 + e)
        if diagonal:
            ln = jnp.where(causal, ln, 0.0)
        hi, lo = _split_bf16(ln)
        incl = jnp.dot(jnp.concatenate([hi, lo], axis=1), u, preferred_element_type=F32)
        a = jnp.exp(s * scale + incl)
        if diagonal:
            a = jnp.where(causal, a, 0.0)
        pv = jnp.dot(a.astype(BF16), v_ref[keys, lanes], preferred_element_type=F32)
        return pv, incl[:, 0:1]

    has_prev = i > 0
    jp = jnp.maximum(i - 1, 0)
    carries, accs = [], []
    for h in range(heads):
        pv_d, tot_d = tile_terms(h, i, True)
        pv_p, tot_p = tile_terms(h, jp, False)
        accs.append(pv_d + jnp.where(has_prev, jnp.exp(tot_d), 0.0) * pv_p)
        carries.append(tot_d + jnp.where(has_prev, tot_p, 0.0))

    tops = [jnp.max(c) for c in carries]
    for h in range(heads):
        def live(state):
            j, top, _, _ = state
            return (j >= 0) & (top >= SKIP_BELOW)

        def step(state, h=h):
            j, _, carry, acc = state
            pv, tot = tile_terms(h, j, False)
            carry_new = carry + tot
            return j - 1, jnp.max(carry_new), carry_new, acc + jnp.exp(carry) * pv

        _, _, _, acc = lax.while_loop(live, step, (i - 2, tops[h], carries[h], accs[h]))
        o_ref[q_rows, h * HEAD_DIM:(h + 1) * HEAD_DIM] = acc.astype(o_ref.dtype)


def _sb_attention(p, tile=256, heads=8, tiles_per_step=4):
    B, T, _ = p.shape
    idx = np.arange(tile)
    u1 = (idx[:, None] >= idx[None, :]).astype(np.float32)
    u = jnp.asarray(np.concatenate([u1, u1], axis=0), BF16)
    kern = functools.partial(_sb_kernel, tile=tile, heads=heads,
                             scale=1.0 / math.sqrt(HEAD_DIM))
    w = heads * HEAD_DIM
    groups = SB_HEADS // heads
    tq = tile * tiles_per_step
    assert T % tq == 0 and SB_HEADS % heads == 0
    return pl.pallas_call(
        kern,
        grid=(B, groups, T // tq),
        in_specs=[
            pl.BlockSpec((None, tq, w), lambda b, g, i: (b, i, SB_Q_BLK // heads + g)),
            pl.BlockSpec((None, T, w), lambda b, g, i: (b, 0, SB_K_BLK // heads + g)),
            pl.BlockSpec((None, T, w), lambda b, g, i: (b, 0, SB_V_BLK // heads + g)),
            pl.BlockSpec((2 * tile, tile), lambda b, g, i: (0, 0)),
        ],
        out_specs=pl.BlockSpec((None, tq, w), lambda b, g, i: (b, i, g)),
        out_shape=jax.ShapeDtypeStruct((B, T, SB_HEADS * HEAD_DIM), BF16),
        compiler_params=_params(("parallel", "parallel", "arbitrary")),
        name="sb_attention",
    )(p, p, p, u)


def _hgrn_kernel(hq_ref, hf_ref, hi_ref, hg_ref, lb_ref, gn_ref, tri_ref, *rest,
                 tb, chunk, heads, n_cast):
    cast_src, (o_ref, *cast_dst), st_ref = rest[:n_cast], rest[n_cast:-1], rest[-1]
    for src, dst in zip(cast_src, cast_dst):
        dst[...] = src[...].astype(dst.dtype)

    @pl.when(pl.program_id(2) == 0)
    def _():
        st_ref[...] = jnp.zeros_like(st_ref)

    def body(sub, carry):
        rows = pl.ds(pl.multiple_of(sub * tb, tb), tb)
        _hgrn_block(hq_ref, hf_ref, hi_ref, hg_ref, lb_ref, gn_ref, tri_ref, o_ref, st_ref,
                    rows, tb=tb, chunk=chunk, heads=heads)
        return carry

    lax.fori_loop(0, o_ref.shape[0] // tb, body, 0)


def _hgrn_block(hq_ref, hf_ref, hi_ref, hg_ref, lb_ref, gn_ref, tri_ref, o_ref, st_ref,
                rows, *, tb, chunk, heads):
    hf = hf_ref[rows, :].astype(F32)
    lb = lb_ref[0:1, :]
    one_m_lb = lb_ref[1:2, :]

    e = jnp.exp(-jnp.abs(hf))
    r = 1.0 / (1.0 + e)
    er = e * r
    pos = hf >= 0.0
    logf = jnp.log(lb + one_m_lb * jnp.where(pos, r, er))
    kk_all = one_m_lb * jnp.where(pos, er, r)
    qq = _silu(hq_ref[rows, :].astype(F32))

    hi, lo = _split_bf16(logf)
    g_all = jnp.dot(tri_ref[...], jnp.concatenate([hi, lo], axis=0),
                    preferred_element_type=F32)
    chunks = [slice(n * chunk, (n + 1) * chunk) for n in range(tb // chunk)]
    mids = [g_all[sl.start + chunk // 2 - 1:sl.start + chunk // 2, :] for sl in chunks]
    ends = [g_all[sl.stop - 1:sl.stop, :] for sl in chunks]

    def per_chunk(chunk_rows):
        return jnp.concatenate(
            [jnp.broadcast_to(r, (chunk, r.shape[1])) for r in chunk_rows], axis=0)

    rel = g_all - per_chunk(mids)
    q_mid = qq * jnp.exp(rel)
    k_mid = kk_all * jnp.exp(-rel)
    qg_all = q_mid.astype(BF16)
    kg_all = k_mid.astype(BF16)
    q0_all = (q_mid * per_chunk([jnp.exp(m) for m in mids])).astype(BF16)
    kdec_all = (k_mid * per_chunk([jnp.exp(e - m) for e, m in zip(ends, mids)])).astype(BF16)
    decays = [jnp.exp(e) for e in ends]

    row = lax.broadcasted_iota(jnp.int32, (tb, tb), 0)
    col = lax.broadcasted_iota(jnp.int32, (tb, tb), 1)
    keep = (col <= row) & ((row // chunk) == (col // chunk))

    for h in range(heads):
        lanes = slice(h * HEAD_DIM, (h + 1) * HEAD_DIM)
        q0, kdec = q0_all[:, lanes], kdec_all[:, lanes]
        v = hi_ref[rows, lanes]

        upds = [lax.dot_general(v[sl], kdec[sl], TN_DIMS, preferred_element_type=F32)
                for sl in chunks]

        scores = lax.dot_general(qg_all[:, lanes], kg_all[:, lanes], NT_DIMS,
                                 preferred_element_type=F32)
        scores = jnp.where(keep, scores, 0.0).astype(BF16)
        o = jnp.dot(scores, v, preferred_element_type=F32)

        st = st_ref[h]
        inter = []
        for sl, upd, dec in zip(chunks, upds, decays):
            inter.append(lax.dot_general(q0[sl], st.astype(BF16), NT_DIMS,
                                         preferred_element_type=F32))
            st = dec[:, lanes] * st + upd
        st_ref[h] = st

        o = o + jnp.concatenate(inter, axis=0)
        y = _rms(o, gn_ref[:, lanes]) * _silu(hg_ref[rows, lanes].astype(F32))
        o_ref[rows, lanes] = y.astype(o_ref.dtype)


def _hgrn(p, lb_rows, gn, casts, layer, tb=256, heads=8, blocks_per_step=2):
    B, T, _ = p.shape
    c = HG_CHUNK
    idx = np.arange(tb)
    same = (idx[:, None] // c) == (idx[None, :] // c)
    tri1 = (same & (idx[None, :] <= idx[:, None])).astype(np.float32)
    tri = jnp.asarray(np.concatenate([tri1, tri1], axis=1), BF16)
    kern = functools.partial(_hgrn_kernel, tb=tb, chunk=c, heads=heads, n_cast=len(casts))
    w = heads * HEAD_DIM
    groups = HG_HEADS // heads
    ts = tb * blocks_per_step
    grid = (B, groups, T // ts)
    n_steps = grid[0] * grid[1] * grid[2]
    assert T % ts == 0 and tb % c == 0 and HG_HEADS % heads == 0
    assert all(stack.shape[1] % (16 * n_steps) == 0 for stack, _ in casts)

    def col(blk):
        return pl.BlockSpec((None, ts, w), lambda b, g, i: (b, i, blk // heads + g))

    def step(b, g, i):
        return (b * grid[1] + g) * grid[2] + i

    cast_in, cast_out, cast_shapes = [], [], []
    for stack, src_layer in casts:
        _, R, C = stack.shape
        rows = R // n_steps
        cast_in.append(pl.BlockSpec((None, rows, C),
                                    lambda b, g, i, src=src_layer: (src, step(b, g, i), 0)))
        cast_out.append(pl.BlockSpec((rows, C), lambda b, g, i: (step(b, g, i), 0)))
        cast_shapes.append(jax.ShapeDtypeStruct((R, C), BF16))

    return pl.pallas_call(
        kern,
        grid=grid,
        in_specs=[
            col(HG_Q_BLK), col(HG_F_BLK), col(HG_I_BLK), col(HG_G_BLK),
            pl.BlockSpec((None, 2, w), lambda b, g, i: (layer, 0, g)),
            pl.BlockSpec((None, 1, w), lambda b, g, i: (layer, 0, g)),
            pl.BlockSpec((tb, 2 * tb), lambda b, g, i: (0, 0)),
        ] + cast_in,
        out_specs=[pl.BlockSpec((None, ts, w), lambda b, g, i: (b, i, g))] + cast_out,
        out_shape=[jax.ShapeDtypeStruct((B, T, HG_HEADS * HEAD_DIM), BF16)] + cast_shapes,
        scratch_shapes=[pltpu.VMEM((heads, HEAD_DIM, HEAD_DIM), F32)],
        compiler_params=_params(("parallel", "parallel", "arbitrary")),
        name="hgrn2",
    )(p, p, p, p, lb_rows, gn, tri, *[stack for stack, _ in casts])


def _mix_out_kernel(ysb_ref, yhg_ref, ga0_ref, ga1_ref, gb0_ref, gb1_ref,
                    wsb_ref, whg_ref, wout_ref, x_ref, g_ref, cond_ref, o_ref):
    ysb = ysb_ref[...]
    yhg = yhg_ref[...]
    half = ga0_ref.shape[-1]
    mo = None
    for c, (ga_ref, gb_ref) in enumerate(((ga0_ref, gb0_ref), (ga1_ref, gb1_ref))):
        cols = slice(c * half, (c + 1) * half)
        a = jnp.dot(ysb, wsb_ref[:, cols], preferred_element_type=F32)
        b = jnp.dot(yhg, whg_ref[:, cols], preferred_element_type=F32)
        m = (jax.nn.sigmoid(ga_ref[...].astype(F32)) * a
             + jax.nn.sigmoid(gb_ref[...].astype(F32)) * b)
        part = jnp.dot(m.astype(BF16), wout_ref[cols, :], preferred_element_type=F32)
        mo = part if mo is None else mo + part
    o_ref[...] = x_ref[...] + cond_ref[2:3, :] * _rms(mo, g_ref[...])


def _mix_out(y_sb, y_hg, p, w_sb, w_hg, w_out, x, g, cond, layer, tm=512):
    B, T, D = x.shape
    W = y_sb.shape[-1]
    half = D // 2
    ga_blk, gb_blk = GATE_A_COL // half, GATE_B_COL // half
    assert T % tm == 0 and GATE_A_COL % half == 0 and GATE_B_COL % half == 0

    def gate(blk):
        return pl.BlockSpec((None, tm, half), lambda b, i: (b, i, blk))

    def resident(shape):
        return pl.BlockSpec(shape, lambda b, i: (0, 0), pipeline_mode=pl.Buffered(1))

    return pl.pallas_call(
        _mix_out_kernel,
        grid=(B, T // tm),
        in_specs=[
            pl.BlockSpec((None, tm, W), lambda b, i: (b, i, 0)),
            pl.BlockSpec((None, tm, W), lambda b, i: (b, i, 0)),
            gate(ga_blk), gate(ga_blk + 1), gate(gb_blk), gate(gb_blk + 1),
            resident((W, D)), resident((W, D)), resident((D, D)),
            pl.BlockSpec((None, tm, D), lambda b, i: (b, i, 0)),
            pl.BlockSpec((None, 1, D), lambda b, i: (layer, 0, 0)),
            pl.BlockSpec((None, None, 6, D), lambda b, i: (layer, b, 0, 0)),
        ],
        out_specs=pl.BlockSpec((None, tm, D), lambda b, i: (b, i, 0)),
        out_shape=jax.ShapeDtypeStruct((B, T, D), F32),
        compiler_params=_params(("parallel", "parallel")),
        name="mix_out",
    )(y_sb, y_hg, p, p, p, p, w_sb, w_hg, w_out, x, g, cond)


def _mlp_kernel(x_ref, g1_ref, g2_ref, cond_ref, wup_hbm, wdn_hbm, o_ref,
                h_ref, wup_buf, wdn_buf, sem, *, tf, sub):
    tm = h_ref.shape[0]
    n_tiles = wup_hbm.shape[1] // tf
    step = pl.program_id(0) * pl.num_programs(1) + pl.program_id(1)
    n_steps = pl.num_programs(0) * pl.num_programs(1)

    def weight_copies(j, slot):
        return (pltpu.make_async_copy(wup_hbm.at[:, pl.ds(j * tf, tf)], wup_buf.at[slot],
                                      sem.at[0, slot]),
                pltpu.make_async_copy(wdn_hbm.at[pl.ds(j * tf, tf), :], wdn_buf.at[slot],
                                      sem.at[1, slot]))

    def start(j, slot):
        for cp in weight_copies(j, slot):
            cp.start()

    def wait(j, slot):
        for cp in weight_copies(j, slot):
            cp.wait()

    @pl.when(step == 0)
    def _():
        start(0, 0)

    def accumulate(slot, rows, first):
        for c in range(tf // sub):
            cols = slice(c * sub, (c + 1) * sub)
            u = jnp.maximum(jnp.dot(h_ref[rows, :], wup_buf[slot, :, cols],
                                    preferred_element_type=F32), 0.0)
            part = jnp.dot((u * u).astype(BF16), wdn_buf[slot, cols, :],
                           preferred_element_type=F32)
            if first and c == 0:
                o_ref[rows, :] = part
            else:
                o_ref[rows, :] += part

    def finish(first_row, n_rows):
        gate = cond_ref[5:6, :]
        g2 = g2_ref[...]
        for r in range(first_row, first_row + n_rows, 32):
            rows = slice(r, r + 32)
            o_ref[rows, :] = x_ref[rows, :] + gate * _rms(o_ref[rows, :], g2)

    for j in range(n_tiles):
        slot = j % 2
        wait(j, slot)
        if j + 1 < n_tiles:
            start(j + 1, 1 - slot)
        else:
            @pl.when(step + 1 < n_steps)
            def _():
                start(0, 1 - slot)

        if j == 0:
            shift, scale = cond_ref[3:4, :], cond_ref[4:5, :]
            _norm_modulate_to(h_ref, x_ref, g1_ref, shift, scale, first_row=0,
                              n_rows=EDGE_ROWS, inline=True)
            accumulate(slot, slice(0, EDGE_ROWS), True)
            _norm_modulate_to(h_ref, x_ref, g1_ref, shift, scale, first_row=EDGE_ROWS,
                              n_rows=tm - EDGE_ROWS, inline=True)
            accumulate(slot, slice(EDGE_ROWS, tm), True)
        elif j + 1 < n_tiles:
            accumulate(slot, slice(None), False)
        else:
            half = tm // 2
            accumulate(slot, slice(0, half), False)
            finish(0, half)
            second = pl.multiple_of(jnp.minimum(step + 1, 1) * half, half)
            accumulate(slot, pl.ds(second, half), False)
            finish(half, half)


def _mlp(x, g1, g2, cond, w_up, w_dn, layer, tm=512, tf=2048, sub=1024):
    B, T, D = x.shape
    F = w_up.shape[1]
    n_tiles = F // tf
    assert T % tm == 0 and F % tf == 0 and tf % sub == 0
    assert n_tiles >= 2 and n_tiles % 2 == 0
    assert EDGE_ROWS < tm and tm % (2 * EDGE_ROWS) == 0
    return pl.pallas_call(
        functools.partial(_mlp_kernel, tf=tf, sub=sub),
        grid=(B, T // tm),
        in_specs=[
            pl.BlockSpec((None, tm, D), lambda b, i: (b, i, 0)),
            pl.BlockSpec((None, 1, D), lambda b, i: (layer, 0, 0)),
            pl.BlockSpec((None, 1, D), lambda b, i: (layer, 0, 0)),
            pl.BlockSpec((None, None, 6, D), lambda b, i: (layer, b, 0, 0)),
            pl.BlockSpec(memory_space=pl.ANY),
            pl.BlockSpec(memory_space=pl.ANY),
        ],
        out_specs=pl.BlockSpec((None, tm, D), lambda b, i: (b, i, 0)),
        out_shape=jax.ShapeDtypeStruct((B, T, D), F32),
        scratch_shapes=[
            pltpu.VMEM((tm, D), BF16),
            pltpu.VMEM((2, D, tf), BF16),
            pltpu.VMEM((2, tf, D), BF16),
            pltpu.SemaphoreType.DMA((2, 2)),
        ],
        compiler_params=_params(("arbitrary", "arbitrary")),
        name="mlp",
    )(x, g1, g2, cond, w_up, w_dn)


def kernel(x, c, w_ada, b_ada, g_pre_mix, g_post_mix, w_in, hg_lb_logits, g_hg_norm,
           w_proj_sb, w_proj_hg, w_out, g_pre_mlp, g_post_mlp, w_mlp_up, w_mlp_down):
    B, T, D = x.shape
    L = w_ada.shape[0]
    assert GATE_A_COL == (HG_G_BLK + HG_HEADS) * HEAD_DIM and GATE_B_COL == GATE_A_COL + D
    assert w_in.shape[1:] == (D, GATE_B_COL + D) and B <= 8

    lbs = jnp.cumsum(jax.nn.softmax(hg_lb_logits.astype(F32), axis=0), axis=0)
    lbs = lbs - lbs[0:1]
    lb_rows = jnp.stack([lbs, 1.0 - lbs], axis=1)

    c_pad = jnp.pad(c, ((0, 8 - B), (0, 0)))
    cond = _ada_cond(c_pad, w_ada, b_ada).reshape(L, 8, 6, D)

    g_pre_mix, g_post_mix, g_pre_mlp, g_post_mlp, g_hg_norm = (
        g.reshape(L, 1, -1) for g in (g_pre_mix, g_post_mix, g_pre_mlp, g_post_mlp, g_hg_norm))

    w_in_l = w_in[0].astype(BF16)

    for l in range(L):
        p = _in_proj(x, g_pre_mix, cond, w_in_l, l)
        y_sb = _sb_attention(p)
        casts = [(w, l) for w in (w_proj_sb, w_proj_hg, w_out, w_mlp_up, w_mlp_down)]
        casts += [(w_in, l + 1)] if l + 1 < L else []
        y_hg, w_sb_l, w_hg_l, w_out_l, w_up_l, w_dn_l, *w_in_next = _hgrn(
            p, lb_rows, g_hg_norm, casts, l)
        x = _mix_out(y_sb, y_hg, p, w_sb_l, w_hg_l, w_out_l, x, g_post_mix, cond, l)
        x = _mlp(x, g_pre_mlp, g_post_mlp, cond, w_up_l, w_dn_l, l)
        if w_in_next:
            w_in_l = w_in_next[0]
    return x
```
